```python
import jax, jax.numpy as jnp
from jax import lax
import numpy as np

D_MODEL = 2048
BATCH = 8
SEQ = 2048
DEPTH = 1
DEC_BATCH = 32
DEC_SEQ = 1
PAST_LEN = 8192
PAGE_SIZE = 128

N_HEADS = 16
HEAD_DIM = 64
KV_HEADS = 4
Q_PER_KV = N_HEADS // KV_HEADS
ATTN_WIDTH = N_HEADS * HEAD_DIM
KV_WIDTH = KV_HEADS * HEAD_DIM
CMP_LEN = 32
CMP_STRIDE = 16
CMP_SUB = CMP_LEN // CMP_STRIDE
SLC_BLOCK = 64
N_SELECT = 16
WINDOW = 512
Q_BLOCK = 32
FORCE_SCORE = 1e4
NEG_INF = -1e30
PM_WIDTH = D_MODEL - ATTN_WIDTH
POOL_WINDOWS = (2, 4, 8, 16)
N_PM_GROUPS = len(POOL_WINDOWS)
PM_GROUP = PM_WIDTH // N_PM_GROUPS
MAX_POOL_W = max(POOL_WINDOWS)
POOL_HIST = MAX_POOL_W - 1
SPLITS = (ATTN_WIDTH, ATTN_WIDTH + 2 * KV_WIDTH, ATTN_WIDTH + 4 * KV_WIDTH, ATTN_WIDTH + 6 * KV_WIDTH, ATTN_WIDTH + 6 * KV_WIDTH + 3 * N_HEADS)
IN_WIDTH = SPLITS[-1] + PM_WIDTH
N_EXPERTS = 32
TOP_K = 4
D_FF = D_MODEL
SWIGLU_LIMIT = 7.0
SWIGLU_ALPHA = 1.702
MOE_BLOCK = 128
EPS = 1e-6

kernel_name = 'nsa_pool_moe_hybrid_step'


def rms_norm(x, g):
    xf = x.astype(jnp.float32)
    y = xf * lax.rsqrt(jnp.mean(xf * xf, axis=-1, keepdims=True) + EPS)
    return (y * g.astype(jnp.float32)).astype(x.dtype)


def masked_softmax(s, mask):
    s = jnp.where(mask, s.astype(jnp.float32), NEG_INF)
    return jnp.where(mask, jax.nn.softmax(s, axis=-1), 0.0)


def alibi_slopes():
    h = jnp.arange(1, N_HEADS + 1, dtype=jnp.float32)
    return (2.0 ** (-8.0 * h / N_HEADS)).reshape(KV_HEADS, Q_PER_KV)


def cmp_to_slc(n_cmp, n_slc):
    i0 = jnp.arange(n_cmp)[:, None] * CMP_STRIDE
    j0 = jnp.arange(n_slc)[None, :] * SLC_BLOCK
    shared = jnp.minimum(i0 + CMP_LEN, j0 + SLC_BLOCK) - jnp.maximum(i0, j0)
    return jnp.clip(shared, 0, None).astype(jnp.float32) / CMP_LEN


def compress(kv, w_cmp_k, b_cmp_k, w_cmp_v, b_cmp_v, k_cmp_gain):
    b, s = kv.shape[:2]
    n_sub = s // CMP_STRIDE
    n_cmp = n_sub - CMP_SUB + 1
    sub = kv[:, :n_sub * CMP_STRIDE].reshape(b, n_sub, CMP_STRIDE, 2, KV_HEADS, HEAD_DIM)
    w = jnp.stack([w_cmp_k, w_cmp_v], 0).reshape(2, CMP_SUB, CMP_STRIDE, HEAD_DIM)
    acc = jnp.einsum('bnjcgd,cjd->bncgd', sub[:, 0:n_cmp], w[:, 0])
    for o in range(1, CMP_SUB):
        acc = acc + jnp.einsum('bnjcgd,cjd->bncgd', sub[:, o:o + n_cmp], w[:, o])
    acc = acc + jnp.stack([b_cmp_k, b_cmp_v], 0)[:, None, :]
    return rms_norm(acc[:, :, 0], k_cmp_gain), acc[:, :, 1]


def nsa_core(q, gate, q_pos, kc, vc, fetch_slc, n_slc, kw, vw, w_pos):
    b, tq = q.shape[:2]
    f32 = jnp.float32
    slopes = alibi_slopes()
    qg = q.reshape(b, tq, KV_HEADS, Q_PER_KV, HEAD_DIM)
    qf = q_pos.astype(f32)
    n_cmp = kc.shape[1]
    c_start = jnp.arange(n_cmp) * CMP_STRIDE
    c_mid = c_start.astype(f32) + 0.5 * (CMP_LEN - 1)
    s_c = jnp.einsum('bqgrd,bcgd->bgrqc', qg, kc).astype(f32) - slopes[:, :, None, None] * (qf[:, None] - c_mid[None, :])
    p_c = masked_softmax(s_c, (c_start[None, :] + CMP_LEN - 1) <= q_pos[:, None])
    o_c = jnp.einsum('bgrqc,bcgd->bqgrd', p_c.astype(vc.dtype), vc)
    imp = jnp.einsum('bgrqc,cn->bgqn', p_c, cmp_to_slc(n_cmp, n_slc))
    blk = jnp.arange(n_slc)[None, :]
    q_blk = (q_pos // SLC_BLOCK)[:, None]
    forced = (blk == 0) | (blk == q_blk) | (blk == q_blk - 1)
    score = jnp.where(forced, FORCE_SCORE, jnp.where(blk * SLC_BLOCK <= q_pos[:, None], imp, NEG_INF))
    n_sel = min(N_SELECT, n_slc)
    idx = lax.top_k(score, n_sel)[1].transpose(0, 2, 1, 3)
    kv_sel = fetch_slc(idx)
    dist_s = q_pos[None, :, None, None, None] - (idx[..., None] * SLC_BLOCK + jnp.arange(SLC_BLOCK))
    s_s = jnp.einsum('bqgrd,bqgnkd->bqgrnk', qg, kv_sel[..., 0, :]).astype(f32)
    s_s = s_s - slopes[:, :, None, None] * dist_s[:, :, :, None].astype(f32)
    n_keys = n_sel * SLC_BLOCK
    p_s = masked_softmax(s_s.reshape(b, tq, KV_HEADS, Q_PER_KV, n_keys), (dist_s >= 0).reshape(b, tq, KV_HEADS, 1, n_keys))
    v_sel = kv_sel[..., 1, :].reshape(b, tq, KV_HEADS, n_keys, HEAD_DIM)
    o_s = jnp.einsum('bqgrm,bqgmd->bqgrd', p_s.astype(v_sel.dtype), v_sel)
    dist_w = q_pos[:, None] - w_pos[None, :]
    s_w = jnp.einsum('bqgrd,bsgd->bgrqs', qg, kw).astype(f32) - slopes[:, :, None, None] * dist_w.astype(f32)
    p_w = masked_softmax(s_w, (dist_w >= 0) & (dist_w < WINDOW) & (w_pos >= 0)[None, :])
    o_w = jnp.einsum('bgrqs,bsgd->bqgrd', p_w.astype(vw.dtype), vw)
    g = gate.reshape(b, tq, KV_HEADS, Q_PER_KV, 3)
    o = g[..., 0:1] * o_c + g[..., 1:2] * o_s + g[..., 2:3] * o_w
    return o.reshape(b, tq, ATTN_WIDTH)


def pool_mix(u_ext, t0, n_new, w_pm, s_pm):
    b, L = u_ext.shape[:2]
    uf = u_ext.astype(jnp.float32).reshape(b, L, N_PM_GROUPS, PM_GROUP)
    csum = jnp.pad(jnp.cumsum(uf, axis=1), ((0, 0), (1 + MAX_POOL_W, 0), (0, 0), (0, 0)))
    off = MAX_POOL_W + L - n_new + 1
    tpos = t0 + L - n_new + jnp.arange(n_new)
    means = []
    for gi, w in enumerate(POOL_WINDOWS):
        s = csum[:, off:off + n_new, gi] - csum[:, off - w:off - w + n_new, gi]
        cnt = jnp.minimum(w, tpos + 1).astype(jnp.float32)
        means.append(s / cnt[None, :, None])
    d = (jnp.stack(means, axis=2) - uf[:, L - n_new:]).astype(u_ext.dtype)
    y = jnp.einsum('btgc,gcd->btgd', d, w_pm) * s_pm
    return y.reshape(b, n_new, PM_WIDTH)


def mixer_inputs(x, norm1_g, w_in, q_gain, k_slc_gain, k_win_gain):
    b, t = x.shape[:2]
    z = jnp.einsum('btd,de->bte', rms_norm(x, norm1_g), w_in)
    q, kv_c, kv_s, kv_w, g_logit, u = jnp.split(z, SPLITS, axis=-1)
    kv_shape = (b, t, 2, KV_HEADS, HEAD_DIM)
    q = rms_norm(q.reshape(b, t, N_HEADS, HEAD_DIM), q_gain) * (HEAD_DIM ** -0.5)
    kv_c = kv_c.reshape(kv_shape)
    kv_s = kv_s.reshape(kv_shape)
    kv_w = kv_w.reshape(kv_shape)
    kv_s = jnp.stack([rms_norm(kv_s[:, :, 0], k_slc_gain), kv_s[:, :, 1]], axis=2)
    kv_w = jnp.stack([rms_norm(kv_w[:, :, 0], k_win_gain), kv_w[:, :, 1]], axis=2)
    gate = jax.nn.sigmoid(g_logit.reshape(b, t, N_HEADS, 3))
    return q, gate, kv_c, kv_s, kv_w, u


def moe_ffn(h, w_router, b_router, w_gu, b_gu, w_down, b_down):
    shp = h.shape
    h = h.reshape(-1, shp[-1])
    n = h.shape[0]
    logits = jnp.einsum('nd,de->ne', h, w_router).astype(jnp.float32) + b_router.astype(jnp.float32)
    top_v, top_e = lax.top_k(logits, TOP_K)
    gates = jax.nn.softmax(top_v, axis=-1)
    n_assign = n * TOP_K
    e_flat = top_e.reshape(n_assign)
    order = jnp.argsort(e_flat)
    e_sorted = e_flat[order]
    tok_sorted = order // TOP_K
    g_sorted = gates.reshape(n_assign)[order]
    counts = jnp.bincount(e_flat, length=N_EXPERTS)
    grp_start = jnp.cumsum(counts) - counts
    padded = (counts + MOE_BLOCK - 1) // MOE_BLOCK * MOE_BLOCK
    pad_end = jnp.cumsum(padded)
    pad_start = pad_end - padded
    slot = pad_start[e_sorted] + jnp.arange(n_assign) - grp_start[e_sorted]
    n_blk = -(-n_assign // MOE_BLOCK) + N_EXPERTS
    x_buf = jnp.zeros((n_blk * MOE_BLOCK, shp[-1]), h.dtype).at[slot].set(h[tok_sorted])
    blk_start = jnp.arange(n_blk) * MOE_BLOCK
    blk_exp = jnp.minimum(jnp.searchsorted(pad_end, blk_start, side='right'), N_EXPERTS - 1)
    blk_used = blk_start < pad_end[-1]

    def expert_block(args):
        xb, e, used = args

        def run(xb):
            gu = xb @ w_gu[e] + b_gu[e]
            g_h = jnp.minimum(gu[:, :D_FF], SWIGLU_LIMIT)
            up = jnp.clip(gu[:, D_FF:], -SWIGLU_LIMIT, SWIGLU_LIMIT)
            act = (up + 1.0) * g_h * jax.nn.sigmoid(SWIGLU_ALPHA * g_h)
            return act @ w_down[e] + b_down[e]
        return lax.cond(used, run, jnp.zeros_like, xb)

    y_buf = lax.map(expert_block, (x_buf.reshape(n_blk, MOE_BLOCK, shp[-1]), blk_exp, blk_used))
    y_buf = y_buf.reshape(n_blk * MOE_BLOCK, shp[-1])
    y = jnp.zeros_like(h).at[tok_sorted].add(y_buf[slot] * g_sorted[:, None].astype(h.dtype))
    return y.reshape(shp)


def prompt_layer(x, norm1_g, w_in, q_gain, k_cmp_gain, k_slc_gain, k_win_gain, w_cmp_k, b_cmp_k, w_cmp_v, b_cmp_v,
                 w_pm, s_pm, w_out, norm2_g, w_router, b_router, w_gu, b_gu, w_down, b_down):
    b, t = x.shape[:2]
    q, gate, kv_c, kv_s, kv_w, u = mixer_inputs(x, norm1_g, w_in, q_gain, k_slc_gain, k_win_gain)
    kc, vc = compress(kv_c, w_cmp_k, b_cmp_k, w_cmp_v, b_cmp_v, k_cmp_gain)
    n_slc = -(-t // SLC_BLOCK)
    kv_s_pad = jnp.pad(kv_s, ((0, 0), (0, n_slc * SLC_BLOCK - t), (0, 0), (0, 0), (0, 0)))
    b_idx = jnp.arange(b)[:, None, None, None, None]
    g_idx = jnp.arange(KV_HEADS)[None, None, :, None, None]

    def fetch(idx):
        rows = idx[..., None] * SLC_BLOCK + jnp.arange(SLC_BLOCK)
        return kv_s_pad[b_idx, rows, :, g_idx, :]

    kv_w_pad = jnp.pad(kv_w, ((0, 0), (WINDOW, 0), (0, 0), (0, 0), (0, 0)))
    n_qb = t // Q_BLOCK
    q_blocks = q.reshape(b, n_qb, Q_BLOCK, N_HEADS, HEAD_DIM).swapaxes(0, 1)
    g_blocks = gate.reshape(b, n_qb, Q_BLOCK, N_HEADS, 3).swapaxes(0, 1)

    def block(args):
        q_i, g_i, i = args
        q0 = i * Q_BLOCK
        q_pos = q0 + jnp.arange(Q_BLOCK)
        kw = lax.dynamic_slice_in_dim(kv_w_pad, q0, Q_BLOCK + WINDOW, axis=1)
        w_pos = q0 - WINDOW + jnp.arange(Q_BLOCK + WINDOW)
        return nsa_core(q_i, g_i, q_pos, kc, vc, fetch, n_slc, kw[:, :, 0], kw[:, :, 1], w_pos)

    o_attn = lax.map(block, (q_blocks, g_blocks, jnp.arange(n_qb)))
    o_attn = o_attn.swapaxes(0, 1).reshape(b, t, ATTN_WIDTH)
    o_pm = pool_mix(u, 0, t, w_pm, s_pm)
    x = x + jnp.einsum('bte,ed->btd', jnp.concatenate([o_attn, o_pm.astype(o_attn.dtype)], axis=-1), w_out)
    x = x + moe_ffn(rms_norm(x, norm2_g), w_router, b_router, w_gu, b_gu, w_down, b_down)
    return x, (kv_c, kv_s, kv_w[:, t - min(WINDOW, t):], u[:, t - POOL_HIST:])


def sample_layer(x, cache_cmp_kv, cache_slc_kv, state_win_kv, state_pool, page_table,
                 norm1_g, w_in, q_gain, k_cmp_gain, k_slc_gain, k_win_gain, w_cmp_k, b_cmp_k, w_cmp_v, b_cmp_v,
                 w_pm, s_pm, w_out, norm2_g, w_router, b_router, w_gu, b_gu, w_down, b_down):
    b, t = x.shape[:2]
    n_pages = page_table.shape[1]
    past = n_pages * PAGE_SIZE
    q, gate, kv_c, kv_s, kv_w, u = mixer_inputs(x, norm1_g, w_in, q_gain, k_slc_gain, k_win_gain)
    past_c = cache_cmp_kv[page_table].reshape(b, past, 2, KV_HEADS, HEAD_DIM)
    kc, vc = compress(jnp.concatenate([past_c, kv_c], axis=1), w_cmp_k, b_cmp_k, w_cmp_v, b_cmp_v, k_cmp_gain)
    n_slc = -(-(past + t) // SLC_BLOCK)
    past_blks = past // SLC_BLOCK
    n_new_blks = n_slc - past_blks
    blk_per_page = PAGE_SIZE // SLC_BLOCK
    new_pad = jnp.pad(kv_s, ((0, 0), (0, n_new_blks * SLC_BLOCK - t), (0, 0), (0, 0), (0, 0)))
    b_idx4 = jnp.arange(b)[:, None, None, None]
    b_idx = jnp.arange(b)[:, None, None, None, None]
    g_idx = jnp.arange(KV_HEADS)[None, None, :, None, None]
    offs = jnp.arange(SLC_BLOCK)

    def fetch(idx):
        phys = page_table[b_idx4, jnp.clip(idx // blk_per_page, 0, n_pages - 1)]
        rows_p = (idx % blk_per_page)[..., None] * SLC_BLOCK + offs
        from_pool = cache_slc_kv[phys[..., None], rows_p, :, g_idx, :]
        rows_n = jnp.clip(idx - past_blks, 0, n_new_blks - 1)[..., None] * SLC_BLOCK + offs
        from_new = new_pad[b_idx, rows_n, :, g_idx, :]
        return jnp.where((idx >= past_blks)[..., None, None, None], from_new, from_pool)

    q_pos = past + jnp.arange(t)
    n_buf = state_win_kv.shape[1]
    kw = jnp.concatenate([state_win_kv, kv_w], axis=1)
    w_pos = past - n_buf + jnp.arange(n_buf + t)
    o_attn = nsa_core(q, gate, q_pos, kc, vc, fetch, n_slc, kw[:, :, 0], kw[:, :, 1], w_pos)
    u_ext = jnp.concatenate([state_pool, u], axis=1)
    o_pm = pool_mix(u_ext, past - POOL_HIST, t, w_pm, s_pm)
    x = x + jnp.einsum('bte,ed->btd', jnp.concatenate([o_attn, o_pm.astype(o_attn.dtype)], axis=-1), w_out)
    x = x + moe_ffn(rms_norm(x, norm2_g), w_router, b_router, w_gu, b_gu, w_down, b_down)
    return x, (kv_c, kv_s, kw[:, t:], u_ext[:, t:])


def setup_inputs(seed: int = 0) -> dict:
    keys = iter(jax.random.split(jax.random.key(seed), 40))

    def nrm(shape, scale):
        return jax.random.normal(next(keys), shape, jnp.float32) * scale

    def gain(shape, scale=0.02):
        return 1.0 + nrm(shape, scale)

    L = DEPTH
    n_pages = PAST_LEN // PAGE_SIZE
    n_used = DEC_BATCH * n_pages
    n_pool_pages = n_used + (n_used + 3) // 4
    win_buf = min(WINDOW, PAST_LEN)
    x_prompt = nrm((BATCH, SEQ, D_MODEL), 1.0)
    x_sample = nrm((DEC_BATCH, DEC_SEQ, D_MODEL), 1.0)
    cache_cmp_kv = nrm((L, n_pool_pages, PAGE_SIZE, 2, KV_HEADS, HEAD_DIM), 1.0)
    cache_slc_kv = nrm((L, n_pool_pages, PAGE_SIZE, 2, KV_HEADS, HEAD_DIM), 1.0)
    state_win_kv = nrm((L, DEC_BATCH, win_buf, 2, KV_HEADS, HEAD_DIM), 1.0)
    state_pool = nrm((L, DEC_BATCH, POOL_HIST, PM_WIDTH), 1.0)
    page_table = jax.random.permutation(next(keys), n_pool_pages)[:n_used].reshape(DEC_BATCH, n_pages).astype(jnp.int32)
    return {
        'x_prompt': x_prompt,
        'x_sample': x_sample,
        'cache_cmp_kv': cache_cmp_kv,
        'cache_slc_kv': cache_slc_kv,
        'state_win_kv': state_win_kv,
        'state_pool': state_pool,
        'page_table': page_table,
        'norm1_g': gain((L, D_MODEL)),
        'w_in': nrm((L, D_MODEL, IN_WIDTH), D_MODEL ** -0.5),
        'q_gain': gain((L, HEAD_DIM)),
        'k_cmp_gain': gain((L, HEAD_DIM)),
        'k_slc_gain': gain((L, HEAD_DIM)),
        'k_win_gain': gain((L, HEAD_DIM)),
        'w_cmp_k': nrm((L, CMP_LEN, HEAD_DIM), CMP_LEN ** -0.5),
        'b_cmp_k': nrm((L, HEAD_DIM), 0.02),
        'w_cmp_v': nrm((L, CMP_LEN, HEAD_DIM), CMP_LEN ** -0.5),
        'b_cmp_v': nrm((L, HEAD_DIM), 0.02),
        'w_pm': nrm((L, N_PM_GROUPS, PM_GROUP, PM_GROUP), PM_GROUP ** -0.5),
        's_pm': gain((L, N_PM_GROUPS, PM_GROUP), 0.1),
        'w_out': nrm((L, D_MODEL, D_MODEL), D_MODEL ** -0.5),
        'norm2_g': gain((L, D_MODEL)),
        'w_router': nrm((L, D_MODEL, N_EXPERTS), D_MODEL ** -0.5),
        'b_router': nrm((L, N_EXPERTS), 0.01),
        'w_gu': nrm((L, N_EXPERTS, D_MODEL, 2 * D_FF), D_MODEL ** -0.5),
        'b_gu': nrm((L, N_EXPERTS, 2 * D_FF), 0.02),
        'w_down': nrm((L, N_EXPERTS, D_FF, D_MODEL), D_FF ** -0.5),
        'b_down': nrm((L, N_EXPERTS, D_MODEL), 0.02),
    }


def reference(x_prompt, x_sample, cache_cmp_kv, cache_slc_kv, state_win_kv, state_pool, page_table,
              norm1_g, w_in, q_gain, k_cmp_gain, k_slc_gain, k_win_gain, w_cmp_k, b_cmp_k, w_cmp_v, b_cmp_v,
              w_pm, s_pm, w_out, norm2_g, w_router, b_router, w_gu, b_gu, w_down, b_down):
    layer_params = (norm1_g, w_in, q_gain, k_cmp_gain, k_slc_gain, k_win_gain, w_cmp_k, b_cmp_k, w_cmp_v, b_cmp_v,
                    w_pm, s_pm, w_out, norm2_g, w_router, b_router, w_gu, b_gu, w_down, b_down)
    y_prompt, y_sample = x_prompt, x_sample
    p_states, s_states = [], []
    for layer in range(DEPTH):
        lw = [w[layer] for w in layer_params]
        y_prompt, st_p = prompt_layer(y_prompt, *lw)
        p_states.append(st_p)
        y_sample, st_s = sample_layer(y_sample, cache_cmp_kv[layer], cache_slc_kv[layer], state_win_kv[layer],
                                      state_pool[layer], page_table, *lw)
        s_states.append(st_s)

    def stack(states, i):
        return jnp.stack([s[i] for s in states], axis=0)

    return (y_prompt, y_sample,
            stack(p_states, 0), stack(p_states, 1), stack(p_states, 2), stack(p_states, 3),
            stack(s_states, 0), stack(s_states, 1), stack(s_states, 2), stack(s_states, 3))
```

```python
import functools
import math

import jax
import jax.numpy as jnp
from jax import lax
from jax.experimental import pallas as pl
from jax.experimental.pallas import tpu as pltpu

F32 = jnp.float32
BF16 = jnp.bfloat16
I32 = jnp.int32

N_HEADS = 16
HEAD_DIM = 64
KV_HEADS = 4
Q_PER_KV = N_HEADS // KV_HEADS
ATTN_WIDTH = N_HEADS * HEAD_DIM
KV_WIDTH = KV_HEADS * HEAD_DIM
CMP_LEN = 32
CMP_STRIDE = 16
SLC_BLOCK = 64
N_SELECT = 16
WINDOW = 512
FORCE_SCORE = 1e4
NEG_INF = -1e30
POOL_WINDOWS = (2, 4, 8, 16)
MAX_POOL_W = max(POOL_WINDOWS)
POOL_HIST = MAX_POOL_W - 1
N_EXPERTS = 32
TOP_K = 4
SWIGLU_LIMIT = 7.0
SWIGLU_ALPHA = 1.702
EPS = 1e-6
PAGE_SIZE = 128

LANES = 128
VMEM_LIMIT = 56 * 1024 * 1024

GATE_PAD = LANES
SLOPES = [2.0 ** (-8.0 * (h + 1) / N_HEADS) for h in range(N_HEADS)]


def _cparams(sem):
    return pltpu.CompilerParams(dimension_semantics=sem, vmem_limit_bytes=VMEM_LIMIT)


def _iota(shape, dim):
    return lax.broadcasted_iota(I32, shape, dim)


def _split_bf16(x):
    hi = x.astype(BF16)
    lo = (x - hi.astype(F32)).astype(BF16)
    return hi, lo


def _dot(a, b):
    return jnp.dot(a, b, preferred_element_type=F32)


def _dot_t(a, b):
    return lax.dot_general(a, b, (((1,), (1,)), ((), ())), preferred_element_type=F32)


def _head_mean_sq(z):
    m, w = z.shape
    ones_bd = jnp.where(_iota((256, 256), 0) // HEAD_DIM == _iota((256, 256), 1) // HEAD_DIM, 1.0, 0.0).astype(BF16)
    zz = z * z
    hi, lo = _split_bf16(zz)
    parts = []
    for c in range(w // 256):
        sl = slice(c * 256, (c + 1) * 256)
        parts.append(_dot(hi[:, sl], ones_bd) + _dot(lo[:, sl], ones_bd))
    ss = parts[0] if len(parts) == 1 else jnp.concatenate(parts, axis=1)
    return ss * (1.0 / HEAD_DIM)


C_Q = 0
C_KVC = ATTN_WIDTH
C_KVS = C_KVC + 2 * KV_WIDTH
C_KVW = C_KVS + 2 * KV_WIDTH
C_U = C_KVW + 2 * KV_WIDTH


def _inproj_kernel(x_ref, g1_ref, w_ref, qg_ref, ksg_ref, kwg_ref, *refs, tm, pm_width, seq_tiles, hist_pos):
    if seq_tiles is None:
        sp_ref, refs = refs[0], refs[1:]
    q_out, kvc_out, kvs_out, kvw_out, kvsb_out, kvwb_out, gate_out, u_out, d_out = refs[:9]
    x = x_ref[...]
    ms = jnp.mean(x * x, axis=-1, keepdims=True)
    h = (x * lax.rsqrt(ms + EPS)) * g1_ref[...]
    hb = h.astype(BF16)
    c_gate = C_U + pm_width

    zq = _dot(hb, w_ref[:, C_Q:C_KVC])
    qn = (zq * lax.rsqrt(_head_mean_sq(zq) + EPS)) * qg_ref[...]
    q_out[...] = (qn * (HEAD_DIM ** -0.5)).astype(BF16)

    kvc_out[...] = _dot(hb, w_ref[:, C_KVC:C_KVS])

    zs = _dot(hb, w_ref[:, C_KVS:C_KVW])
    ks = zs[:, :KV_WIDTH]
    ks = (ks * lax.rsqrt(_head_mean_sq(ks) + EPS)) * ksg_ref[...]
    kvs = jnp.concatenate([ks, zs[:, KV_WIDTH:]], axis=1)
    kvs_out[...] = kvs
    kvsb_out[...] = kvs.astype(BF16)

    zw = _dot(hb, w_ref[:, C_KVW:C_U])
    kw = zw[:, :KV_WIDTH]
    kw = (kw * lax.rsqrt(_head_mean_sq(kw) + EPS)) * kwg_ref[...]
    kvw = jnp.concatenate([kw, zw[:, KV_WIDTH:]], axis=1)
    kvw_out[...] = kvw
    kvwb_out[...] = kvw.astype(BF16)

    gate_out[...] = jax.nn.sigmoid(_dot(hb, w_ref[:, c_gate:c_gate + GATE_PAD]))

    u = _dot(hb, w_ref[:, C_U:c_gate])
    u_out[...] = u

    pm_group = pm_width // len(POOL_WINDOWS)
    if seq_tiles is None:
        tpos = float(hist_pos + 1)
        for gi, w in enumerate(POOL_WINDOWS):
            cs = slice(gi * pm_group, (gi + 1) * pm_group)
            s = u[:, cs]
            for k in range(1, w):
                s = s + sp_ref[:, POOL_HIST - k, cs]
            d_out[:, cs] = (s / min(float(w), tpos) - u[:, cs]).astype(BF16)
    else:
        ext_ref = refs[9]
        j = pl.program_id(0) % seq_tiles

        @pl.when(j == 0)
        def _():
            ext_ref[0:MAX_POOL_W, :] = jnp.zeros((MAX_POOL_W, pm_width), F32)

        ext_ref[MAX_POOL_W:MAX_POOL_W + tm, :] = u
        tpos = (j * tm + _iota((tm, 1), 0) + 1).astype(F32)
        for gi, w in enumerate(POOL_WINDOWS):
            cs = slice(gi * pm_group, (gi + 1) * pm_group)
            s = ext_ref[MAX_POOL_W:MAX_POOL_W + tm, cs]
            for k in range(1, w):
                s = s + ext_ref[MAX_POOL_W - k:MAX_POOL_W - k + tm, cs]
            cnt = jnp.minimum(float(w), tpos)
            d_out[:, cs] = (s / cnt - u[:, cs]).astype(BF16)
        ext_ref[0:MAX_POOL_W, :] = ext_ref[tm:tm + MAX_POOL_W, :]


def _inproj(x2d, g1, w_packed, qg, ksg, kwg, *, tm, seq_len=None, pool_state=None, hist_pos=None):
    n, d_model = x2d.shape
    pm_width = d_model - ATTN_WIDTH
    seq_tiles = None if seq_len is None else seq_len // tm
    row = lambda w: pl.BlockSpec((tm, w), lambda i: (i, 0))
    full = lambda a: pl.BlockSpec(a.shape, lambda i: (0,) * a.ndim)
    out_shape = [
        jax.ShapeDtypeStruct((n, ATTN_WIDTH), BF16),
        jax.ShapeDtypeStruct((n, 2 * KV_WIDTH), F32),
        jax.ShapeDtypeStruct((n, 2 * KV_WIDTH), F32),
        jax.ShapeDtypeStruct((n, 2 * KV_WIDTH), F32),
        jax.ShapeDtypeStruct((n, 2 * KV_WIDTH), BF16),
        jax.ShapeDtypeStruct((n, 2 * KV_WIDTH), BF16),
        jax.ShapeDtypeStruct((n, GATE_PAD), F32),
        jax.ShapeDtypeStruct((n, pm_width), F32),
        jax.ShapeDtypeStruct((n, pm_width), BF16),
    ]
    out_specs = [row(ATTN_WIDTH), row(2 * KV_WIDTH), row(2 * KV_WIDTH), row(2 * KV_WIDTH),
                 row(2 * KV_WIDTH), row(2 * KV_WIDTH), row(GATE_PAD), row(pm_width), row(pm_width)]
    in_specs = [row(d_model), full(g1), full(w_packed), full(qg), full(ksg), full(kwg)]
    args = [x2d, g1, w_packed, qg, ksg, kwg]
    scratch = []
    if seq_tiles is None:
        in_specs.append(pl.BlockSpec((tm, POOL_HIST, pm_width), lambda i: (i, 0, 0)))
        args.append(pool_state)
    else:
        scratch.append(pltpu.VMEM((tm + MAX_POOL_W, pm_width), F32))
    return pl.pallas_call(
        functools.partial(_inproj_kernel, tm=tm, pm_width=pm_width, seq_tiles=seq_tiles, hist_pos=hist_pos),
        grid=(n // tm,),
        in_specs=in_specs,
        out_specs=out_specs,
        out_shape=out_shape,
        scratch_shapes=scratch,
        compiler_params=_cparams(("arbitrary",)),
        name="inproj",
    )(*args)


CMP_SUB_PER_PAGE = PAGE_SIZE // CMP_STRIDE


def _compress_kernel(pt_ref, *refs, pages_per_step):
    del pt_ref
    p = pages_per_step
    page_refs = refs[:p]
    halo_ref, w0_ref, w1_ref, b_ref, kg_ref, kc_out, vc_out, xs_ref = refs[p:]
    n_lt = 2 * KV_WIDTH // LANES
    nb = p * CMP_SUB_PER_PAGE
    parts = []
    for c in range(n_lt):
        cs = slice(c * LANES, (c + 1) * LANES)
        for k in range(p):
            xs_ref[c, k * PAGE_SIZE:(k + 1) * PAGE_SIZE, :] = page_refs[k][:, cs]
        xs_ref[c, p * PAGE_SIZE:p * PAGE_SIZE + CMP_STRIDE, :] = halo_ref[:, cs]
        acc = jnp.zeros((nb, LANES), F32) + b_ref[:, cs]
        for j in range(CMP_STRIDE):
            acc = acc + xs_ref[c, pl.ds(j, nb, stride=CMP_STRIDE), :] * w0_ref[j:j + 1, cs]
            acc = acc + xs_ref[c, pl.ds(CMP_STRIDE + j, nb, stride=CMP_STRIDE), :] * w1_ref[j:j + 1, cs]
        parts.append(acc)
    acc = jnp.concatenate(parts, axis=1)
    kc = acc[:, :KV_WIDTH]
    kc = (kc * lax.rsqrt(_head_mean_sq(kc) + EPS)) * kg_ref[...]
    kc_out[...] = kc.astype(BF16)
    vc_out[...] = acc[:, KV_WIDTH:].astype(BF16)


def _compress(page_table, pages, w0t, w1t, bias, kgain, *, pages_per_step):
    nb, npg = page_table.shape
    p = pages_per_step
    steps = npg // p

    def page_spec(k):
        return pl.BlockSpec((None, PAGE_SIZE, 2 * KV_WIDTH), lambda b, i, pt: (pt[b, i * p + k], 0, 0))

    halo_spec = pl.BlockSpec((None, CMP_STRIDE, 2 * KV_WIDTH),
                             lambda b, i, pt: (pt[b, jnp.minimum(i * p + p, npg - 1)], 0, 0))
    full = lambda a: pl.BlockSpec(a.shape, lambda b, i, pt: (0,) * a.ndim)
    out_spec = pl.BlockSpec((None, p * CMP_SUB_PER_PAGE, KV_WIDTH), lambda b, i, pt: (b, i, 0))
    grid_spec = pltpu.PrefetchScalarGridSpec(
        num_scalar_prefetch=1,
        grid=(nb, steps),
        in_specs=[page_spec(k) for k in range(p)] + [halo_spec, full(w0t), full(w1t), full(bias), full(kgain)],
        out_specs=[out_spec, out_spec],
        scratch_shapes=[pltpu.VMEM((2 * KV_WIDTH // LANES, p * PAGE_SIZE + CMP_STRIDE, LANES), F32)],
    )
    return pl.pallas_call(
        functools.partial(_compress_kernel, pages_per_step=p),
        grid_spec=grid_spec,
        out_shape=[jax.ShapeDtypeStruct((nb, npg * CMP_SUB_PER_PAGE, KV_WIDTH), BF16)] * 2,
        compiler_params=_cparams(("arbitrary", "arbitrary")),
        name="compress",
    )(page_table, *([pages] * p), pages, w0t, w1t, bias, kgain)


TQ = 128
TK = 256
QROWS = Q_PER_KV * TQ


def _select_members_t(score_t, n_cand):
    rows = _iota((n_cand, 1), 0)
    rank = jnp.zeros(score_t.shape, F32)
    for i in range(n_cand):
        si = score_t[i:i + 1, :]
        ahead = jnp.where(si > score_t, 1.0, jnp.where(si == score_t, jnp.where(rows > i, 1.0, 0.0), 0.0))
        rank = rank + ahead
    return jnp.where(rank < float(N_SELECT), 1.0, 0.0)


def _prompt_attn_kernel(q_ref, gate_ref, kc_ref, vc_ref, ks_ref, vs_ref, kw_ref, vw_ref, ct_ref, ex_ref,
                        o_ref, memb_ref, m_ref, l_ref, acc_ref, *, n_slc):
    qi = pl.program_id(1)
    q0 = qi * TQ
    lane_g = _iota((1, KV_WIDTH), 1) // HEAD_DIM
    row = _iota((QROWS, 1), 0)
    r_idx = row // TQ
    qpos_i = q0 + row % TQ
    qpos = qpos_i.astype(F32)
    n_cmp_pad = kc_ref.shape[0]
    c_start = _iota((1, n_cmp_pad), 1) * CMP_STRIDE
    c_end = (c_start + (CMP_LEN - 1)).astype(F32)
    c_mid = c_start.astype(F32) + 0.5 * (CMP_LEN - 1)
    gates = gate_ref[...]
    n_kt = memb_ref.shape[0]

    def flash(qpad, slope, k_ref, v_ref, lo, hi, mask_fn):
        m_ref[...] = jnp.full((QROWS, 1), NEG_INF, F32)
        l_ref[...] = jnp.zeros((QROWS, 1), F32)
        acc_ref[...] = jnp.zeros((QROWS, KV_WIDTH), F32)

        def body(kj, carry):
            k0 = pl.multiple_of(kj * TK, TK)
            kt = k_ref[pl.ds(k0, TK), :]
            vt = v_ref[pl.ds(k0, TK), :]
            kpos = (k0 + _iota((1, TK), 1)).astype(F32)
            s = _dot_t(qpad, kt) + slope * kpos
            s = mask_fn(s, kj, kpos)
            m_old = m_ref[...]
            m_new = jnp.maximum(m_old, jnp.max(s, axis=-1, keepdims=True))
            alpha = jnp.exp(m_old - m_new)
            p = jnp.exp(s - m_new)
            l_ref[...] = alpha * l_ref[...] + jnp.sum(p, axis=-1, keepdims=True)
            acc_ref[...] = alpha * acc_ref[...] + _dot(p.astype(BF16), vt)
            m_ref[...] = m_new
            return carry

        lax.fori_loop(lo, hi, body, 0)
        l = l_ref[...]
        return acc_ref[...] * jnp.where(l > 0.0, 1.0 / l, 0.0)

    outs = [jnp.zeros((TQ, KV_WIDTH), F32) for _ in range(Q_PER_KV)]
    for g in range(KV_HEADS):
        qpad = jnp.concatenate(
            [jnp.where(lane_g == g, q_ref[:, r * KV_WIDTH:(r + 1) * KV_WIDTH], jnp.zeros((), BF16))
             for r in range(Q_PER_KV)], axis=0)
        slope = jnp.zeros((QROWS, 1), F32)
        for r in range(Q_PER_KV):
            slope = jnp.where(r_idx == r, SLOPES[g * Q_PER_KV + r], slope)

        def gate_col(k):
            c = k * N_HEADS + g * Q_PER_KV
            return jnp.concatenate([gates[:, c + r:c + r + 1] for r in range(Q_PER_KV)], axis=0)

        s = _dot_t(qpad, kc_ref[...]) + slope * c_mid
        cmask = c_end <= qpos
        s = jnp.where(cmask, s, NEG_INF)
        e = jnp.where(cmask, jnp.exp(s - jnp.max(s, axis=-1, keepdims=True)), 0.0)
        l = jnp.sum(e, axis=-1, keepdims=True)
        pc = e * jnp.where(l > 0.0, 1.0 / l, 0.0)
        total = _dot(pc.astype(BF16), vc_ref[...]) * gate_col(0)

        p_sum = pc[0:TQ]
        for r in range(1, Q_PER_KV):
            p_sum = p_sum + pc[r * TQ:(r + 1) * TQ]
        hi, lo = _split_bf16(p_sum)
        imp_t = (_dot_t(ct_ref[...], hi) + _dot_t(ct_ref[...], lo))[0:n_slc]
        blk = _iota((n_slc, 1), 0)
        qpos_l = q0 + _iota((1, TQ), 1)
        qblk = qpos_l // SLC_BLOCK
        forced = (blk == 0) | (blk == qblk) | (blk == qblk - 1)
        score_t = jnp.where(forced, FORCE_SCORE, jnp.where(blk * SLC_BLOCK <= qpos_l, imp_t, NEG_INF))
        member_t = _select_members_t(score_t, n_slc)
        member_t = jnp.concatenate([member_t, jnp.zeros((LANES - n_slc, TQ), F32)], axis=0)
        member = member_t.T.astype(BF16)
        memb_keys = _dot(member, ex_ref[...])
        for j in range(n_kt):
            memb_ref[j] = memb_keys[:, j * TK:(j + 1) * TK]

        def slc_mask(s, kj, kpos):
            mk = memb_ref[kj]
            mk = jnp.concatenate([mk] * Q_PER_KV, axis=0)
            return jnp.where(mk > 0.5, jnp.where(kpos <= qpos, s, NEG_INF), NEG_INF)

        n_hi = (q0 + TQ + TK - 1) // TK
        total = total + flash(qpad, slope, ks_ref, vs_ref, 0, n_hi, slc_mask) * gate_col(1)

        def win_mask(s, kj, kpos):
            return jnp.where(kpos <= qpos, jnp.where(kpos > qpos - float(WINDOW), s, NEG_INF), NEG_INF)

        w_lo = jnp.maximum(q0 - WINDOW, 0) // TK
        total = total + flash(qpad, slope, kw_ref, vw_ref, w_lo, n_hi, win_mask) * gate_col(2)

        for r in range(Q_PER_KV):
            outs[r] = outs[r] + jnp.where(lane_g == g, total[r * TQ:(r + 1) * TQ], 0.0)

    for r in range(Q_PER_KV):
        o_ref[:, r * KV_WIDTH:(r + 1) * KV_WIDTH] = outs[r].astype(BF16)


def _prompt_attn(q, gate, kc, vc, kvs_b, kvw_b, ct, ex, *, batch, seq):
    n_slc = -(-seq // SLC_BLOCK)
    n_cmp_pad = kc.shape[1]
    n_qt = seq // TQ
    rowblk = lambda w: pl.BlockSpec((TQ, w), lambda b, i: (b * n_qt + i, 0))
    kvblk = lambda c: pl.BlockSpec((seq, KV_WIDTH), lambda b, i: (b, c))
    cblk = pl.BlockSpec((None, n_cmp_pad, KV_WIDTH), lambda b, i: (b, 0, 0))
    full = lambda a: pl.BlockSpec(a.shape, lambda b, i: (0,) * a.ndim)
    return pl.pallas_call(
        functools.partial(_prompt_attn_kernel, n_slc=n_slc),
        grid=(batch, n_qt),
        in_specs=[rowblk(ATTN_WIDTH), rowblk(GATE_PAD), cblk, cblk, kvblk(0), kvblk(1), kvblk(0), kvblk(1),
                  full(ct), full(ex)],
        out_specs=rowblk(ATTN_WIDTH),
        out_shape=jax.ShapeDtypeStruct((batch * seq, ATTN_WIDTH), BF16),
        scratch_shapes=[pltpu.VMEM((seq // TK, TQ, TK), F32),
                        pltpu.VMEM((QROWS, 1), F32), pltpu.VMEM((QROWS, 1), F32),
                        pltpu.VMEM((QROWS, KV_WIDTH), F32)],
        compiler_params=_cparams(("arbitrary", "arbitrary")),
        name="prompt_attn",
    )(q, gate, kc, vc, kvs_b, kvs_b, kvw_b, kvw_b, ct, ex)


DROWS = N_HEADS
NEVER = -3e38


def _decode_qpad(q_ref):
    lane_g = _iota((KV_HEADS, KV_WIDTH), 1) // HEAD_DIM
    row_g = _iota((KV_HEADS, KV_WIDTH), 0)
    parts = []
    for r in range(Q_PER_KV):
        qr = jnp.broadcast_to(q_ref[:, r * KV_WIDTH:(r + 1) * KV_WIDTH].astype(F32), (KV_HEADS, KV_WIDTH))
        parts.append(jnp.where(lane_g == row_g, qr, 0.0))
    return jnp.concatenate(parts, axis=0).astype(BF16)


def _decode_cmp_kernel(q_ref, slope_ref, kc_ref, vc_ref, c_ref, oc_out, member_out, *, qpos, n_slc):
    qpad = _decode_qpad(q_ref)
    slope = slope_ref[...]
    n_cmp_pad = kc_ref.shape[0]
    c_start = _iota((1, n_cmp_pad), 1) * CMP_STRIDE
    c_mid = c_start.astype(F32) + 0.5 * (CMP_LEN - 1)
    cmask = (c_start + (CMP_LEN - 1)) <= qpos
    s = _dot_t(qpad, kc_ref[...]) + slope * c_mid
    s = jnp.where(cmask, s, NEG_INF)
    e = jnp.where(cmask, jnp.exp(s - jnp.max(s, axis=-1, keepdims=True)), 0.0)
    l = jnp.sum(e, axis=-1, keepdims=True)
    pc = e * jnp.where(l > 0.0, 1.0 / l, 0.0)
    oc_out[...] = _dot(pc.astype(BF16), vc_ref[...])
    p_sum = pc[0:KV_HEADS]
    for r in range(1, Q_PER_KV):
        p_sum = p_sum + pc[r * KV_HEADS:(r + 1) * KV_HEADS]
    p_sum = jnp.concatenate([p_sum, jnp.zeros((DROWS - KV_HEADS, n_cmp_pad), F32)], axis=0)
    hi, lo = _split_bf16(p_sum)
    imp = _dot(hi, c_ref[...]) + _dot(lo, c_ref[...])
    n_pad = imp.shape[1]
    blk = _iota((1, n_pad), 1)
    qblk = qpos // SLC_BLOCK
    forced = (blk == 0) | (blk == qblk) | (blk == qblk - 1)
    score = jnp.where(forced, FORCE_SCORE, jnp.where(blk * SLC_BLOCK <= qpos, imp, NEG_INF))
    score = jnp.where(blk < n_slc, score, NEVER)
    rank = jnp.zeros(score.shape, F32)
    for i in range(n_slc):
        si = score[:, i:i + 1]
        rank = rank + jnp.where(si > score, 1.0, jnp.where(si == score, jnp.where(blk > i, 1.0, 0.0), 0.0))
    member_out[...] = jnp.where(rank < float(min(N_SELECT, n_slc)), 1.0, 0.0)


def _decode_cmp(q3, slopes, kc, vc, cmat, *, qpos, n_slc):
    nb = q3.shape[0]
    n_cmp_pad = kc.shape[1]
    n_pad = cmat.shape[1]
    per_b = lambda r, c: pl.BlockSpec((None, r, c), lambda b: (b, 0, 0))
    full = lambda a: pl.BlockSpec(a.shape, lambda b: (0,) * a.ndim)
    return pl.pallas_call(
        functools.partial(_decode_cmp_kernel, qpos=qpos, n_slc=n_slc),
        grid=(nb,),
        in_specs=[per_b(1, ATTN_WIDTH), full(slopes), per_b(n_cmp_pad, KV_WIDTH), per_b(n_cmp_pad, KV_WIDTH), full(cmat)],
        out_specs=[per_b(DROWS, KV_WIDTH), per_b(DROWS, n_pad)],
        out_shape=[jax.ShapeDtypeStruct((nb, DROWS, KV_WIDTH), F32), jax.ShapeDtypeStruct((nb, DROWS, n_pad), F32)],
        compiler_params=_cparams(("arbitrary",)),
        name="decode_cmp",
    )(q3, slopes, kc, vc, cmat)


def _decode_sw_kernel(pt_ref, q_ref, slope_ref, oc_ref, member_ref, gate_ref, gexp_ref, win_ref, news_ref, neww_ref,
                      *refs, pages_per_step, qpos, past, n_buf):
    del pt_ref
    p = pages_per_step
    page_refs = refs[:p]
    o_ref, m_ref, l_ref, acc_ref, ow_ref = refs[p:]
    c = pl.program_id(1)
    qpad = _decode_qpad(q_ref)
    qf = qpad.astype(F32)
    slope = slope_ref[...]

    def new_row(row_ref):
        kn = row_ref[:, :KV_WIDTH].astype(BF16).astype(F32)
        vn = row_ref[:, KV_WIDTH:].astype(BF16).astype(F32)
        return jnp.sum(qf * kn, axis=-1, keepdims=True) + slope * float(qpos), vn

    @pl.when(c == 0)
    def _():
        m_ref[...] = jnp.full((DROWS, 1), NEG_INF, F32)
        l_ref[...] = jnp.zeros((DROWS, 1), F32)
        acc_ref[...] = jnp.zeros((DROWS, KV_WIDTH), F32)
        kw = win_ref[:, :KV_WIDTH].astype(BF16)
        vw = win_ref[:, KV_WIDTH:].astype(BF16)
        wpos = past - n_buf + _iota((1, n_buf), 1)
        ok = (wpos <= qpos) & (wpos > qpos - WINDOW) & (wpos >= 0)
        s = jnp.where(ok, _dot_t(qpad, kw) + slope * wpos.astype(F32), NEG_INF)
        s_new, v_new = new_row(neww_ref)
        m = jnp.maximum(jnp.max(s, axis=-1, keepdims=True), s_new)
        e = jnp.where(ok, jnp.exp(s - m), 0.0)
        e_new = jnp.exp(s_new - m)
        l = jnp.sum(e, axis=-1, keepdims=True) + e_new
        ow_ref[...] = (_dot(e.astype(BF16), vw) + e_new * v_new) / l

    nk = p * PAGE_SIZE
    kt = jnp.concatenate([page_refs[k][:, :KV_WIDTH].astype(BF16) for k in range(p)], axis=0)
    vt = jnp.concatenate([page_refs[k][:, KV_WIDTH:].astype(BF16) for k in range(p)], axis=0)
    member = member_ref[0:KV_HEADS, :].astype(BF16)
    member = jnp.concatenate([member] * Q_PER_KV, axis=0)
    n_pad = member.shape[1]
    kidx = c * nk + _iota((n_pad, nk), 1)
    expand = jnp.where(kidx // SLC_BLOCK == _iota((n_pad, nk), 0), 1.0, 0.0).astype(BF16)
    mk = _dot(member, expand)
    kpos = (c * nk + _iota((1, nk), 1)).astype(F32)
    s = jnp.where(mk > 0.5, _dot_t(qpad, kt) + slope * kpos, NEG_INF)
    m_old = m_ref[...]
    m_new = jnp.maximum(m_old, jnp.max(s, axis=-1, keepdims=True))
    alpha = jnp.exp(m_old - m_new)
    pr = jnp.exp(s - m_new)
    l_ref[...] = alpha * l_ref[...] + jnp.sum(pr, axis=-1, keepdims=True)
    acc_ref[...] = alpha * acc_ref[...] + _dot(pr.astype(BF16), vt)
    m_ref[...] = m_new

    @pl.when(c == pl.num_programs(1) - 1)
    def _():
        new_blk = qpos // SLC_BLOCK
        is_member = jnp.concatenate([member_ref[0:KV_HEADS, new_blk:new_blk + 1]] * Q_PER_KV, axis=0) > 0.5
        s_new, v_new = new_row(news_ref)
        s_new = jnp.where(is_member, s_new, NEG_INF)
        m_old = m_ref[...]
        m_new = jnp.maximum(m_old, s_new)
        alpha = jnp.exp(m_old - m_new)
        e_new = jnp.where(is_member, jnp.exp(s_new - m_new), 0.0)
        l = alpha * l_ref[...] + e_new
        os = (alpha * acc_ref[...] + e_new * v_new) / l

        lane_g = _iota((1, KV_WIDTH), 1) // HEAD_DIM

        def flat(o):
            segs = []
            for r in range(Q_PER_KV):
                seg = jnp.zeros((1, KV_WIDTH), F32)
                for g in range(KV_HEADS):
                    i = r * KV_HEADS + g
                    seg = seg + jnp.where(lane_g == g, o[i:i + 1, :], 0.0)
                segs.append(seg)
            return jnp.concatenate(segs, axis=1)

        ghi, glo = _split_bf16(jnp.broadcast_to(gate_ref[...], (DROWS, GATE_PAD)))
        gx = (_dot(ghi, gexp_ref[...]) + _dot(glo, gexp_ref[...]))[0:1]
        o = (gx[:, 0:ATTN_WIDTH] * flat(oc_ref[...])
             + gx[:, ATTN_WIDTH:2 * ATTN_WIDTH] * flat(os)
             + gx[:, 2 * ATTN_WIDTH:] * flat(ow_ref[...]))
        o_ref[...] = o.astype(BF16)


def _decode_sw(page_table, q3, slopes, oc, member, gate3, gexp, win_state, new_s, new_w, pages,
               *, pages_per_step, qpos, past):
    nb, npg = page_table.shape
    p = pages_per_step
    n_buf = win_state.shape[1]
    n_pad = member.shape[2]
    per_b = lambda r, c: pl.BlockSpec((None, r, c), lambda b, i, pt: (b, 0, 0))
    full = lambda a: pl.BlockSpec(a.shape, lambda b, i, pt: (0,) * a.ndim)

    def page_spec(k):
        return pl.BlockSpec((None, PAGE_SIZE, 2 * KV_WIDTH), lambda b, i, pt: (pt[b, i * p + k], 0, 0))

    grid_spec = pltpu.PrefetchScalarGridSpec(
        num_scalar_prefetch=1,
        grid=(nb, npg // p),
        in_specs=[per_b(1, ATTN_WIDTH), full(slopes), per_b(DROWS, KV_WIDTH), per_b(DROWS, n_pad), per_b(1, GATE_PAD),
                  full(gexp), per_b(n_buf, 2 * KV_WIDTH), per_b(1, 2 * KV_WIDTH), per_b(1, 2 * KV_WIDTH)]
                 + [page_spec(k) for k in range(p)],
        out_specs=per_b(1, ATTN_WIDTH),
        scratch_shapes=[pltpu.VMEM((DROWS, 1), F32), pltpu.VMEM((DROWS, 1), F32),
                        pltpu.VMEM((DROWS, KV_WIDTH), F32), pltpu.VMEM((DROWS, KV_WIDTH), F32)],
    )
    return pl.pallas_call(
        functools.partial(_decode_sw_kernel, pages_per_step=p, qpos=qpos, past=past, n_buf=n_buf),
        grid_spec=grid_spec,
        out_shape=jax.ShapeDtypeStruct((nb, 1, ATTN_WIDTH), BF16),
        compiler_params=_cparams(("arbitrary", "arbitrary")),
        name="decode_slc_win",
    )(page_table, q3, slopes, oc, member, gate3, gexp, win_state, new_s, new_w, *([pages] * p))


def _outproj_kernel(x_ref, oa_ref, d_ref, wpm_ref, spm_ref, woa_ref, wop_ref, g2_ref, wrh_ref, wrl_ref, br_ref,
                    x1_out, h2_out, tope_out, gate_out):
    n_grp, pm_group = wpm_ref.shape[0], wpm_ref.shape[1]
    pm = jnp.concatenate([_dot(d_ref[:, gi * pm_group:(gi + 1) * pm_group], wpm_ref[gi]) for gi in range(n_grp)], axis=1)
    pm = pm * spm_ref[...]
    x1 = x_ref[...] + _dot(oa_ref[...], woa_ref[...]) + _dot(pm.astype(BF16), wop_ref[...])
    x1_out[...] = x1
    ms = jnp.mean(x1 * x1, axis=-1, keepdims=True)
    h2 = (x1 * lax.rsqrt(ms + EPS)) * g2_ref[...]
    h2_out[...] = h2
    hi, lo = _split_bf16(h2)
    logits = _dot(hi, wrh_ref[...]) + _dot(lo, wrh_ref[...]) + _dot(hi, wrl_ref[...]) + br_ref[...]
    tm = logits.shape[0]
    lane = _iota((tm, LANES), 1)
    logits = jnp.where(lane < N_EXPERTS, logits, NEVER)
    vals, idxs = [], []
    for _ in range(TOP_K):
        m = jnp.max(logits, axis=-1, keepdims=True)
        idx = jnp.min(jnp.where(logits == m, lane, LANES), axis=-1, keepdims=True)
        vals.append(m)
        idxs.append(idx)
        logits = jnp.where(lane == idx, NEVER, logits)
    es = [jnp.exp(v - vals[0]) for v in vals]
    den = es[0]
    for e in es[1:]:
        den = den + e
    tope = jnp.full((tm, LANES), -1, I32)
    gts = jnp.zeros((tm, LANES), F32)
    for k in range(TOP_K):
        tope = jnp.where(lane == k, idxs[k], tope)
        gts = jnp.where(lane == k, es[k] / den, gts)
    tope_out[...] = tope
    gate_out[...] = gts


def _outproj(x2d, o_attn, dpool, wpm, spm, woa, wop, g2, wrh, wrl, br, *, tm):
    n, d_model = x2d.shape
    row = lambda w: pl.BlockSpec((tm, w), lambda i: (i, 0))
    full = lambda a: pl.BlockSpec(a.shape, lambda i: (0,) * a.ndim)
    consts = [wpm, spm, woa, wop, g2, wrh, wrl, br]
    return pl.pallas_call(
        _outproj_kernel,
        grid=(n // tm,),
        in_specs=[row(d_model), row(o_attn.shape[1]), row(dpool.shape[1])] + [full(a) for a in consts],
        out_specs=[row(d_model), row(d_model), row(LANES), row(LANES)],
        out_shape=[jax.ShapeDtypeStruct((n, d_model), F32), jax.ShapeDtypeStruct((n, d_model), F32),
                   jax.ShapeDtypeStruct((n, LANES), I32), jax.ShapeDtypeStruct((n, LANES), F32)],
        compiler_params=_cparams(("arbitrary",)),
        name="outproj",
    )(x2d, o_attn, dpool, *consts)


MOE_ROWS = 1024
MOE_SUB = 256
MOE_FC = 256
ROUTE_TILE = 512


def _route_kernel(e_ref, rank_out, cnt_out, carry_ref):
    @pl.when(pl.program_id(0) == 0)
    def _():
        carry_ref[...] = jnp.zeros(carry_ref.shape, F32)

    tr = e_ref.shape[0]
    lane = _iota((tr, LANES), 1)
    e = e_ref[...]
    ohs = [jnp.where(e[:, k:k + 1] == lane, 1.0, 0.0) for k in range(TOP_K)]
    tot = ohs[0]
    for oh in ohs[1:]:
        tot = tot + oh
    lower = jnp.where(_iota((tr, tr), 1) < _iota((tr, tr), 0), 1.0, 0.0).astype(BF16)
    before = _dot(lower, tot.astype(BF16)) + carry_ref[...]
    rank = jnp.zeros((tr, LANES), I32)
    for k in range(TOP_K):
        rk = jnp.sum(ohs[k] * before, axis=-1, keepdims=True).astype(I32)
        rank = jnp.where(lane == k, rk, rank)
    rank_out[...] = rank
    carry_ref[...] = carry_ref[...] + jnp.sum(tot, axis=0, keepdims=True)
    cnt_out[...] = carry_ref[...]


def _route(tope):
    n = tope.shape[0]
    return pl.pallas_call(
        _route_kernel,
        grid=(n // ROUTE_TILE,),
        in_specs=[pl.BlockSpec((ROUTE_TILE, LANES), lambda i: (i, 0))],
        out_specs=[pl.BlockSpec((ROUTE_TILE, LANES), lambda i: (i, 0)), pl.BlockSpec((1, LANES), lambda i: (0, 0))],
        out_shape=[jax.ShapeDtypeStruct((n, LANES), I32), jax.ShapeDtypeStruct((1, LANES), F32)],
        scratch_shapes=[pltpu.VMEM((1, LANES), F32)],
        compiler_params=_cparams(("arbitrary",)),
        name="moe_route",
    )(tope)


def _dispatch_kernel(slot_ref, h_ref, *refs):
    xbuf, sem = refs[-2], refs[-1]
    tr = h_ref.shape[0]

    def row_copy(i, s):
        return pltpu.make_async_copy(h_ref.at[pl.ds(i, 1)], xbuf.at[pl.ds(s, 1)], sem)

    def issue(i, carry):
        for k in range(TOP_K):
            row_copy(i, slot_ref[i * TOP_K + k]).start()
        return carry

    lax.fori_loop(0, tr, issue, 0)

    def drain(i, carry):
        for k in range(TOP_K):
            row_copy(i, slot_ref[i * TOP_K + k]).wait()
        return carry

    lax.fori_loop(0, tr, drain, 0)


def _dispatch(slots_flat, h2, xbuf, n_rows, *, tr):
    n, d = h2.shape
    in_specs = [pl.BlockSpec((tr * TOP_K,), lambda i: (i,), memory_space=pltpu.SMEM),
                pl.BlockSpec((tr, d), lambda i: (i, 0))]
    args = [slots_flat, h2]
    aliases = {}
    if xbuf is not None:
        in_specs.append(pl.BlockSpec(memory_space=pl.ANY))
        args.append(xbuf)
        aliases = {2: 0}
    return pl.pallas_call(
        _dispatch_kernel,
        grid=(n // tr,),
        in_specs=in_specs,
        out_specs=pl.BlockSpec(memory_space=pl.ANY),
        out_shape=jax.ShapeDtypeStruct((n_rows, d), F32),
        scratch_shapes=[pltpu.SemaphoreType.DMA(())],
        input_output_aliases=aliases,
        compiler_params=pltpu.CompilerParams(dimension_semantics=("arbitrary",), vmem_limit_bytes=VMEM_LIMIT,
                                             has_side_effects=True),
        name="moe_dispatch",
    )(*args)


def _experts_kernel(we_ref, wr_ref, wb_ref, x_ref, wg_ref, wu_ref, bg_ref, bu_ref, wd_ref, bd_ref, o_ref, xb_ref):
    del we_ref, wb_ref
    w = pl.program_id(0)
    c = pl.program_id(1)
    rows = wr_ref[w]

    @pl.when(rows > 0)
    def _():
        @pl.when(c == 0)
        def _():
            ridx = _iota((MOE_ROWS, 1), 0)
            xb_ref[...] = jnp.where(ridx < rows, x_ref[...], 0.0).astype(BF16)
            o_ref[...] = jnp.broadcast_to(bd_ref[...], o_ref.shape)

        wg = wg_ref[...].astype(BF16)
        wu = wu_ref[...].astype(BF16)
        wd = wd_ref[...].astype(BF16)
        for i in range(MOE_ROWS // MOE_SUB):
            @pl.when(i * MOE_SUB < rows)
            def _():
                rs = slice(i * MOE_SUB, (i + 1) * MOE_SUB)
                xs = xb_ref[rs, :]
                g = _dot(xs, wg) + bg_ref[...]
                u = _dot(xs, wu) + bu_ref[...]
                gh = jnp.minimum(g, SWIGLU_LIMIT)
                up = jnp.clip(u, -SWIGLU_LIMIT, SWIGLU_LIMIT)
                act = (up + 1.0) * gh * jax.nn.sigmoid(SWIGLU_ALPHA * gh)
                o_ref[rs, :] = o_ref[rs, :] + _dot(act.astype(BF16), wd)


def _experts(work_e, work_rows, work_blk, xbuf, w_gu, b_gu3, w_down, b_down3):
    n_work = work_e.shape[0]
    n_exp, d_model, two_ff = w_gu.shape
    d_ff = two_ff // 2
    nc = d_ff // MOE_FC

    def cidx(w, c, wr):
        return jnp.where(wr[w] > 0, c, nc - 1)

    grid_spec = pltpu.PrefetchScalarGridSpec(
        num_scalar_prefetch=3,
        grid=(n_work, nc),
        in_specs=[
            pl.BlockSpec((MOE_ROWS, d_model), lambda w, c, we, wr, wb: (wb[w], 0)),
            pl.BlockSpec((None, d_model, MOE_FC), lambda w, c, we, wr, wb: (we[w], 0, cidx(w, c, wr))),
            pl.BlockSpec((None, d_model, MOE_FC), lambda w, c, we, wr, wb: (we[w], 0, nc + cidx(w, c, wr))),
            pl.BlockSpec((None, 1, MOE_FC), lambda w, c, we, wr, wb: (we[w], 0, cidx(w, c, wr))),
            pl.BlockSpec((None, 1, MOE_FC), lambda w, c, we, wr, wb: (we[w], 0, nc + cidx(w, c, wr))),
            pl.BlockSpec((None, MOE_FC, d_model), lambda w, c, we, wr, wb: (we[w], cidx(w, c, wr), 0)),
            pl.BlockSpec((None, 1, d_model), lambda w, c, we, wr, wb: (we[w], 0, 0)),
        ],
        out_specs=pl.BlockSpec((MOE_ROWS, d_model), lambda w, c, we, wr, wb: (wb[w], 0)),
        scratch_shapes=[pltpu.VMEM((MOE_ROWS, d_model), BF16)],
    )
    return pl.pallas_call(
        _experts_kernel,
        grid_spec=grid_spec,
        out_shape=jax.ShapeDtypeStruct(xbuf.shape, F32),
        compiler_params=_cparams(("arbitrary", "arbitrary")),
        name="moe_experts",
    )(work_e, work_rows, work_blk, xbuf, w_gu, w_gu, b_gu3, b_gu3, w_down, b_down3)


def _combine_kernel(slot_ref, x1_ref, gate_ref, ybuf, o_ref, rows_ref, sem):
    tc = x1_ref.shape[0]

    def row_copy(i, k):
        return pltpu.make_async_copy(ybuf.at[pl.ds(slot_ref[i * TOP_K + k], 1)], rows_ref.at[k, pl.ds(i, 1)], sem)

    def issue(i, carry):
        for k in range(TOP_K):
            row_copy(i, k).start()
        return carry

    lax.fori_loop(0, tc, issue, 0)

    def drain(i, carry):
        for k in range(TOP_K):
            row_copy(i, k).wait()
        return carry

    lax.fori_loop(0, tc, drain, 0)
    gates = gate_ref[...]
    y = x1_ref[...]
    for k in range(TOP_K):
        y = y + gates[:, k:k + 1] * rows_ref[k]
    o_ref[...] = y


def _combine(slots_flat, x1, gates, ybuf, *, tc):
    n, d = x1.shape
    return pl.pallas_call(
        _combine_kernel,
        grid=(n // tc,),
        in_specs=[pl.BlockSpec((tc * TOP_K,), lambda i: (i,), memory_space=pltpu.SMEM),
                  pl.BlockSpec((tc, d), lambda i: (i, 0)),
                  pl.BlockSpec((tc, LANES), lambda i: (i, 0)),
                  pl.BlockSpec(memory_space=pl.ANY)],
        out_specs=pl.BlockSpec((tc, d), lambda i: (i, 0)),
        out_shape=jax.ShapeDtypeStruct((n, d), F32),
        scratch_shapes=[pltpu.VMEM((TOP_K, tc, d), F32), pltpu.SemaphoreType.DMA(())],
        compiler_params=_cparams(("arbitrary",)),
        name="moe_combine",
    )(slots_flat, x1, gates, ybuf)


def _moe(h2_p, h2_s, tope_p, tope_s, gate_p, gate_s, x1_p, x1_s, w_gu, b_gu, w_down, b_down):
    n_p, n_s = h2_p.shape[0], h2_s.shape[0]
    n_exp = w_gu.shape[0]
    pad = (-(n_p + n_s)) % ROUTE_TILE
    tope_all = jnp.concatenate([tope_p, tope_s, jnp.full((pad, LANES), -1, I32)], axis=0)
    rank, counts = _route(tope_all)
    counts = counts[0, :n_exp].astype(I32)
    n_assign = (n_p + n_s) * TOP_K
    n_work = n_assign // MOE_ROWS + n_exp
    items = (counts + MOE_ROWS - 1) // MOE_ROWS
    item_end = jnp.cumsum(items)
    item_start = item_end - items
    n_used = item_end[-1]
    w_ids = jnp.arange(n_work, dtype=I32)
    used = w_ids < n_used
    w_eff = jnp.where(used, w_ids, n_used - 1)
    work_e = jnp.minimum(jnp.searchsorted(item_end, w_eff, side="right"), n_exp - 1).astype(I32)
    work_rows = jnp.clip(counts[work_e] - (w_eff - item_start[work_e]) * MOE_ROWS, 0, MOE_ROWS)
    work_rows = jnp.where(used, work_rows, 0).astype(I32)
    base = (item_start * MOE_ROWS).astype(I32)
    e_all = tope_all[:n_p + n_s, :TOP_K]
    slots = (base[e_all] + rank[:n_p + n_s, :TOP_K]).reshape(-1)
    slots_p, slots_s = slots[:n_p * TOP_K], slots[n_p * TOP_K:]
    n_rows = n_work * MOE_ROWS
    xbuf = _dispatch(slots_p, h2_p, None, n_rows, tr=512)
    xbuf = _dispatch(slots_s, h2_s, xbuf, n_rows, tr=n_s)
    ybuf = _experts(work_e, work_rows, w_eff.astype(I32), xbuf, w_gu, b_gu[:, None, :], w_down, b_down[:, None, :])
    y_p = _combine(slots_p, x1_p, gate_p, ybuf, tc=128)
    y_s = _combine(slots_s, x1_s, gate_s, ybuf, tc=n_s)
    return y_p, y_s


def _head_perm():
    return [g * Q_PER_KV + r for r in range(Q_PER_KV) for g in range(KV_HEADS)]


def _pack_w_in(w_in):
    d_model = w_in.shape[0]
    pm_width = d_model - ATTN_WIDTH
    s0 = ATTN_WIDTH
    s1 = s0 + 2 * KV_WIDTH
    s2 = s1 + 2 * KV_WIDTH
    s3 = s2 + 2 * KV_WIDTH
    s4 = s3 + 3 * N_HEADS
    wq = w_in[:, :s0].reshape(d_model, N_HEADS, HEAD_DIM)[:, jnp.array(_head_perm())].reshape(d_model, ATTN_WIDTH)
    wg = w_in[:, s3:s4].reshape(d_model, N_HEADS, 3).transpose(0, 2, 1).reshape(d_model, 3 * N_HEADS)
    wg = jnp.pad(wg, ((0, 0), (0, GATE_PAD - 3 * N_HEADS)))
    return jnp.concatenate([wq, w_in[:, s0:s3], w_in[:, s4:s4 + pm_width], wg], axis=1).astype(BF16)


def _cmp_weights(w_cmp_k, b_cmp_k, w_cmp_v, b_cmp_v):
    def half(o):
        wk = jnp.tile(w_cmp_k[o * CMP_STRIDE:(o + 1) * CMP_STRIDE], (1, KV_HEADS))
        wv = jnp.tile(w_cmp_v[o * CMP_STRIDE:(o + 1) * CMP_STRIDE], (1, KV_HEADS))
        return jnp.concatenate([wk, wv], axis=1)
    bias = jnp.concatenate([jnp.tile(b_cmp_k, KV_HEADS), jnp.tile(b_cmp_v, KV_HEADS)])[None, :]
    return half(0), half(1), bias


def _cmp_to_slc_t(n_cmp_pad, n_cmp, n_slc, n_slc_pad):
    i0 = jnp.arange(n_cmp_pad)[None, :] * CMP_STRIDE
    j0 = jnp.arange(n_slc_pad)[:, None] * SLC_BLOCK
    shared = jnp.minimum(i0 + CMP_LEN, j0 + SLC_BLOCK) - jnp.maximum(i0, j0)
    frac = jnp.clip(shared, 0, None).astype(F32) / CMP_LEN
    ok = (jnp.arange(n_cmp_pad)[None, :] < n_cmp) & (jnp.arange(n_slc_pad)[:, None] < n_slc)
    return jnp.where(ok, frac, 0.0).astype(BF16)


def _block_expand(n_blk_pad, n_keys):
    return (jnp.arange(n_blk_pad)[:, None] == (jnp.arange(n_keys)[None, :] // SLC_BLOCK)).astype(BF16)


def _prompt_mixer(x_prompt, p):
    batch, seq, d_model = x_prompt.shape
    n = batch * seq
    q, kvc, kvs, kvw, kvs_b, kvw_b, gate, u, dpool = _inproj(
        x_prompt.reshape(n, d_model), p["g1"], p["w_in"], p["qg"], p["ksg"], p["kwg"], tm=256, seq_len=seq)
    npg = seq // PAGE_SIZE
    pt = (jnp.arange(batch, dtype=I32)[:, None] * npg + jnp.arange(npg, dtype=I32)[None, :])
    kc, vc = _compress(pt, kvc.reshape(batch * npg, PAGE_SIZE, 2 * KV_WIDTH), p["cw0"], p["cw1"], p["cb"], p["kcg"],
                       pages_per_step=8)
    n_cmp = seq // CMP_STRIDE - 1
    n_slc = -(-seq // SLC_BLOCK)
    ct = _cmp_to_slc_t(kc.shape[1], n_cmp, n_slc, LANES)
    ex = _block_expand(LANES, seq)
    o_attn = _prompt_attn(q, gate, kc, vc, kvs_b, kvw_b, ct, ex, batch=batch, seq=seq)
    return o_attn, dpool, kvc, kvs, kvw, u


def _sample_mixer(x_sample, cache_cmp, cache_slc, state_win, state_pool, page_table, p):
    nb, t, d_model = x_sample.shape
    assert t == 1, "decode path handles one new row per sequence"
    npg = page_table.shape[1]
    past = npg * PAGE_SIZE
    qpos = past
    q, kvc, kvs, kvw, _, _, gate, u, dpool = _inproj(
        x_sample.reshape(nb, d_model), p["g1"], p["w_in"], p["qg"], p["ksg"], p["kwg"], tm=nb,
        pool_state=state_pool, hist_pos=qpos)
    n_pool = cache_cmp.shape[0]
    kc, vc = _compress(page_table, cache_cmp.reshape(n_pool, PAGE_SIZE, 2 * KV_WIDTH), p["cw0"], p["cw1"], p["cb"],
                       p["kcg"], pages_per_step=8)
    n_cmp = (past + t) // CMP_STRIDE - 1
    n_slc = -(-(past + t) // SLC_BLOCK)
    n_pad = -(-n_slc // LANES) * LANES
    cmat = _cmp_to_slc_t(kc.shape[1], n_cmp, n_slc, n_pad).T
    slopes = jnp.array([SLOPES[g * Q_PER_KV + r] for r in range(Q_PER_KV) for g in range(KV_HEADS)], F32)[:, None]
    q3 = q.reshape(nb, 1, ATTN_WIDTH)
    oc, member = _decode_cmp(q3, slopes, kc, vc, cmat, qpos=qpos, n_slc=n_slc)
    rows = jnp.arange(3 * N_HEADS)
    k_i, g_i, r_i = rows // N_HEADS, (rows % N_HEADS) // Q_PER_KV, rows % Q_PER_KV
    col_head = k_i * N_HEADS + r_i * KV_HEADS + g_i
    gexp = (jnp.arange(3 * ATTN_WIDTH)[None, :] // HEAD_DIM == col_head[:, None])
    gexp = jnp.pad(gexp, ((0, GATE_PAD - 3 * N_HEADS), (0, 0))).astype(BF16)
    o = _decode_sw(page_table, q3, slopes, oc, member, gate.reshape(nb, 1, GATE_PAD), gexp,
                   state_win.reshape(nb, state_win.shape[1], 2 * KV_WIDTH),
                   kvs.reshape(nb, 1, 2 * KV_WIDTH), kvw.reshape(nb, 1, 2 * KV_WIDTH),
                   cache_slc.reshape(n_pool, PAGE_SIZE, 2 * KV_WIDTH), pages_per_step=8, qpos=qpos, past=past)
    return o.reshape(nb, ATTN_WIDTH), dpool, kvc, kvs, kvw, u


def _prep_params(norm1_g, w_in, q_gain, k_cmp_gain, k_slc_gain, k_win_gain, w_cmp_k, b_cmp_k, w_cmp_v, b_cmp_v):
    cw0, cw1, cb = _cmp_weights(w_cmp_k, b_cmp_k, w_cmp_v, b_cmp_v)
    return {
        "g1": norm1_g[None, :],
        "w_in": _pack_w_in(w_in),
        "qg": jnp.tile(q_gain, N_HEADS)[None, :],
        "ksg": jnp.tile(k_slc_gain, KV_HEADS)[None, :],
        "kwg": jnp.tile(k_win_gain, KV_HEADS)[None, :],
        "kcg": jnp.tile(k_cmp_gain, KV_HEADS)[None, :],
        "cw0": cw0, "cw1": cw1, "cb": cb,
    }


def _prep_out_params(w_pm, s_pm, w_out, norm2_g, w_router, b_router):
    d_model = w_out.shape[0]
    woa = w_out[:ATTN_WIDTH].reshape(N_HEADS, HEAD_DIM, d_model)[jnp.array(_head_perm())].reshape(ATTN_WIDTH, d_model)
    wr = jnp.pad(w_router, ((0, 0), (0, LANES - w_router.shape[1])))
    wrh, wrl = _split_bf16(wr)
    return (w_pm.astype(BF16), s_pm.reshape(1, -1), woa.astype(BF16), w_out[ATTN_WIDTH:].astype(BF16),
            norm2_g[None, :], wrh, wrl, jnp.pad(b_router, (0, LANES - b_router.shape[0]))[None, :])


def _layer(x_prompt, x_sample, cache_cmp, cache_slc, state_win, state_pool, page_table,
           norm1_g, w_in, q_gain, k_cmp_gain, k_slc_gain, k_win_gain, w_cmp_k, b_cmp_k, w_cmp_v, b_cmp_v,
           w_pm, s_pm, w_out, norm2_g, w_router, b_router, w_gu, b_gu, w_down, b_down):
    batch, seq, d_model = x_prompt.shape
    nb, t = x_sample.shape[:2]
    p = _prep_params(norm1_g, w_in, q_gain, k_cmp_gain, k_slc_gain, k_win_gain, w_cmp_k, b_cmp_k, w_cmp_v, b_cmp_v)
    oa_p, d_p, kvc_p, kvs_p, kvw_p, u_p = _prompt_mixer(x_prompt, p)
    oa_s, d_s, kvc_s, kvs_s, kvw_s, u_s = _sample_mixer(x_sample, cache_cmp, cache_slc, state_win, state_pool, page_table, p)
    op = _prep_out_params(w_pm, s_pm, w_out, norm2_g, w_router, b_router)
    x1_p, h2_p, te_p, gt_p = _outproj(x_prompt.reshape(batch * seq, d_model), oa_p, d_p, *op, tm=256)
    x1_s, h2_s, te_s, gt_s = _outproj(x_sample.reshape(nb * t, d_model), oa_s, d_s, *op, tm=nb * t)
    y_p, y_s = _moe(h2_p, h2_s, te_p, te_s, gt_p, gt_s, x1_p, x1_s, w_gu, b_gu, w_down, b_down)
    kv_shape = (2, KV_HEADS, HEAD_DIM)
    n_win = min(WINDOW, seq)
    st_p = (kvc_p.reshape(batch, seq, *kv_shape), kvs_p.reshape(batch, seq, *kv_shape),
            kvw_p.reshape(batch, seq, *kv_shape)[:, seq - n_win:], u_p.reshape(batch, seq, -1)[:, seq - POOL_HIST:])
    kvw_s5 = kvw_s.reshape(nb, t, *kv_shape)
    u_s3 = u_s.reshape(nb, t, -1)
    st_s = (kvc_s.reshape(nb, t, *kv_shape), kvs_s.reshape(nb, t, *kv_shape),
            jnp.concatenate([state_win, kvw_s5], axis=1)[:, t:], jnp.concatenate([state_pool, u_s3], axis=1)[:, t:])
    return y_p.reshape(batch, seq, d_model), y_s.reshape(nb, t, d_model), st_p, st_s


def kernel(x_prompt, x_sample, cache_cmp_kv, cache_slc_kv, state_win_kv, state_pool, page_table, norm1_g, w_in, q_gain, k_cmp_gain, k_slc_gain, k_win_gain, w_cmp_k, b_cmp_k, w_cmp_v, b_cmp_v, w_pm, s_pm, w_out, norm2_g, w_router, b_router, w_gu, b_gu, w_down, b_down):
    layer_params = (norm1_g, w_in, q_gain, k_cmp_gain, k_slc_gain, k_win_gain, w_cmp_k, b_cmp_k, w_cmp_v, b_cmp_v,
                    w_pm, s_pm, w_out, norm2_g, w_router, b_router, w_gu, b_gu, w_down, b_down)
    y_p, y_s = x_prompt, x_sample
    p_states, s_states = [], []
    for layer in range(norm1_g.shape[0]):
        lw = [w[layer] for w in layer_params]
        y_p, y_s, st_p, st_s = _layer(y_p, y_s, cache_cmp_kv[layer], cache_slc_kv[layer], state_win_kv[layer],
                                      state_pool[layer], page_table, *lw)
        p_states.append(st_p)
        s_states.append(st_s)
    stack = lambda states, i: jnp.stack([s[i] for s in states], axis=0)
    return (y_p, y_s,
            stack(p_states, 0), stack(p_states, 1), stack(p_states, 2), stack(p_states, 3),
            stack(s_states, 0), stack(s_states, 1), stack(s_states, 2), stack(s_states, 3))
```

```python
import functools
import math

import jax
import jax.numpy as jnp
from jax import lax
from jax.experimental import pallas as pl
from jax.experimental.pallas import tpu as pltpu

F32 = jnp.float32
BF16 = jnp.bfloat16
I32 = jnp.int32

N_HEADS = 16
HEAD_DIM = 64
KV_HEADS = 4
Q_PER_KV = N_HEADS // KV_HEADS
ATTN_WIDTH = N_HEADS * HEAD_DIM
KV_WIDTH = KV_HEADS * HEAD_DIM
CMP_LEN = 32
CMP_STRIDE = 16
SLC_BLOCK = 64
N_SELECT = 16
WINDOW = 512
FORCE_SCORE = 1e4
NEG_INF = -1e30
POOL_WINDOWS = (2, 4, 8, 16)
MAX_POOL_W = max(POOL_WINDOWS)
POOL_HIST = MAX_POOL_W - 1
N_EXPERTS = 32
TOP_K = 4
SWIGLU_LIMIT = 7.0
SWIGLU_ALPHA = 1.702
EPS = 1e-6
PAGE_SIZE = 128

LANES = 128
VMEM_LIMIT = 56 * 1024 * 1024

GATE_PAD = LANES
SLOPES = [2.0 ** (-8.0 * (h + 1) / N_HEADS) for h in range(N_HEADS)]
LOG2E = math.log2(math.e)


def _cparams(sem):
    return pltpu.CompilerParams(dimension_semantics=sem, vmem_limit_bytes=VMEM_LIMIT)


def _iota(shape, dim):
    return lax.broadcasted_iota(I32, shape, dim)


def _split_bf16(x):
    hi = x.astype(BF16)
    lo = (x - hi.astype(F32)).astype(BF16)
    return hi, lo


def _dot(a, b):
    return jnp.dot(a, b, preferred_element_type=F32)


def _dot_t(a, b):
    return lax.dot_general(a, b, (((1,), (1,)), ((), ())), preferred_element_type=F32)


def _head_mean_sq(z):
    m, w = z.shape
    ones_bd = jnp.where(_iota((256, 256), 0) // HEAD_DIM == _iota((256, 256), 1) // HEAD_DIM, 1.0, 0.0).astype(BF16)
    zz = z * z
    hi, lo = _split_bf16(zz)
    parts = []
    for c in range(w // 256):
        sl = slice(c * 256, (c + 1) * 256)
        parts.append(_dot(hi[:, sl], ones_bd) + _dot(lo[:, sl], ones_bd))
    ss = parts[0] if len(parts) == 1 else jnp.concatenate(parts, axis=1)
    return ss * (1.0 / HEAD_DIM)


C_Q = 0
C_KVC = ATTN_WIDTH
C_KVS = C_KVC + 2 * KV_WIDTH
C_KVW = C_KVS + 2 * KV_WIDTH
C_U = C_KVW + 2 * KV_WIDTH


def _inproj_kernel(x_ref, g1_ref, w_ref, qg_ref, ksg_ref, kwg_ref, *refs, tm, pm_width, seq_tiles, hist_pos):
    if seq_tiles is None:
        sp_ref, refs = refs[0], refs[1:]
    q_out, kvc_out, kvs_out, kvw_out, kvsb_out, kvwb_out, gate_out, u_out, d_out = refs[:9]
    x = x_ref[...]
    ms = jnp.mean(x * x, axis=-1, keepdims=True)
    h = (x * lax.rsqrt(ms + EPS)) * g1_ref[...]
    hb = h.astype(BF16)
    c_gate = C_U + pm_width

    zq = _dot(hb, w_ref[:, C_Q:C_KVC])
    qn = (zq * lax.rsqrt(_head_mean_sq(zq) + EPS)) * qg_ref[...]
    q_out[...] = (qn * (HEAD_DIM ** -0.5 * LOG2E)).astype(BF16)

    kvc_out[...] = _dot(hb, w_ref[:, C_KVC:C_KVS])

    zs = _dot(hb, w_ref[:, C_KVS:C_KVW])
    ks = zs[:, :KV_WIDTH]
    ks = (ks * lax.rsqrt(_head_mean_sq(ks) + EPS)) * ksg_ref[...]
    kvs = jnp.concatenate([ks, zs[:, KV_WIDTH:]], axis=1)
    kvs_out[...] = kvs
    kvsb_out[...] = kvs.astype(BF16)

    zw = _dot(hb, w_ref[:, C_KVW:C_U])
    kw = zw[:, :KV_WIDTH]
    kw = (kw * lax.rsqrt(_head_mean_sq(kw) + EPS)) * kwg_ref[...]
    kvw = jnp.concatenate([kw, zw[:, KV_WIDTH:]], axis=1)
    kvw_out[...] = kvw
    kvwb_out[...] = kvw.astype(BF16)

    gate_out[...] = jax.nn.sigmoid(_dot(hb, w_ref[:, c_gate:c_gate + GATE_PAD]))

    u = _dot(hb, w_ref[:, C_U:c_gate])
    u_out[...] = u

    pm_group = pm_width // len(POOL_WINDOWS)
    if seq_tiles is None:
        tpos = float(hist_pos + 1)
        for gi, w in enumerate(POOL_WINDOWS):
            cs = slice(gi * pm_group, (gi + 1) * pm_group)
            s = u[:, cs]
            for k in range(1, w):
                s = s + sp_ref[:, POOL_HIST - k, cs]
            d_out[:, cs] = (s / min(float(w), tpos) - u[:, cs]).astype(BF16)
    else:
        ext_ref = refs[9]
        j = pl.program_id(0) % seq_tiles

        @pl.when(j == 0)
        def _():
            ext_ref[0:MAX_POOL_W, :] = jnp.zeros((MAX_POOL_W, pm_width), F32)

        ext_ref[MAX_POOL_W:MAX_POOL_W + tm, :] = u
        tpos = (j * tm + _iota((tm, 1), 0) + 1).astype(F32)
        for gi, w in enumerate(POOL_WINDOWS):
            cs = slice(gi * pm_group, (gi + 1) * pm_group)
            s = ext_ref[MAX_POOL_W:MAX_POOL_W + tm, cs]
            for k in range(1, w):
                s = s + ext_ref[MAX_POOL_W - k:MAX_POOL_W - k + tm, cs]
            cnt = jnp.minimum(float(w), tpos)
            d_out[:, cs] = (s / cnt - u[:, cs]).astype(BF16)
        ext_ref[0:MAX_POOL_W, :] = ext_ref[tm:tm + MAX_POOL_W, :]


def _inproj(x2d, g1, w_packed, qg, ksg, kwg, *, tm, seq_len=None, pool_state=None, hist_pos=None):
    n, d_model = x2d.shape
    pm_width = d_model - ATTN_WIDTH
    seq_tiles = None if seq_len is None else seq_len // tm
    row = lambda w: pl.BlockSpec((tm, w), lambda i: (i, 0))
    full = lambda a: pl.BlockSpec(a.shape, lambda i: (0,) * a.ndim)
    out_shape = [
        jax.ShapeDtypeStruct((n, ATTN_WIDTH), BF16),
        jax.ShapeDtypeStruct((n, 2 * KV_WIDTH), F32),
        jax.ShapeDtypeStruct((n, 2 * KV_WIDTH), F32),
        jax.ShapeDtypeStruct((n, 2 * KV_WIDTH), F32),
        jax.ShapeDtypeStruct((n, 2 * KV_WIDTH), BF16),
        jax.ShapeDtypeStruct((n, 2 * KV_WIDTH), BF16),
        jax.ShapeDtypeStruct((n, GATE_PAD), F32),
        jax.ShapeDtypeStruct((n, pm_width), F32),
        jax.ShapeDtypeStruct((n, pm_width), BF16),
    ]
    out_specs = [row(ATTN_WIDTH), row(2 * KV_WIDTH), row(2 * KV_WIDTH), row(2 * KV_WIDTH),
                 row(2 * KV_WIDTH), row(2 * KV_WIDTH), row(GATE_PAD), row(pm_width), row(pm_width)]
    in_specs = [row(d_model), full(g1), full(w_packed), full(qg), full(ksg), full(kwg)]
    args = [x2d, g1, w_packed, qg, ksg, kwg]
    scratch = []
    if seq_tiles is None:
        in_specs.append(pl.BlockSpec((tm, POOL_HIST, pm_width), lambda i: (i, 0, 0)))
        args.append(pool_state)
    else:
        scratch.append(pltpu.VMEM((tm + MAX_POOL_W, pm_width), F32))
    return pl.pallas_call(
        functools.partial(_inproj_kernel, tm=tm, pm_width=pm_width, seq_tiles=seq_tiles, hist_pos=hist_pos),
        grid=(n // tm,),
        in_specs=in_specs,
        out_specs=out_specs,
        out_shape=out_shape,
        scratch_shapes=scratch,
        compiler_params=_cparams(("arbitrary",)),
        name="inproj",
    )(*args)


CMP_SUB_PER_PAGE = PAGE_SIZE // CMP_STRIDE


def _compress_kernel(pt_ref, *refs, pages_per_step):
    del pt_ref
    p = pages_per_step
    page_refs = refs[:p]
    halo_ref, w0_ref, w1_ref, b_ref, kg_ref, kc_out, vc_out, xs_ref = refs[p:]
    n_lt = 2 * KV_WIDTH // LANES
    nb = p * CMP_SUB_PER_PAGE
    parts = []
    for c in range(n_lt):
        cs = slice(c * LANES, (c + 1) * LANES)
        for k in range(p):
            xs_ref[c, k * PAGE_SIZE:(k + 1) * PAGE_SIZE, :] = page_refs[k][:, cs]
        xs_ref[c, p * PAGE_SIZE:p * PAGE_SIZE + CMP_STRIDE, :] = halo_ref[:, cs]
        acc = jnp.zeros((nb, LANES), F32) + b_ref[:, cs]
        for j in range(CMP_STRIDE):
            acc = acc + xs_ref[c, pl.ds(j, nb, stride=CMP_STRIDE), :] * w0_ref[j:j + 1, cs]
            acc = acc + xs_ref[c, pl.ds(CMP_STRIDE + j, nb, stride=CMP_STRIDE), :] * w1_ref[j:j + 1, cs]
        parts.append(acc)
    acc = jnp.concatenate(parts, axis=1)
    kc = acc[:, :KV_WIDTH]
    kc = (kc * lax.rsqrt(_head_mean_sq(kc) + EPS)) * kg_ref[...]
    kc_out[...] = kc.astype(BF16)
    vc_out[...] = acc[:, KV_WIDTH:].astype(BF16)


def _compress(page_table, pages, w0t, w1t, bias, kgain, *, pages_per_step):
    nb, npg = page_table.shape
    p = pages_per_step
    steps = npg // p

    def page_spec(k):
        return pl.BlockSpec((None, PAGE_SIZE, 2 * KV_WIDTH), lambda b, i, pt: (pt[b, i * p + k], 0, 0))

    halo_spec = pl.BlockSpec((None, CMP_STRIDE, 2 * KV_WIDTH),
                             lambda b, i, pt: (pt[b, jnp.minimum(i * p + p, npg - 1)], 0, 0))
    full = lambda a: pl.BlockSpec(a.shape, lambda b, i, pt: (0,) * a.ndim)
    out_spec = pl.BlockSpec((None, p * CMP_SUB_PER_PAGE, KV_WIDTH), lambda b, i, pt: (b, i, 0))
    grid_spec = pltpu.PrefetchScalarGridSpec(
        num_scalar_prefetch=1,
        grid=(nb, steps),
        in_specs=[page_spec(k) for k in range(p)] + [halo_spec, full(w0t), full(w1t), full(bias), full(kgain)],
        out_specs=[out_spec, out_spec],
        scratch_shapes=[pltpu.VMEM((2 * KV_WIDTH // LANES, p * PAGE_SIZE + CMP_STRIDE, LANES), F32)],
    )
    return pl.pallas_call(
        functools.partial(_compress_kernel, pages_per_step=p),
        grid_spec=grid_spec,
        out_shape=[jax.ShapeDtypeStruct((nb, npg * CMP_SUB_PER_PAGE, KV_WIDTH), BF16)] * 2,
        compiler_params=_cparams(("arbitrary", "arbitrary")),
        name="compress",
    )(page_table, *([pages] * p), pages, w0t, w1t, bias, kgain)


TQ = 128
TK = 256
QROWS = Q_PER_KV * TQ


def _select_members_t(score_t, n_cand):
    rows = _iota((n_cand, 1), 0)
    rank = jnp.zeros(score_t.shape, F32)
    for i in range(n_cand):
        si = score_t[i:i + 1, :]
        ahead = jnp.where(si > score_t, 1.0, jnp.where(si == score_t, jnp.where(rows > i, 1.0, 0.0), 0.0))
        rank = rank + ahead
    return jnp.where(rank < float(N_SELECT), 1.0, 0.0)


def _prompt_attn_kernel(q_ref, gate_ref, kc_ref, vct_ref, ks_ref, vst_ref, kw_ref, vwt_ref, ct_ref, ext_ref,
                        o_ref, qpad_ref, bias_ref, memb_ref, m_ref, l_ref, acc_ref, tot_ref, *, n_slc):
    qi = pl.program_id(1)
    q0 = qi * TQ
    lane = _iota((1, QROWS), 1)
    r_lane = lane // TQ
    qidx = q0 + lane % TQ
    qpos_l = q0 + _iota((1, TQ), 1)
    lane_g = _iota((1, KV_WIDTH), 1) // HEAD_DIM
    n_cmp_pad = kc_ref.shape[0]
    n_kt = memb_ref.shape[1]
    gates_t = gate_ref[...].T

    def slope_row(g):
        row = jnp.zeros((1, QROWS), F32)
        for r in range(Q_PER_KV):
            row = jnp.where(r_lane == r, SLOPES[g * Q_PER_KV + r] * LOG2E, row)
        return row

    def gate_row(g, k):
        c = k * N_HEADS + g * Q_PER_KV
        return jnp.concatenate([gates_t[c + r:c + r + 1, :] for r in range(Q_PER_KV)], axis=1)

    @pl.when((pl.program_id(0) == 0) & (qi == 0))
    def _():
        off = _iota((TK, QROWS), 0).astype(F32)
        for g in range(KV_HEADS):
            bias_ref[g] = slope_row(g) * off

    c_start = _iota((n_cmp_pad, QROWS), 0) * CMP_STRIDE
    c_mid = c_start.astype(F32) + 0.5 * (CMP_LEN - 1)
    cmask = (c_start + (CMP_LEN - 1)) <= qidx
    kidx = _iota((TK, TQ), 0)

    for g in range(KV_HEADS):
        qpad = jnp.concatenate(
            [jnp.where(lane_g == g, q_ref[:, r * KV_WIDTH:(r + 1) * KV_WIDTH], jnp.zeros((), BF16))
             for r in range(Q_PER_KV)], axis=0)
        qpad_ref[g] = qpad
        s = _dot_t(kc_ref[...], qpad) + slope_row(g) * c_mid
        s = jnp.where(cmask, s, NEG_INF)
        e = jnp.where(cmask, jnp.exp2(s - jnp.max(s, axis=0, keepdims=True)), 0.0)
        l = jnp.sum(e, axis=0, keepdims=True)
        pc = e * jnp.where(l > 0.0, 1.0 / l, 0.0)
        oc = _dot(vct_ref[g * HEAD_DIM:(g + 1) * HEAD_DIM, :], pc.astype(BF16))
        tot_ref[g] = oc * gate_row(g, 0)
        p_sum = pc[:, 0:TQ]
        for r in range(1, Q_PER_KV):
            p_sum = p_sum + pc[:, r * TQ:(r + 1) * TQ]
        hi, lo = _split_bf16(p_sum)
        imp_t = (_dot(ct_ref[...], hi) + _dot(ct_ref[...], lo))[0:n_slc]
        blk = _iota((n_slc, 1), 0)
        qblk = qpos_l // SLC_BLOCK
        forced = (blk == 0) | (blk == qblk) | (blk == qblk - 1)
        score_t = jnp.where(forced, FORCE_SCORE, jnp.where(blk * SLC_BLOCK <= qpos_l, imp_t, NEG_INF))
        member_t = _select_members_t(score_t, n_slc)
        member_t = jnp.concatenate([member_t, jnp.zeros((LANES - n_slc, TQ), F32)], axis=0).astype(BF16)
        memb_keys = _dot(ext_ref[...], member_t)
        for j in range(n_kt):
            ok = jnp.where(kidx + j * TK <= qpos_l, memb_keys[j * TK:(j + 1) * TK, :], 0.0)
            memb_ref[g, j] = jnp.where(ok > 0.5, 0.0, NEG_INF)

    def sweep(k_ref, vt_ref, lo_t, hi_t, mask_fn, gate_k):
        m_ref[...] = jnp.full(m_ref.shape, NEG_INF, F32)
        l_ref[...] = jnp.zeros(l_ref.shape, F32)
        acc_ref[...] = jnp.zeros(acc_ref.shape, F32)

        def body(kj, carry):
            k0 = pl.multiple_of(kj * TK, TK)
            kt = k_ref[pl.ds(k0, TK), :]
            vt = vt_ref[kj]
            k0f = k0.astype(F32)
            for g in range(KV_HEADS):
                s = _dot_t(kt, qpad_ref[g]) + bias_ref[g] + slope_row(g) * k0f + mask_fn(g, kj, k0)
                m_old = m_ref[g]
                m_new = jnp.maximum(m_old, jnp.max(s, axis=0, keepdims=True))
                alpha = jnp.exp2(m_old - m_new)
                p = jnp.exp2(s - m_new)
                l_ref[g] = alpha * l_ref[g] + jnp.sum(p, axis=0, keepdims=True)
                acc_ref[g] = alpha * acc_ref[g] + _dot(vt[g * HEAD_DIM:(g + 1) * HEAD_DIM, :], p.astype(BF16))
                m_ref[g] = m_new
            return carry

        lax.fori_loop(lo_t, hi_t, body, 0)
        for g in range(KV_HEADS):
            l = l_ref[g]
            tot_ref[g] = tot_ref[g] + acc_ref[g] * (jnp.where(l > 0.0, 1.0 / l, 0.0) * gate_row(g, gate_k))

    def slc_mask(g, kj, k0):
        return jnp.concatenate([memb_ref[g, kj]] * Q_PER_KV, axis=1)

    n_hi = (q0 + TQ + TK - 1) // TK
    sweep(ks_ref, vst_ref, 0, n_hi, slc_mask, 1)

    def win_mask(g, kj, k0):
        d = kidx - _iota((TK, TQ), 1) + (k0 - q0)
        wm = jnp.where(d <= 0, jnp.where(d > -WINDOW, 0.0, NEG_INF), NEG_INF)
        return jnp.concatenate([wm] * Q_PER_KV, axis=1)

    w_lo = jnp.maximum(q0 - WINDOW, 0) // TK
    sweep(kw_ref, vwt_ref, w_lo, n_hi, win_mask, 2)

    total = jnp.concatenate([tot_ref[g] for g in range(KV_HEADS)], axis=0)
    for r in range(Q_PER_KV):
        o_ref[:, r * KV_WIDTH:(r + 1) * KV_WIDTH] = total[:, r * TQ:(r + 1) * TQ].T.astype(BF16)


def _prompt_attn(q, gate, kc, vct, kvs_b, vst, kvw_b, vwt, ct, ext, *, batch, seq):
    n_slc = -(-seq // SLC_BLOCK)
    n_cmp_pad = kc.shape[1]
    n_qt = seq // TQ
    n_kt = seq // TK
    rowblk = lambda w: pl.BlockSpec((TQ, w), lambda b, i: (b * n_qt + i, 0))
    kblk = pl.BlockSpec((seq, KV_WIDTH), lambda b, i: (b, 0))
    vtblk = pl.BlockSpec((n_kt, KV_WIDTH, TK), lambda b, i: (b, 0, 0))
    full = lambda a: pl.BlockSpec(a.shape, lambda b, i: (0,) * a.ndim)
    return pl.pallas_call(
        functools.partial(_prompt_attn_kernel, n_slc=n_slc),
        grid=(batch, n_qt),
        in_specs=[rowblk(ATTN_WIDTH), rowblk(GATE_PAD),
                  pl.BlockSpec((None, n_cmp_pad, KV_WIDTH), lambda b, i: (b, 0, 0)),
                  pl.BlockSpec((None, KV_WIDTH, n_cmp_pad), lambda b, i: (b, 0, 0)),
                  kblk, vtblk, kblk, vtblk, full(ct), full(ext)],
        out_specs=rowblk(ATTN_WIDTH),
        out_shape=jax.ShapeDtypeStruct((batch * seq, ATTN_WIDTH), BF16),
        scratch_shapes=[pltpu.VMEM((KV_HEADS, QROWS, KV_WIDTH), BF16),
                        pltpu.VMEM((KV_HEADS, TK, QROWS), F32),
                        pltpu.VMEM((KV_HEADS, n_kt, TK, TQ), F32),
                        pltpu.VMEM((KV_HEADS, 1, QROWS), F32), pltpu.VMEM((KV_HEADS, 1, QROWS), F32),
                        pltpu.VMEM((KV_HEADS, HEAD_DIM, QROWS), F32), pltpu.VMEM((KV_HEADS, HEAD_DIM, QROWS), F32)],
        compiler_params=_cparams(("arbitrary", "arbitrary")),
        name="prompt_attn",
    )(q, gate, kc, vct, kvs_b, vst, kvw_b, vwt, ct, ext)


DROWS = N_HEADS
NEVER = -3e38


def _decode_qpad(q_ref):
    lane_g = _iota((KV_HEADS, KV_WIDTH), 1) // HEAD_DIM
    row_g = _iota((KV_HEADS, KV_WIDTH), 0)
    parts = []
    for r in range(Q_PER_KV):
        qr = jnp.broadcast_to(q_ref[:, r * KV_WIDTH:(r + 1) * KV_WIDTH].astype(F32), (KV_HEADS, KV_WIDTH))
        parts.append(jnp.where(lane_g == row_g, qr, 0.0))
    return jnp.concatenate(parts, axis=0).astype(BF16)


def _compress_t_kernel(pt_ref, *refs, pages_per_step):
    del pt_ref
    p = pages_per_step
    page_refs = refs[:p]
    w0_ref, w1_ref, a0_out, a1_out = refs[p:]
    prow = _iota((PAGE_SIZE, LANES), 0)
    ocol = _iota((PAGE_SIZE, LANES), 1)
    a0 = jnp.zeros(a0_out.shape, F32)
    a1 = jnp.zeros(a1_out.shape, F32)
    for k in range(p):
        sel = jnp.where(ocol == k * CMP_SUB_PER_PAGE + prow // CMP_STRIDE, 1.0, 0.0).astype(BF16)
        x = page_refs[k][...]
        hi, lo = _split_bf16(x * w0_ref[...])
        a0 = a0 + _dot(hi, sel) + _dot(lo, sel)
        hi, lo = _split_bf16(x * w1_ref[...])
        a1 = a1 + _dot(hi, sel) + _dot(lo, sel)
    a0_out[...] = a0
    a1_out[...] = a1


def _compress_t(page_table, pages_t, w0t, w1t, *, pages_per_step):
    nb, npg = page_table.shape
    p = pages_per_step
    assert p * CMP_SUB_PER_PAGE == LANES

    def page_spec(k):
        return pl.BlockSpec((None, 2 * KV_WIDTH, PAGE_SIZE), lambda b, i, pt: (pt[b, i * p + k], 0, 0))

    full = lambda a: pl.BlockSpec(a.shape, lambda b, i, pt: (0,) * a.ndim)
    out_spec = pl.BlockSpec((None, 2 * KV_WIDTH, LANES), lambda b, i, pt: (b, 0, i))
    grid_spec = pltpu.PrefetchScalarGridSpec(
        num_scalar_prefetch=1,
        grid=(nb, npg // p),
        in_specs=[page_spec(k) for k in range(p)] + [full(w0t), full(w1t)],
        out_specs=[out_spec, out_spec],
    )
    return pl.pallas_call(
        functools.partial(_compress_t_kernel, pages_per_step=p),
        grid_spec=grid_spec,
        out_shape=[jax.ShapeDtypeStruct((nb, 2 * KV_WIDTH, npg * CMP_SUB_PER_PAGE), F32)] * 2,
        compiler_params=_cparams(("arbitrary", "arbitrary")),
        name="compress_t",
    )(page_table, *([pages_t] * p), w0t, w1t)


def _decode_cmp_kernel(q_ref, slope_ref, a0_ref, a1_ref, cb_ref, kg_ref, c_ref, oc_out, member_out, *, qpos, n_slc):
    qpad = _decode_qpad(q_ref)
    slope = slope_ref[...]
    n_cmp_pad = a0_ref.shape[1]
    acc = a0_ref[...] + pltpu.roll(a1_ref[...], n_cmp_pad - 1, 1) + cb_ref[...]
    kparts = []
    for g in range(KV_HEADS):
        kg = acc[g * HEAD_DIM:(g + 1) * HEAD_DIM, :]
        ms = jnp.mean(kg * kg, axis=0, keepdims=True)
        kparts.append((kg * lax.rsqrt(ms + EPS)) * kg_ref[...])
    kc_t = jnp.concatenate(kparts, axis=0).astype(BF16)
    vc_t = acc[KV_WIDTH:, :].astype(BF16)
    c_start = _iota((1, n_cmp_pad), 1) * CMP_STRIDE
    c_mid = c_start.astype(F32) + 0.5 * (CMP_LEN - 1)
    cmask = (c_start + (CMP_LEN - 1)) <= qpos
    s = _dot(qpad, kc_t) + slope * c_mid
    s = jnp.where(cmask, s, NEG_INF)
    e = jnp.where(cmask, jnp.exp2(s - jnp.max(s, axis=-1, keepdims=True)), 0.0)
    l = jnp.sum(e, axis=-1, keepdims=True)
    pc = e * jnp.where(l > 0.0, 1.0 / l, 0.0)
    oc_out[...] = _dot_t(pc.astype(BF16), vc_t)
    p_sum = pc[0:KV_HEADS]
    for r in range(1, Q_PER_KV):
        p_sum = p_sum + pc[r * KV_HEADS:(r + 1) * KV_HEADS]
    p_sum = jnp.concatenate([p_sum, jnp.zeros((DROWS - KV_HEADS, n_cmp_pad), F32)], axis=0)
    hi, lo = _split_bf16(p_sum)
    imp = _dot(hi, c_ref[...]) + _dot(lo, c_ref[...])
    n_pad = imp.shape[1]
    blk = _iota((1, n_pad), 1)
    qblk = qpos // SLC_BLOCK
    forced = (blk == 0) | (blk == qblk) | (blk == qblk - 1)
    score = jnp.where(forced, FORCE_SCORE, jnp.where(blk * SLC_BLOCK <= qpos, imp, NEG_INF))
    score = jnp.where(blk < n_slc, score, NEVER)
    rank = jnp.zeros(score.shape, F32)
    for i in range(n_slc):
        si = score[:, i:i + 1]
        rank = rank + jnp.where(si > score, 1.0, jnp.where(si == score, jnp.where(blk > i, 1.0, 0.0), 0.0))
    member_out[...] = jnp.where(rank < float(min(N_SELECT, n_slc)), 1.0, 0.0)


def _decode_cmp(q3, slopes, a0, a1, cb_col, kg_col, cmat, *, qpos, n_slc):
    nb = q3.shape[0]
    n_cmp_pad = a0.shape[2]
    n_pad = cmat.shape[1]
    per_b = lambda r, c: pl.BlockSpec((None, r, c), lambda b: (b, 0, 0))
    full = lambda a: pl.BlockSpec(a.shape, lambda b: (0,) * a.ndim)
    return pl.pallas_call(
        functools.partial(_decode_cmp_kernel, qpos=qpos, n_slc=n_slc),
        grid=(nb,),
        in_specs=[per_b(1, ATTN_WIDTH), full(slopes), per_b(2 * KV_WIDTH, n_cmp_pad), per_b(2 * KV_WIDTH, n_cmp_pad),
                  full(cb_col), full(kg_col), full(cmat)],
        out_specs=[per_b(DROWS, KV_WIDTH), per_b(DROWS, n_pad)],
        out_shape=[jax.ShapeDtypeStruct((nb, DROWS, KV_WIDTH), F32), jax.ShapeDtypeStruct((nb, DROWS, n_pad), F32)],
        compiler_params=_cparams(("arbitrary",)),
        name="decode_cmp",
    )(q3, slopes, a0, a1, cb_col, kg_col, cmat)


def _decode_sw_kernel(pt_ref, q_ref, slope_ref, oc_ref, member_ref, gate_ref, gexp_ref, win_ref, news_ref, neww_ref,
                      *refs, pages_per_step, qpos, past, n_buf):
    del pt_ref
    p = pages_per_step
    page_refs = refs[:p]
    o_ref, m_ref, l_ref, acc_ref, ow_ref = refs[p:]
    c = pl.program_id(1)
    qpad = _decode_qpad(q_ref)
    qf = qpad.astype(F32)
    slope = slope_ref[...]

    def new_row(row_ref):
        kn = row_ref[:, :KV_WIDTH].astype(BF16).astype(F32)
        vn = row_ref[:, KV_WIDTH:].astype(BF16).astype(F32)
        return jnp.sum(qf * kn, axis=-1, keepdims=True) + slope * float(qpos), vn

    @pl.when(c == 0)
    def _():
        m_ref[...] = jnp.full((DROWS, 1), NEG_INF, F32)
        l_ref[...] = jnp.zeros((DROWS, 1), F32)
        acc_ref[...] = jnp.zeros((DROWS, KV_WIDTH), F32)
        kw_t = win_ref[:KV_WIDTH, :].astype(BF16)
        vw_t = win_ref[KV_WIDTH:, :].astype(BF16)
        wpos = past - n_buf + _iota((1, n_buf), 1)
        ok = (wpos <= qpos) & (wpos > qpos - WINDOW) & (wpos >= 0)
        s = jnp.where(ok, _dot(qpad, kw_t) + slope * wpos.astype(F32), NEG_INF)
        s_new, v_new = new_row(neww_ref)
        m = jnp.maximum(jnp.max(s, axis=-1, keepdims=True), s_new)
        e = jnp.where(ok, jnp.exp2(s - m), 0.0)
        e_new = jnp.exp2(s_new - m)
        l = jnp.sum(e, axis=-1, keepdims=True) + e_new
        ow_ref[...] = (_dot_t(e.astype(BF16), vw_t) + e_new * v_new) / l

    nk = p * PAGE_SIZE
    kt_t = jnp.concatenate([page_refs[k][:KV_WIDTH, :].astype(BF16) for k in range(p)], axis=1)
    vt_t = jnp.concatenate([page_refs[k][KV_WIDTH:, :].astype(BF16) for k in range(p)], axis=1)
    member = member_ref[0:KV_HEADS, :].astype(BF16)
    member = jnp.concatenate([member] * Q_PER_KV, axis=0)
    n_pad = member.shape[1]
    kidx = c * nk + _iota((n_pad, nk), 1)
    expand = jnp.where(kidx // SLC_BLOCK == _iota((n_pad, nk), 0), 1.0, 0.0).astype(BF16)
    mk = _dot(member, expand)
    kpos = (c * nk + _iota((1, nk), 1)).astype(F32)
    s = jnp.where(mk > 0.5, _dot(qpad, kt_t) + slope * kpos, NEG_INF)
    m_old = m_ref[...]
    m_new = jnp.maximum(m_old, jnp.max(s, axis=-1, keepdims=True))
    alpha = jnp.exp2(m_old - m_new)
    pr = jnp.exp2(s - m_new)
    l_ref[...] = alpha * l_ref[...] + jnp.sum(pr, axis=-1, keepdims=True)
    acc_ref[...] = alpha * acc_ref[...] + _dot_t(pr.astype(BF16), vt_t)
    m_ref[...] = m_new

    @pl.when(c == pl.num_programs(1) - 1)
    def _():
        new_blk = qpos // SLC_BLOCK
        is_member = jnp.concatenate([member_ref[0:KV_HEADS, new_blk:new_blk + 1]] * Q_PER_KV, axis=0) > 0.5
        s_new, v_new = new_row(news_ref)
        s_new = jnp.where(is_member, s_new, NEG_INF)
        m_old = m_ref[...]
        m_new = jnp.maximum(m_old, s_new)
        alpha = jnp.exp2(m_old - m_new)
        e_new = jnp.where(is_member, jnp.exp2(s_new - m_new), 0.0)
        l = alpha * l_ref[...] + e_new
        os = (alpha * acc_ref[...] + e_new * v_new) / l

        lane_g = _iota((1, KV_WIDTH), 1) // HEAD_DIM

        def flat(o):
            segs = []
            for r in range(Q_PER_KV):
                seg = jnp.zeros((1, KV_WIDTH), F32)
                for g in range(KV_HEADS):
                    i = r * KV_HEADS + g
                    seg = seg + jnp.where(lane_g == g, o[i:i + 1, :], 0.0)
                segs.append(seg)
            return jnp.concatenate(segs, axis=1)

        ghi, glo = _split_bf16(jnp.broadcast_to(gate_ref[...], (DROWS, GATE_PAD)))
        gx = (_dot(ghi, gexp_ref[...]) + _dot(glo, gexp_ref[...]))[0:1]
        o = (gx[:, 0:ATTN_WIDTH] * flat(oc_ref[...])
             + gx[:, ATTN_WIDTH:2 * ATTN_WIDTH] * flat(os)
             + gx[:, 2 * ATTN_WIDTH:] * flat(ow_ref[...]))
        o_ref[...] = o.astype(BF16)


def _decode_sw(page_table, q3, slopes, oc, member, gate3, gexp, win_state_t, new_s, new_w, pages_t,
               *, pages_per_step, qpos, past):
    nb, npg = page_table.shape
    p = pages_per_step
    n_buf = win_state_t.shape[2]
    n_pad = member.shape[2]
    per_b = lambda r, c: pl.BlockSpec((None, r, c), lambda b, i, pt: (b, 0, 0))
    full = lambda a: pl.BlockSpec(a.shape, lambda b, i, pt: (0,) * a.ndim)

    def page_spec(k):
        return pl.BlockSpec((None, 2 * KV_WIDTH, PAGE_SIZE), lambda b, i, pt: (pt[b, i * p + k], 0, 0))

    grid_spec = pltpu.PrefetchScalarGridSpec(
        num_scalar_prefetch=1,
        grid=(nb, npg // p),
        in_specs=[per_b(1, ATTN_WIDTH), full(slopes), per_b(DROWS, KV_WIDTH), per_b(DROWS, n_pad), per_b(1, GATE_PAD),
                  full(gexp), per_b(2 * KV_WIDTH, n_buf), per_b(1, 2 * KV_WIDTH), per_b(1, 2 * KV_WIDTH)]
                 + [page_spec(k) for k in range(p)],
        out_specs=per_b(1, ATTN_WIDTH),
        scratch_shapes=[pltpu.VMEM((DROWS, 1), F32), pltpu.VMEM((DROWS, 1), F32),
                        pltpu.VMEM((DROWS, KV_WIDTH), F32), pltpu.VMEM((DROWS, KV_WIDTH), F32)],
    )
    return pl.pallas_call(
        functools.partial(_decode_sw_kernel, pages_per_step=p, qpos=qpos, past=past, n_buf=n_buf),
        grid_spec=grid_spec,
        out_shape=jax.ShapeDtypeStruct((nb, 1, ATTN_WIDTH), BF16),
        compiler_params=_cparams(("arbitrary", "arbitrary")),
        name="decode_slc_win",
    )(page_table, q3, slopes, oc, member, gate3, gexp, win_state_t, new_s, new_w, *([pages_t] * p))


def _outproj_kernel(x_ref, oa_ref, d_ref, wpm_ref, spm_ref, woa_ref, wop_ref, g2_ref, wrh_ref, wrl_ref, br_ref,
                    x1_out, h2_out, tope_out, gate_out):
    n_grp, pm_group = wpm_ref.shape[0], wpm_ref.shape[1]
    pm = jnp.concatenate([_dot(d_ref[:, gi * pm_group:(gi + 1) * pm_group], wpm_ref[gi]) for gi in range(n_grp)], axis=1)
    pm = pm * spm_ref[...]
    x1 = x_ref[...] + _dot(oa_ref[...], woa_ref[...]) + _dot(pm.astype(BF16), wop_ref[...])
    x1_out[...] = x1
    ms = jnp.mean(x1 * x1, axis=-1, keepdims=True)
    h2 = (x1 * lax.rsqrt(ms + EPS)) * g2_ref[...]
    h2_out[...] = h2
    hi, lo = _split_bf16(h2)
    logits = _dot(hi, wrh_ref[...]) + _dot(lo, wrh_ref[...]) + _dot(hi, wrl_ref[...]) + br_ref[...]
    tm = logits.shape[0]
    lane = _iota((tm, LANES), 1)
    logits = jnp.where(lane < N_EXPERTS, logits, NEVER)
    vals, idxs = [], []
    for _ in range(TOP_K):
        m = jnp.max(logits, axis=-1, keepdims=True)
        idx = jnp.min(jnp.where(logits == m, lane, LANES), axis=-1, keepdims=True)
        vals.append(m)
        idxs.append(idx)
        logits = jnp.where(lane == idx, NEVER, logits)
    es = [jnp.exp(v - vals[0]) for v in vals]
    den = es[0]
    for e in es[1:]:
        den = den + e
    tope = jnp.full((tm, LANES), -1, I32)
    gts = jnp.zeros((tm, LANES), F32)
    for k in range(TOP_K):
        tope = jnp.where(lane == k, idxs[k], tope)
        gts = jnp.where(lane == k, es[k] / den, gts)
    tope_out[...] = tope
    gate_out[...] = gts


def _outproj(x2d, o_attn, dpool, wpm, spm, woa, wop, g2, wrh, wrl, br, *, tm):
    n, d_model = x2d.shape
    row = lambda w: pl.BlockSpec((tm, w), lambda i: (i, 0))
    full = lambda a: pl.BlockSpec(a.shape, lambda i: (0,) * a.ndim)
    consts = [wpm, spm, woa, wop, g2, wrh, wrl, br]
    return pl.pallas_call(
        _outproj_kernel,
        grid=(n // tm,),
        in_specs=[row(d_model), row(o_attn.shape[1]), row(dpool.shape[1])] + [full(a) for a in consts],
        out_specs=[row(d_model), row(d_model), row(LANES), row(LANES)],
        out_shape=[jax.ShapeDtypeStruct((n, d_model), F32), jax.ShapeDtypeStruct((n, d_model), F32),
                   jax.ShapeDtypeStruct((n, LANES), I32), jax.ShapeDtypeStruct((n, LANES), F32)],
        compiler_params=_cparams(("arbitrary",)),
        name="outproj",
    )(x2d, o_attn, dpool, *consts)


MOE_ROWS = 1024
MOE_SUB = 256
MOE_FC = 256
ROUTE_TILE = 512


def _route_kernel(e_ref, rank_out, cnt_out, carry_ref):
    @pl.when(pl.program_id(0) == 0)
    def _():
        carry_ref[...] = jnp.zeros(carry_ref.shape, F32)

    tr = e_ref.shape[0]
    lane = _iota((tr, LANES), 1)
    e = e_ref[...]
    ohs = [jnp.where(e[:, k:k + 1] == lane, 1.0, 0.0) for k in range(TOP_K)]
    tot = ohs[0]
    for oh in ohs[1:]:
        tot = tot + oh
    lower = jnp.where(_iota((tr, tr), 1) < _iota((tr, tr), 0), 1.0, 0.0).astype(BF16)
    before = _dot(lower, tot.astype(BF16)) + carry_ref[...]
    rank = jnp.zeros((tr, LANES), I32)
    for k in range(TOP_K):
        rk = jnp.sum(ohs[k] * before, axis=-1, keepdims=True).astype(I32)
        rank = jnp.where(lane == k, rk, rank)
    rank_out[...] = rank
    carry_ref[...] = carry_ref[...] + jnp.sum(tot, axis=0, keepdims=True)
    cnt_out[...] = carry_ref[...]


def _route(tope):
    n = tope.shape[0]
    return pl.pallas_call(
        _route_kernel,
        grid=(n // ROUTE_TILE,),
        in_specs=[pl.BlockSpec((ROUTE_TILE, LANES), lambda i: (i, 0))],
        out_specs=[pl.BlockSpec((ROUTE_TILE, LANES), lambda i: (i, 0)), pl.BlockSpec((1, LANES), lambda i: (0, 0))],
        out_shape=[jax.ShapeDtypeStruct((n, LANES), I32), jax.ShapeDtypeStruct((1, LANES), F32)],
        scratch_shapes=[pltpu.VMEM((1, LANES), F32)],
        compiler_params=_cparams(("arbitrary",)),
        name="moe_route",
    )(tope)


def _dispatch_kernel(slot_ref, h_ref, *refs):
    xbuf, sem = refs[-2], refs[-1]
    tr = h_ref.shape[0]

    def row_copy(i, s):
        return pltpu.make_async_copy(h_ref.at[pl.ds(i, 1)], xbuf.at[pl.ds(s, 1)], sem)

    def issue(i, carry):
        for k in range(TOP_K):
            row_copy(i, slot_ref[i * TOP_K + k]).start()
        return carry

    lax.fori_loop(0, tr, issue, 0)

    def drain(i, carry):
        for k in range(TOP_K):
            row_copy(i, slot_ref[i * TOP_K + k]).wait()
        return carry

    lax.fori_loop(0, tr, drain, 0)


def _dispatch(slots_flat, h2, xbuf, n_rows, *, tr):
    n, d = h2.shape
    in_specs = [pl.BlockSpec((tr * TOP_K,), lambda i: (i,), memory_space=pltpu.SMEM),
                pl.BlockSpec((tr, d), lambda i: (i, 0))]
    args = [slots_flat, h2]
    aliases = {}
    if xbuf is not None:
        in_specs.append(pl.BlockSpec(memory_space=pl.ANY))
        args.append(xbuf)
        aliases = {2: 0}
    return pl.pallas_call(
        _dispatch_kernel,
        grid=(n // tr,),
        in_specs=in_specs,
        out_specs=pl.BlockSpec(memory_space=pl.ANY),
        out_shape=jax.ShapeDtypeStruct((n_rows, d), F32),
        scratch_shapes=[pltpu.SemaphoreType.DMA(())],
        input_output_aliases=aliases,
        compiler_params=pltpu.CompilerParams(dimension_semantics=("arbitrary",), vmem_limit_bytes=VMEM_LIMIT,
                                             has_side_effects=True),
        name="moe_dispatch",
    )(*args)


def _experts_kernel(we_ref, wr_ref, wb_ref, x_ref, wg_ref, wu_ref, bg_ref, bu_ref, wd_ref, bd_ref, o_ref, xb_ref):
    del we_ref, wb_ref
    w = pl.program_id(0)
    c = pl.program_id(1)
    rows = wr_ref[w]

    @pl.when(rows > 0)
    def _():
        @pl.when(c == 0)
        def _():
            ridx = _iota((MOE_ROWS, 1), 0)
            xb_ref[...] = jnp.where(ridx < rows, x_ref[...], 0.0).astype(BF16)
            o_ref[...] = jnp.broadcast_to(bd_ref[...], o_ref.shape)

        def sub_tile(i, wg, wu, wd):
            rs = slice(i * MOE_SUB, (i + 1) * MOE_SUB)
            xs = xb_ref[rs, :]
            g = _dot(xs, wg) + bg_ref[...]
            u = _dot(xs, wu) + bu_ref[...]
            gh = jnp.minimum(g, SWIGLU_LIMIT)
            up = jnp.clip(u, -SWIGLU_LIMIT, SWIGLU_LIMIT)
            act = (up + 1.0) * gh * jax.nn.sigmoid(SWIGLU_ALPHA * gh)
            o_ref[rs, :] = o_ref[rs, :] + _dot(act.astype(BF16), wd)

        n_sub = MOE_ROWS // MOE_SUB

        @pl.when(rows > (n_sub - 1) * MOE_SUB)
        def _():
            ws = (wg_ref[...].astype(BF16), wu_ref[...].astype(BF16), wd_ref[...].astype(BF16))
            for i in range(n_sub):
                sub_tile(i, *ws)

        @pl.when(rows <= (n_sub - 1) * MOE_SUB)
        def _():
            ws = (wg_ref[...].astype(BF16), wu_ref[...].astype(BF16), wd_ref[...].astype(BF16))
            for i in range(n_sub - 1):
                pl.when(i * MOE_SUB < rows)(functools.partial(sub_tile, i, *ws))


def _experts(work_e, work_rows, work_blk, xbuf, w_gu, b_gu3, w_down, b_down3):
    n_work = work_e.shape[0]
    n_exp, d_model, two_ff = w_gu.shape
    d_ff = two_ff // 2
    nc = d_ff // MOE_FC

    def cidx(w, c, wr):
        return jnp.where(wr[w] > 0, c, nc - 1)

    grid_spec = pltpu.PrefetchScalarGridSpec(
        num_scalar_prefetch=3,
        grid=(n_work, nc),
        in_specs=[
            pl.BlockSpec((MOE_ROWS, d_model), lambda w, c, we, wr, wb: (wb[w], 0)),
            pl.BlockSpec((None, d_model, MOE_FC), lambda w, c, we, wr, wb: (we[w], 0, cidx(w, c, wr))),
            pl.BlockSpec((None, d_model, MOE_FC), lambda w, c, we, wr, wb: (we[w], 0, nc + cidx(w, c, wr))),
            pl.BlockSpec((None, 1, MOE_FC), lambda w, c, we, wr, wb: (we[w], 0, cidx(w, c, wr))),
            pl.BlockSpec((None, 1, MOE_FC), lambda w, c, we, wr, wb: (we[w], 0, nc + cidx(w, c, wr))),
            pl.BlockSpec((None, MOE_FC, d_model), lambda w, c, we, wr, wb: (we[w], cidx(w, c, wr), 0)),
            pl.BlockSpec((None, 1, d_model), lambda w, c, we, wr, wb: (we[w], 0, 0)),
        ],
        out_specs=pl.BlockSpec((MOE_ROWS, d_model), lambda w, c, we, wr, wb: (wb[w], 0)),
        scratch_shapes=[pltpu.VMEM((MOE_ROWS, d_model), BF16)],
    )
    return pl.pallas_call(
        _experts_kernel,
        grid_spec=grid_spec,
        out_shape=jax.ShapeDtypeStruct(xbuf.shape, F32),
        compiler_params=_cparams(("arbitrary", "arbitrary")),
        name="moe_experts",
    )(work_e, work_rows, work_blk, xbuf, w_gu, w_gu, b_gu3, b_gu3, w_down, b_down3)


def _combine_kernel(slot_ref, x1_ref, gate_ref, ybuf, o_ref, rows_ref, sem):
    tc = x1_ref.shape[0]

    def row_copy(i, k):
        return pltpu.make_async_copy(ybuf.at[pl.ds(slot_ref[i * TOP_K + k], 1)], rows_ref.at[k, pl.ds(i, 1)], sem)

    def issue(i, carry):
        for k in range(TOP_K):
            row_copy(i, k).start()
        return carry

    lax.fori_loop(0, tc, issue, 0)

    def drain(i, carry):
        for k in range(TOP_K):
            row_copy(i, k).wait()
        return carry

    lax.fori_loop(0, tc, drain, 0)
    gates = gate_ref[...]
    y = x1_ref[...]
    for k in range(TOP_K):
        y = y + gates[:, k:k + 1] * rows_ref[k]
    o_ref[...] = y


def _combine(slots_flat, x1, gates, ybuf, *, tc):
    n, d = x1.shape
    return pl.pallas_call(
        _combine_kernel,
        grid=(n // tc,),
        in_specs=[pl.BlockSpec((tc * TOP_K,), lambda i: (i,), memory_space=pltpu.SMEM),
                  pl.BlockSpec((tc, d), lambda i: (i, 0)),
                  pl.BlockSpec((tc, LANES), lambda i: (i, 0)),
                  pl.BlockSpec(memory_space=pl.ANY)],
        out_specs=pl.BlockSpec((tc, d), lambda i: (i, 0)),
        out_shape=jax.ShapeDtypeStruct((n, d), F32),
        scratch_shapes=[pltpu.VMEM((TOP_K, tc, d), F32), pltpu.SemaphoreType.DMA(())],
        compiler_params=_cparams(("arbitrary",)),
        name="moe_combine",
    )(slots_flat, x1, gates, ybuf)


def _moe(h2_p, h2_s, tope_p, tope_s, gate_p, gate_s, x1_p, x1_s, w_gu, b_gu, w_down, b_down):
    n_p, n_s = h2_p.shape[0], h2_s.shape[0]
    n_exp = w_gu.shape[0]
    pad = (-(n_p + n_s)) % ROUTE_TILE
    tope_all = jnp.concatenate([tope_p, tope_s, jnp.full((pad, LANES), -1, I32)], axis=0)
    rank, counts = _route(tope_all)
    counts = counts[0, :n_exp].astype(I32)
    n_assign = (n_p + n_s) * TOP_K
    n_work = n_assign // MOE_ROWS + n_exp
    items = (counts + MOE_ROWS - 1) // MOE_ROWS
    item_end = jnp.cumsum(items)
    item_start = item_end - items
    n_used = item_end[-1]
    w_ids = jnp.arange(n_work, dtype=I32)
    used = w_ids < n_used
    w_eff = jnp.where(used, w_ids, n_used - 1)
    work_e = jnp.minimum(jnp.searchsorted(item_end, w_eff, side="right"), n_exp - 1).astype(I32)
    work_rows = jnp.clip(counts[work_e] - (w_eff - item_start[work_e]) * MOE_ROWS, 0, MOE_ROWS)
    work_rows = jnp.where(used, work_rows, 0).astype(I32)
    base = (item_start * MOE_ROWS).astype(I32)
    e_all = tope_all[:n_p + n_s, :TOP_K]
    slots = (base[e_all] + rank[:n_p + n_s, :TOP_K]).reshape(-1)
    slots_p, slots_s = slots[:n_p * TOP_K], slots[n_p * TOP_K:]
    n_rows = n_work * MOE_ROWS
    xbuf = _dispatch(slots_p, h2_p, None, n_rows, tr=512)
    xbuf = _dispatch(slots_s, h2_s, xbuf, n_rows, tr=n_s)
    ybuf = _experts(work_e, work_rows, w_eff.astype(I32), xbuf, w_gu, b_gu[:, None, :], w_down, b_down[:, None, :])
    y_p = _combine(slots_p, x1_p, gate_p, ybuf, tc=128)
    y_s = _combine(slots_s, x1_s, gate_s, ybuf, tc=n_s)
    return y_p, y_s


def _head_perm():
    return [g * Q_PER_KV + r for r in range(Q_PER_KV) for g in range(KV_HEADS)]


def _pack_w_in(w_in):
    d_model = w_in.shape[0]
    pm_width = d_model - ATTN_WIDTH
    s0 = ATTN_WIDTH
    s1 = s0 + 2 * KV_WIDTH
    s2 = s1 + 2 * KV_WIDTH
    s3 = s2 + 2 * KV_WIDTH
    s4 = s3 + 3 * N_HEADS
    wq = w_in[:, :s0].reshape(d_model, N_HEADS, HEAD_DIM)[:, jnp.array(_head_perm())].reshape(d_model, ATTN_WIDTH)
    wg = w_in[:, s3:s4].reshape(d_model, N_HEADS, 3).transpose(0, 2, 1).reshape(d_model, 3 * N_HEADS)
    wg = jnp.pad(wg, ((0, 0), (0, GATE_PAD - 3 * N_HEADS)))
    return jnp.concatenate([wq, w_in[:, s0:s3], w_in[:, s4:s4 + pm_width], wg], axis=1).astype(BF16)


def _cmp_weights(w_cmp_k, b_cmp_k, w_cmp_v, b_cmp_v):
    def half(o):
        wk = jnp.tile(w_cmp_k[o * CMP_STRIDE:(o + 1) * CMP_STRIDE], (1, KV_HEADS))
        wv = jnp.tile(w_cmp_v[o * CMP_STRIDE:(o + 1) * CMP_STRIDE], (1, KV_HEADS))
        return jnp.concatenate([wk, wv], axis=1)
    bias = jnp.concatenate([jnp.tile(b_cmp_k, KV_HEADS), jnp.tile(b_cmp_v, KV_HEADS)])[None, :]
    return half(0), half(1), bias


def _cmp_to_slc_t(n_cmp_pad, n_cmp, n_slc, n_slc_pad):
    i0 = jnp.arange(n_cmp_pad)[None, :] * CMP_STRIDE
    j0 = jnp.arange(n_slc_pad)[:, None] * SLC_BLOCK
    shared = jnp.minimum(i0 + CMP_LEN, j0 + SLC_BLOCK) - jnp.maximum(i0, j0)
    frac = jnp.clip(shared, 0, None).astype(F32) / CMP_LEN
    ok = (jnp.arange(n_cmp_pad)[None, :] < n_cmp) & (jnp.arange(n_slc_pad)[:, None] < n_slc)
    return jnp.where(ok, frac, 0.0).astype(BF16)


def _block_expand(n_blk_pad, n_keys):
    return (jnp.arange(n_blk_pad)[:, None] == (jnp.arange(n_keys)[None, :] // SLC_BLOCK)).astype(BF16)


def _prompt_mixer(x_prompt, p):
    batch, seq, d_model = x_prompt.shape
    n = batch * seq
    q, kvc, kvs, kvw, kvs_b, kvw_b, gate, u, dpool = _inproj(
        x_prompt.reshape(n, d_model), p["g1"], p["w_in"], p["qg"], p["ksg"], p["kwg"], tm=256, seq_len=seq)
    npg = seq // PAGE_SIZE
    pt = (jnp.arange(batch, dtype=I32)[:, None] * npg + jnp.arange(npg, dtype=I32)[None, :])
    kc, vc = _compress(pt, kvc.reshape(batch * npg, PAGE_SIZE, 2 * KV_WIDTH), p["cw0"], p["cw1"], p["cb"], p["kcg"],
                       pages_per_step=8)
    n_cmp = seq // CMP_STRIDE - 1
    n_slc = -(-seq // SLC_BLOCK)
    ct = _cmp_to_slc_t(kc.shape[1], n_cmp, n_slc, LANES)
    ext = _block_expand(LANES, seq).T

    def v_tiles_t(kv_b):
        return jnp.swapaxes(kv_b[:, KV_WIDTH:].reshape(n // TK, TK, KV_WIDTH), 1, 2)

    o_attn = _prompt_attn(q, gate, kc, jnp.swapaxes(vc, 1, 2), kvs_b, v_tiles_t(kvs_b), kvw_b, v_tiles_t(kvw_b),
                          ct, ext, batch=batch, seq=seq)
    return o_attn, dpool, kvc, kvs, kvw, u


def _sample_mixer(x_sample, cache_cmp, cache_slc, state_win, state_pool, page_table, p):
    nb, t, d_model = x_sample.shape
    assert t == 1, "decode path handles one new row per sequence"
    npg = page_table.shape[1]
    past = npg * PAGE_SIZE
    qpos = past
    q, kvc, kvs, kvw, _, _, gate, u, dpool = _inproj(
        x_sample.reshape(nb, d_model), p["g1"], p["w_in"], p["qg"], p["ksg"], p["kwg"], tm=nb,
        pool_state=state_pool, hist_pos=qpos)
    def rows_on_lanes(a):
        return jnp.transpose(a, (0, 2, 3, 4, 1)).reshape(a.shape[0], 2 * KV_WIDTH, a.shape[1])

    cw0_t = jnp.tile(p["cw0"].T, (1, PAGE_SIZE // CMP_STRIDE))
    cw1_t = jnp.tile(p["cw1"].T, (1, PAGE_SIZE // CMP_STRIDE))
    a0, a1 = _compress_t(page_table, rows_on_lanes(cache_cmp), cw0_t, cw1_t, pages_per_step=LANES // CMP_SUB_PER_PAGE)
    n_cmp = (past + t) // CMP_STRIDE - 1
    n_slc = -(-(past + t) // SLC_BLOCK)
    n_pad = -(-n_slc // LANES) * LANES
    cmat = _cmp_to_slc_t(a0.shape[2], n_cmp, n_slc, n_pad).T
    slopes = jnp.array([SLOPES[g * Q_PER_KV + r] * LOG2E for r in range(Q_PER_KV) for g in range(KV_HEADS)], F32)[:, None]
    q3 = q.reshape(nb, 1, ATTN_WIDTH)
    oc, member = _decode_cmp(q3, slopes, a0, a1, p["cb"].T, p["kcg"][:, :HEAD_DIM].T, cmat, qpos=qpos, n_slc=n_slc)
    rows = jnp.arange(3 * N_HEADS)
    k_i, g_i, r_i = rows // N_HEADS, (rows % N_HEADS) // Q_PER_KV, rows % Q_PER_KV
    col_head = k_i * N_HEADS + r_i * KV_HEADS + g_i
    gexp = (jnp.arange(3 * ATTN_WIDTH)[None, :] // HEAD_DIM == col_head[:, None])
    gexp = jnp.pad(gexp, ((0, GATE_PAD - 3 * N_HEADS), (0, 0))).astype(BF16)
    o = _decode_sw(page_table, q3, slopes, oc, member, gate.reshape(nb, 1, GATE_PAD), gexp,
                   rows_on_lanes(state_win), kvs.reshape(nb, 1, 2 * KV_WIDTH), kvw.reshape(nb, 1, 2 * KV_WIDTH),
                   rows_on_lanes(cache_slc), pages_per_step=8, qpos=qpos, past=past)
    return o.reshape(nb, ATTN_WIDTH), dpool, kvc, kvs, kvw, u


def _prep_params(norm1_g, w_in, q_gain, k_cmp_gain, k_slc_gain, k_win_gain, w_cmp_k, b_cmp_k, w_cmp_v, b_cmp_v):
    cw0, cw1, cb = _cmp_weights(w_cmp_k, b_cmp_k, w_cmp_v, b_cmp_v)
    return {
        "g1": norm1_g[None, :],
        "w_in": _pack_w_in(w_in),
        "qg": jnp.tile(q_gain, N_HEADS)[None, :],
        "ksg": jnp.tile(k_slc_gain, KV_HEADS)[None, :],
        "kwg": jnp.tile(k_win_gain, KV_HEADS)[None, :],
        "kcg": jnp.tile(k_cmp_gain, KV_HEADS)[None, :],
        "cw0": cw0, "cw1": cw1, "cb": cb,
    }


def _prep_out_params(w_pm, s_pm, w_out, norm2_g, w_router, b_router):
    d_model = w_out.shape[0]
    woa = w_out[:ATTN_WIDTH].reshape(N_HEADS, HEAD_DIM, d_model)[jnp.array(_head_perm())].reshape(ATTN_WIDTH, d_model)
    wr = jnp.pad(w_router, ((0, 0), (0, LANES - w_router.shape[1])))
    wrh, wrl = _split_bf16(wr)
    return (w_pm.astype(BF16), s_pm.reshape(1, -1), woa.astype(BF16), w_out[ATTN_WIDTH:].astype(BF16),
            norm2_g[None, :], wrh, wrl, jnp.pad(b_router, (0, LANES - b_router.shape[0]))[None, :])


def _layer(x_prompt, x_sample, cache_cmp, cache_slc, state_win, state_pool, page_table,
           norm1_g, w_in, q_gain, k_cmp_gain, k_slc_gain, k_win_gain, w_cmp_k, b_cmp_k, w_cmp_v, b_cmp_v,
           w_pm, s_pm, w_out, norm2_g, w_router, b_router, w_gu, b_gu, w_down, b_down):
    batch, seq, d_model = x_prompt.shape
    nb, t = x_sample.shape[:2]
    p = _prep_params(norm1_g, w_in, q_gain, k_cmp_gain, k_slc_gain, k_win_gain, w_cmp_k, b_cmp_k, w_cmp_v, b_cmp_v)
    oa_p, d_p, kvc_p, kvs_p, kvw_p, u_p = _prompt_mixer(x_prompt, p)
    oa_s, d_s, kvc_s, kvs_s, kvw_s, u_s = _sample_mixer(x_sample, cache_cmp, cache_slc, state_win, state_pool, page_table, p)
    op = _prep_out_params(w_pm, s_pm, w_out, norm2_g, w_router, b_router)
    x1_p, h2_p, te_p, gt_p = _outproj(x_prompt.reshape(batch * seq, d_model), oa_p, d_p, *op, tm=256)
    x1_s, h2_s, te_s, gt_s = _outproj(x_sample.reshape(nb * t, d_model), oa_s, d_s, *op, tm=nb * t)
    y_p, y_s = _moe(h2_p, h2_s, te_p, te_s, gt_p, gt_s, x1_p, x1_s, w_gu, b_gu, w_down, b_down)
    kv_shape = (2, KV_HEADS, HEAD_DIM)
    n_win = min(WINDOW, seq)
    st_p = (kvc_p.reshape(batch, seq, *kv_shape), kvs_p.reshape(batch, seq, *kv_shape),
            kvw_p.reshape(batch, seq, *kv_shape)[:, seq - n_win:], u_p.reshape(batch, seq, -1)[:, seq - POOL_HIST:])
    kvw_s5 = kvw_s.reshape(nb, t, *kv_shape)
    u_s3 = u_s.reshape(nb, t, -1)
    st_s = (kvc_s.reshape(nb, t, *kv_shape), kvs_s.reshape(nb, t, *kv_shape),
            jnp.concatenate([state_win, kvw_s5], axis=1)[:, t:], jnp.concatenate([state_pool, u_s3], axis=1)[:, t:])
    return y_p.reshape(batch, seq, d_model), y_s.reshape(nb, t, d_model), st_p, st_s


def kernel(x_prompt, x_sample, cache_cmp_kv, cache_slc_kv, state_win_kv, state_pool, page_table, norm1_g, w_in, q_gain, k_cmp_gain, k_slc_gain, k_win_gain, w_cmp_k, b_cmp_k, w_cmp_v, b_cmp_v, w_pm, s_pm, w_out, norm2_g, w_router, b_router, w_gu, b_gu, w_down, b_down):
    layer_params = (norm1_g, w_in, q_gain, k_cmp_gain, k_slc_gain, k_win_gain, w_cmp_k, b_cmp_k, w_cmp_v, b_cmp_v,
                    w_pm, s_pm, w_out, norm2_g, w_router, b_router, w_gu, b_gu, w_down, b_down)
    y_p, y_s = x_prompt, x_sample
    p_states, s_states = [], []
    for layer in range(norm1_g.shape[0]):
        lw = [w[layer] for w in layer_params]
        y_p, y_s, st_p, st_s = _layer(y_p, y_s, cache_cmp_kv[layer], cache_slc_kv[layer], state_win_kv[layer],
                                      state_pool[layer], page_table, *lw)
        p_states.append(st_p)
        s_states.append(st_s)
    stack = lambda states, i: jnp.stack([s[i] for s in states], axis=0)
    return (y_p, y_s,
            stack(p_states, 0), stack(p_states, 1), stack(p_states, 2), stack(p_states, 3),
            stack(s_states, 0), stack(s_states, 1), stack(s_states, 2), stack(s_states, 3))
```

```python
import functools
import math

import jax
import jax.numpy as jnp
from jax import lax
from jax.experimental import pallas as pl
from jax.experimental.pallas import tpu as pltpu

F32 = jnp.float32
BF16 = jnp.bfloat16
I32 = jnp.int32

N_HEADS = 16
HEAD_DIM = 64
KV_HEADS = 4
Q_PER_KV = N_HEADS // KV_HEADS
ATTN_WIDTH = N_HEADS * HEAD_DIM
KV_WIDTH = KV_HEADS * HEAD_DIM
CMP_LEN = 32
CMP_STRIDE = 16
SLC_BLOCK = 64
N_SELECT = 16
WINDOW = 512
FORCE_SCORE = 1e4
NEG_INF = -1e30
POOL_WINDOWS = (2, 4, 8, 16)
MAX_POOL_W = max(POOL_WINDOWS)
POOL_HIST = MAX_POOL_W - 1
N_EXPERTS = 32
TOP_K = 4
SWIGLU_LIMIT = 7.0
SWIGLU_ALPHA = 1.702
EPS = 1e-6
PAGE_SIZE = 128

LANES = 128
VMEM_LIMIT = 56 * 1024 * 1024

GATE_PAD = LANES
SLOPES = [2.0 ** (-8.0 * (h + 1) / N_HEADS) for h in range(N_HEADS)]
LOG2E = math.log2(math.e)


def _cparams(sem):
    return pltpu.CompilerParams(dimension_semantics=sem, vmem_limit_bytes=VMEM_LIMIT)


def _iota(shape, dim):
    return lax.broadcasted_iota(I32, shape, dim)


def _split_bf16(x):
    hi = x.astype(BF16)
    lo = (x - hi.astype(F32)).astype(BF16)
    return hi, lo


def _dot(a, b):
    return jnp.dot(a, b, preferred_element_type=F32)


def _dot_t(a, b):
    return lax.dot_general(a, b, (((1,), (1,)), ((), ())), preferred_element_type=F32)


def _head_mean_sq(z):
    m, w = z.shape
    ones_bd = jnp.where(_iota((256, 256), 0) // HEAD_DIM == _iota((256, 256), 1) // HEAD_DIM, 1.0, 0.0).astype(BF16)
    zz = z * z
    hi, lo = _split_bf16(zz)
    parts = []
    for c in range(w // 256):
        sl = slice(c * 256, (c + 1) * 256)
        parts.append(_dot(hi[:, sl], ones_bd) + _dot(lo[:, sl], ones_bd))
    ss = parts[0] if len(parts) == 1 else jnp.concatenate(parts, axis=1)
    return ss * (1.0 / HEAD_DIM)


C_Q = 0
C_KVC = ATTN_WIDTH
C_KVS = C_KVC + 2 * KV_WIDTH
C_KVW = C_KVS + 2 * KV_WIDTH
C_U = C_KVW + 2 * KV_WIDTH


def _inproj_kernel(x_ref, g1_ref, w_ref, qg_ref, ksg_ref, kwg_ref, *refs, tm, pm_width, seq_tiles, hist_pos):
    if seq_tiles is None:
        sp_ref, refs = refs[0], refs[1:]
    q_out, kvc_out, kvs_out, kvw_out, gate_out, u_out, d_out = refs[:7]
    x = x_ref[...]
    ms = jnp.mean(x * x, axis=-1, keepdims=True)
    h = (x * lax.rsqrt(ms + EPS)) * g1_ref[...]
    hb = h.astype(BF16)
    c_gate = C_U + pm_width

    zq = _dot(hb, w_ref[:, C_Q:C_KVC])
    qn = (zq * lax.rsqrt(_head_mean_sq(zq) + EPS)) * qg_ref[...]
    q_out[...] = (qn * (HEAD_DIM ** -0.5 * LOG2E)).astype(BF16)

    kvc_out[...] = _dot(hb, w_ref[:, C_KVC:C_KVS])

    zs = _dot(hb, w_ref[:, C_KVS:C_KVW])
    ks = zs[:, :KV_WIDTH]
    ks = (ks * lax.rsqrt(_head_mean_sq(ks) + EPS)) * ksg_ref[...]
    kvs = jnp.concatenate([ks, zs[:, KV_WIDTH:]], axis=1)
    kvs_out[...] = kvs

    zw = _dot(hb, w_ref[:, C_KVW:C_U])
    kw = zw[:, :KV_WIDTH]
    kw = (kw * lax.rsqrt(_head_mean_sq(kw) + EPS)) * kwg_ref[...]
    kvw = jnp.concatenate([kw, zw[:, KV_WIDTH:]], axis=1)
    kvw_out[...] = kvw
    if seq_tiles is not None:
        ksb_out, vst_out, kwb_out, vwt_out = refs[7:11]
        ksb_out[...] = ks.astype(BF16)
        vst_out[...] = zs[:, KV_WIDTH:].T.astype(BF16)
        kwb_out[...] = kw.astype(BF16)
        vwt_out[...] = zw[:, KV_WIDTH:].T.astype(BF16)

    gate_out[...] = jax.nn.sigmoid(_dot(hb, w_ref[:, c_gate:c_gate + GATE_PAD]))

    u = _dot(hb, w_ref[:, C_U:c_gate])
    u_out[...] = u

    pm_group = pm_width // len(POOL_WINDOWS)
    if seq_tiles is None:
        tpos = float(hist_pos + 1)
        for gi, w in enumerate(POOL_WINDOWS):
            cs = slice(gi * pm_group, (gi + 1) * pm_group)
            s = u[:, cs]
            for k in range(1, w):
                s = s + sp_ref[:, POOL_HIST - k, cs]
            d_out[:, cs] = (s / min(float(w), tpos) - u[:, cs]).astype(BF16)
    else:
        ext_ref = refs[11]
        j = pl.program_id(0) % seq_tiles

        @pl.when(j == 0)
        def _():
            ext_ref[0:MAX_POOL_W, :] = jnp.zeros((MAX_POOL_W, pm_width), F32)

        ext_ref[MAX_POOL_W:MAX_POOL_W + tm, :] = u
        tpos = (j * tm + _iota((tm, 1), 0) + 1).astype(F32)
        for gi, w in enumerate(POOL_WINDOWS):
            cs = slice(gi * pm_group, (gi + 1) * pm_group)
            s = ext_ref[MAX_POOL_W:MAX_POOL_W + tm, cs]
            for k in range(1, w):
                s = s + ext_ref[MAX_POOL_W - k:MAX_POOL_W - k + tm, cs]
            cnt = jnp.minimum(float(w), tpos)
            d_out[:, cs] = (s / cnt - u[:, cs]).astype(BF16)
        ext_ref[0:MAX_POOL_W, :] = ext_ref[tm:tm + MAX_POOL_W, :]


def _inproj(x2d, g1, w_packed, qg, ksg, kwg, *, tm, seq_len=None, pool_state=None, hist_pos=None):
    n, d_model = x2d.shape
    pm_width = d_model - ATTN_WIDTH
    seq_tiles = None if seq_len is None else seq_len // tm
    row = lambda w: pl.BlockSpec((tm, w), lambda i: (i, 0))
    full = lambda a: pl.BlockSpec(a.shape, lambda i: (0,) * a.ndim)
    out_shape = [
        jax.ShapeDtypeStruct((n, ATTN_WIDTH), BF16),
        jax.ShapeDtypeStruct((n, 2 * KV_WIDTH), F32),
        jax.ShapeDtypeStruct((n, 2 * KV_WIDTH), F32),
        jax.ShapeDtypeStruct((n, 2 * KV_WIDTH), F32),
        jax.ShapeDtypeStruct((n, GATE_PAD), F32),
        jax.ShapeDtypeStruct((n, pm_width), F32),
        jax.ShapeDtypeStruct((n, pm_width), BF16),
    ]
    out_specs = [row(ATTN_WIDTH), row(2 * KV_WIDTH), row(2 * KV_WIDTH), row(2 * KV_WIDTH),
                 row(GATE_PAD), row(pm_width), row(pm_width)]
    in_specs = [row(d_model), full(g1), full(w_packed), full(qg), full(ksg), full(kwg)]
    args = [x2d, g1, w_packed, qg, ksg, kwg]
    scratch = []
    if seq_tiles is None:
        in_specs.append(pl.BlockSpec((tm, POOL_HIST, pm_width), lambda i: (i, 0, 0)))
        args.append(pool_state)
    else:
        per_tile = TK // tm
        vt_spec = pl.BlockSpec((None, KV_WIDTH, tm), lambda i: (i // per_tile, 0, i % per_tile))
        for _ in range(2):
            out_shape += [jax.ShapeDtypeStruct((n, KV_WIDTH), BF16), jax.ShapeDtypeStruct((n // TK, KV_WIDTH, TK), BF16)]
            out_specs += [row(KV_WIDTH), vt_spec]
        scratch.append(pltpu.VMEM((tm + MAX_POOL_W, pm_width), F32))
    return pl.pallas_call(
        functools.partial(_inproj_kernel, tm=tm, pm_width=pm_width, seq_tiles=seq_tiles, hist_pos=hist_pos),
        grid=(n // tm,),
        in_specs=in_specs,
        out_specs=out_specs,
        out_shape=out_shape,
        scratch_shapes=scratch,
        compiler_params=_cparams(("arbitrary",)),
        name="inproj",
    )(*args)


CMP_SUB_PER_PAGE = PAGE_SIZE // CMP_STRIDE


def _compress_kernel(pt_ref, *refs, pages_per_step):
    del pt_ref
    p = pages_per_step
    page_refs = refs[:p]
    halo_ref, w0_ref, w1_ref, b_ref, kg_ref, kc_out, vc_out, xs_ref = refs[p:]
    n_lt = 2 * KV_WIDTH // LANES
    nb = p * CMP_SUB_PER_PAGE
    parts = []
    for c in range(n_lt):
        cs = slice(c * LANES, (c + 1) * LANES)
        for k in range(p):
            xs_ref[c, k * PAGE_SIZE:(k + 1) * PAGE_SIZE, :] = page_refs[k][:, cs]
        xs_ref[c, p * PAGE_SIZE:p * PAGE_SIZE + CMP_STRIDE, :] = halo_ref[:, cs]
        acc = jnp.zeros((nb, LANES), F32) + b_ref[:, cs]
        for j in range(CMP_STRIDE):
            acc = acc + xs_ref[c, pl.ds(j, nb, stride=CMP_STRIDE), :] * w0_ref[j:j + 1, cs]
            acc = acc + xs_ref[c, pl.ds(CMP_STRIDE + j, nb, stride=CMP_STRIDE), :] * w1_ref[j:j + 1, cs]
        parts.append(acc)
    acc = jnp.concatenate(parts, axis=1)
    kc = acc[:, :KV_WIDTH]
    kc = (kc * lax.rsqrt(_head_mean_sq(kc) + EPS)) * kg_ref[...]
    kc_out[...] = kc.astype(BF16)
    vc_out[...] = acc[:, KV_WIDTH:].astype(BF16)


def _compress(page_table, pages, w0t, w1t, bias, kgain, *, pages_per_step):
    nb, npg = page_table.shape
    p = pages_per_step
    steps = npg // p

    def page_spec(k):
        return pl.BlockSpec((None, PAGE_SIZE, 2 * KV_WIDTH), lambda b, i, pt: (pt[b, i * p + k], 0, 0))

    halo_spec = pl.BlockSpec((None, CMP_STRIDE, 2 * KV_WIDTH),
                             lambda b, i, pt: (pt[b, jnp.minimum(i * p + p, npg - 1)], 0, 0))
    full = lambda a: pl.BlockSpec(a.shape, lambda b, i, pt: (0,) * a.ndim)
    out_spec = pl.BlockSpec((None, p * CMP_SUB_PER_PAGE, KV_WIDTH), lambda b, i, pt: (b, i, 0))
    grid_spec = pltpu.PrefetchScalarGridSpec(
        num_scalar_prefetch=1,
        grid=(nb, steps),
        in_specs=[page_spec(k) for k in range(p)] + [halo_spec, full(w0t), full(w1t), full(bias), full(kgain)],
        out_specs=[out_spec, out_spec],
        scratch_shapes=[pltpu.VMEM((2 * KV_WIDTH // LANES, p * PAGE_SIZE + CMP_STRIDE, LANES), F32)],
    )
    return pl.pallas_call(
        functools.partial(_compress_kernel, pages_per_step=p),
        grid_spec=grid_spec,
        out_shape=[jax.ShapeDtypeStruct((nb, npg * CMP_SUB_PER_PAGE, KV_WIDTH), BF16)] * 2,
        compiler_params=_cparams(("arbitrary", "arbitrary")),
        name="compress",
    )(page_table, *([pages] * p), pages, w0t, w1t, bias, kgain)


TQ = 128
TK = 512
QROWS = Q_PER_KV * TQ


def _select_members_t(score_t, n_cand):
    rows = _iota((n_cand, 1), 0)
    rank = jnp.zeros(score_t.shape, F32)
    for i in range(n_cand):
        si = score_t[i:i + 1, :]
        ahead = jnp.where(si > score_t, 1.0, jnp.where(si == score_t, jnp.where(rows > i, 1.0, 0.0), 0.0))
        rank = rank + ahead
    return jnp.where(rank < float(N_SELECT), 1.0, 0.0)


def _prompt_attn_kernel(q_ref, gate_ref, kc_ref, vct_ref, ks_ref, vst_ref, kw_ref, vwt_ref, ct_ref, ext_ref,
                        o_ref, qpad_ref, memb_ref, tot_ref, *stat_refs, n_slc):
    qi = pl.program_id(1)
    q0 = qi * TQ
    lane = _iota((1, QROWS), 1)
    r_lane = lane // TQ
    qidx = q0 + lane % TQ
    qpos_l = q0 + _iota((1, TQ), 1)
    lane_g = _iota((1, KV_WIDTH), 1) // HEAD_DIM
    n_cmp_pad = kc_ref.shape[0]
    n_kt = memb_ref.shape[1]
    gates_t = gate_ref[...].T

    def slope_row(g):
        row = jnp.zeros((1, QROWS), F32)
        for r in range(Q_PER_KV):
            row = jnp.where(r_lane == r, SLOPES[g * Q_PER_KV + r] * LOG2E, row)
        return row

    def gate_row(g, k):
        c = k * N_HEADS + g * Q_PER_KV
        return jnp.concatenate([gates_t[c + r:c + r + 1, :] for r in range(Q_PER_KV)], axis=1)

    c_start = _iota((n_cmp_pad, QROWS), 0) * CMP_STRIDE
    c_mid = c_start.astype(F32) + 0.5 * (CMP_LEN - 1)
    cmask = (c_start + (CMP_LEN - 1)) <= qidx
    kidx = _iota((TK, TQ), 0)

    for g in range(KV_HEADS):
        qpad = jnp.concatenate(
            [jnp.where(lane_g == g, q_ref[:, r * KV_WIDTH:(r + 1) * KV_WIDTH], jnp.zeros((), BF16))
             for r in range(Q_PER_KV)], axis=0)
        qpad_ref[g] = qpad
        s = _dot_t(kc_ref[...], qpad) + slope_row(g) * c_mid
        s = jnp.where(cmask, s, NEG_INF)
        e = jnp.where(cmask, jnp.exp2(s - jnp.max(s, axis=0, keepdims=True)), 0.0)
        l = jnp.sum(e, axis=0, keepdims=True)
        pc = e * jnp.where(l > 0.0, 1.0 / l, 0.0)
        oc = _dot(vct_ref[g * HEAD_DIM:(g + 1) * HEAD_DIM, :], pc.astype(BF16))
        tot_ref[g] = oc * gate_row(g, 0)
        p_sum = pc[:, 0:TQ]
        for r in range(1, Q_PER_KV):
            p_sum = p_sum + pc[:, r * TQ:(r + 1) * TQ]
        hi, lo = _split_bf16(p_sum)
        imp_t = (_dot(ct_ref[...], hi) + _dot(ct_ref[...], lo))[0:n_slc]
        blk = _iota((n_slc, 1), 0)
        qblk = qpos_l // SLC_BLOCK
        forced = (blk == 0) | (blk == qblk) | (blk == qblk - 1)
        score_t = jnp.where(forced, FORCE_SCORE, jnp.where(blk * SLC_BLOCK <= qpos_l, imp_t, NEG_INF))
        member_t = _select_members_t(score_t, n_slc)
        member_t = jnp.concatenate([member_t, jnp.zeros((LANES - n_slc, TQ), F32)], axis=0).astype(BF16)
        memb_keys = _dot(ext_ref[...], member_t)
        for j in range(n_kt):
            ok = jnp.where(kidx + j * TK <= qpos_l, memb_keys[j * TK:(j + 1) * TK, :], 0.0)
            memb_ref[g, j] = jnp.where(ok > 0.5, 0.0, NEG_INF)

    m_refs, l_refs, acc_refs = (stat_refs[i * KV_HEADS:(i + 1) * KV_HEADS] for i in range(3))

    def sweep(k_ref, vt_ref, lo_t, hi_t, shared_fn, mask_fn, gate_k):
        for g in range(KV_HEADS):
            m_refs[g][...] = jnp.full(m_refs[g].shape, NEG_INF, F32)
            l_refs[g][...] = jnp.zeros(l_refs[g].shape, F32)
            acc_refs[g][...] = jnp.zeros(acc_refs[g].shape, F32)

        def body(kj, carry):
            k0 = pl.multiple_of(kj * TK, TK)
            kt = k_ref[pl.ds(k0, TK), :]
            vt = vt_ref[kj]
            kpos = (kidx + k0).astype(F32)
            shared = shared_fn(k0)
            for g in range(KV_HEADS):
                s_all = _dot_t(kt, qpad_ref[g])
                mask = mask_fn(g, kj, shared)
                vg = vt[g * HEAD_DIM:(g + 1) * HEAD_DIM, :]
                for r in range(Q_PER_KV):
                    cs = slice(r * TQ, (r + 1) * TQ)
                    s = s_all[:, cs] + (SLOPES[g * Q_PER_KV + r] * LOG2E) * kpos + mask
                    m_old = m_refs[g][:, cs]
                    m_new = jnp.maximum(m_old, jnp.max(s, axis=0, keepdims=True))
                    alpha = jnp.exp2(m_old - m_new)
                    p = jnp.exp2(s - m_new)
                    l_refs[g][:, cs] = alpha * l_refs[g][:, cs] + jnp.sum(p, axis=0, keepdims=True)
                    acc_refs[g][:, cs] = alpha * acc_refs[g][:, cs] + _dot(vg, p.astype(BF16))
                    m_refs[g][:, cs] = m_new
            return carry

        lax.fori_loop(lo_t, hi_t, body, 0)
        for g in range(KV_HEADS):
            l = l_refs[g][...]
            tot_ref[g] = tot_ref[g] + acc_refs[g][...] * (jnp.where(l > 0.0, 1.0 / l, 0.0) * gate_row(g, gate_k))

    n_hi = (q0 + TQ + TK - 1) // TK
    sweep(ks_ref, vst_ref, 0, n_hi, lambda k0: None, lambda g, kj, shared: memb_ref[g, kj], 1)

    def win_mask(k0):
        d = kidx - _iota((TK, TQ), 1) + (k0 - q0)
        return jnp.where(d <= 0, jnp.where(d > -WINDOW, 0.0, NEG_INF), NEG_INF)

    w_lo = jnp.maximum(q0 - WINDOW, 0) // TK
    sweep(kw_ref, vwt_ref, w_lo, n_hi, win_mask, lambda g, kj, shared: shared, 2)

    total = jnp.concatenate([tot_ref[g] for g in range(KV_HEADS)], axis=0)
    for r in range(Q_PER_KV):
        o_ref[:, r * KV_WIDTH:(r + 1) * KV_WIDTH] = total[:, r * TQ:(r + 1) * TQ].T.astype(BF16)


def _prompt_attn(q, gate, kc, vct, kvs_b, vst, kvw_b, vwt, ct, ext, *, batch, seq):
    n_slc = -(-seq // SLC_BLOCK)
    n_cmp_pad = kc.shape[1]
    n_qt = seq // TQ
    n_kt = seq // TK
    rowblk = lambda w: pl.BlockSpec((TQ, w), lambda b, i: (b * n_qt + i, 0))
    kblk = pl.BlockSpec((seq, KV_WIDTH), lambda b, i: (b, 0))
    vtblk = pl.BlockSpec((n_kt, KV_WIDTH, TK), lambda b, i: (b, 0, 0))
    full = lambda a: pl.BlockSpec(a.shape, lambda b, i: (0,) * a.ndim)
    return pl.pallas_call(
        functools.partial(_prompt_attn_kernel, n_slc=n_slc),
        grid=(batch, n_qt),
        in_specs=[rowblk(ATTN_WIDTH), rowblk(GATE_PAD),
                  pl.BlockSpec((None, n_cmp_pad, KV_WIDTH), lambda b, i: (b, 0, 0)),
                  pl.BlockSpec((None, KV_WIDTH, n_cmp_pad), lambda b, i: (b, 0, 0)),
                  kblk, vtblk, kblk, vtblk, full(ct), full(ext)],
        out_specs=rowblk(ATTN_WIDTH),
        out_shape=jax.ShapeDtypeStruct((batch * seq, ATTN_WIDTH), BF16),
        scratch_shapes=[pltpu.VMEM((KV_HEADS, QROWS, KV_WIDTH), BF16),
                        pltpu.VMEM((KV_HEADS, n_kt, TK, TQ), F32),
                        pltpu.VMEM((KV_HEADS, HEAD_DIM, QROWS), F32)]
                       + [pltpu.VMEM((1, QROWS), F32)] * (2 * KV_HEADS) + [pltpu.VMEM((HEAD_DIM, QROWS), F32)] * KV_HEADS,
        compiler_params=_cparams(("arbitrary", "arbitrary")),
        name="prompt_attn",
    )(q, gate, kc, vct, kvs_b, vst, kvw_b, vwt, ct, ext)


DROWS = N_HEADS
NEVER = -3e38


def _decode_qpad(q_ref):
    lane_g = _iota((KV_HEADS, KV_WIDTH), 1) // HEAD_DIM
    row_g = _iota((KV_HEADS, KV_WIDTH), 0)
    parts = []
    for r in range(Q_PER_KV):
        qr = jnp.broadcast_to(q_ref[:, r * KV_WIDTH:(r + 1) * KV_WIDTH].astype(F32), (KV_HEADS, KV_WIDTH))
        parts.append(jnp.where(lane_g == row_g, qr, 0.0))
    return jnp.concatenate(parts, axis=0).astype(BF16)


def _compress_t_kernel(pt_ref, *refs, pages_per_step):
    del pt_ref
    p = pages_per_step
    page_refs = refs[:p]
    w0_ref, w1_ref, a0_out, a1_out = refs[p:]
    prow = _iota((PAGE_SIZE, LANES), 0)
    ocol = _iota((PAGE_SIZE, LANES), 1)
    a0 = jnp.zeros(a0_out.shape, F32)
    a1 = jnp.zeros(a1_out.shape, F32)
    for k in range(p):
        sel = jnp.where(ocol == k * CMP_SUB_PER_PAGE + prow // CMP_STRIDE, 1.0, 0.0).astype(BF16)
        x = page_refs[k][...]
        a0 = a0 + _dot((x * w0_ref[...]).astype(BF16), sel)
        a1 = a1 + _dot((x * w1_ref[...]).astype(BF16), sel)
    a0_out[...] = a0
    a1_out[...] = a1


def _compress_t(page_table, pages_t, w0t, w1t, *, pages_per_step):
    nb, npg = page_table.shape
    p = pages_per_step
    assert p * CMP_SUB_PER_PAGE == LANES

    def page_spec(k):
        return pl.BlockSpec((None, 2 * KV_WIDTH, PAGE_SIZE), lambda b, i, pt: (pt[b, i * p + k], 0, 0))

    full = lambda a: pl.BlockSpec(a.shape, lambda b, i, pt: (0,) * a.ndim)
    out_spec = pl.BlockSpec((None, 2 * KV_WIDTH, LANES), lambda b, i, pt: (b, 0, i))
    grid_spec = pltpu.PrefetchScalarGridSpec(
        num_scalar_prefetch=1,
        grid=(nb, npg // p),
        in_specs=[page_spec(k) for k in range(p)] + [full(w0t), full(w1t)],
        out_specs=[out_spec, out_spec],
    )
    return pl.pallas_call(
        functools.partial(_compress_t_kernel, pages_per_step=p),
        grid_spec=grid_spec,
        out_shape=[jax.ShapeDtypeStruct((nb, 2 * KV_WIDTH, npg * CMP_SUB_PER_PAGE), F32)] * 2,
        compiler_params=_cparams(("arbitrary", "arbitrary")),
        name="compress_t",
    )(page_table, *([pages_t] * p), w0t, w1t)


def _decode_cmp_kernel(q_ref, slope_ref, a0_ref, a1_ref, cb_ref, kg_ref, c_ref, oc_out, member_out, *, qpos, n_slc):
    qpad = _decode_qpad(q_ref)
    slope = slope_ref[...]
    n_cmp_pad = a0_ref.shape[1]
    acc = a0_ref[...] + pltpu.roll(a1_ref[...], n_cmp_pad - 1, 1) + cb_ref[...]
    kparts = []
    for g in range(KV_HEADS):
        kg = acc[g * HEAD_DIM:(g + 1) * HEAD_DIM, :]
        ms = jnp.mean(kg * kg, axis=0, keepdims=True)
        kparts.append((kg * lax.rsqrt(ms + EPS)) * kg_ref[...])
    kc_t = jnp.concatenate(kparts, axis=0).astype(BF16)
    vc_t = acc[KV_WIDTH:, :].astype(BF16)
    c_start = _iota((1, n_cmp_pad), 1) * CMP_STRIDE
    c_mid = c_start.astype(F32) + 0.5 * (CMP_LEN - 1)
    cmask = (c_start + (CMP_LEN - 1)) <= qpos
    s = _dot(qpad, kc_t) + slope * c_mid
    s = jnp.where(cmask, s, NEG_INF)
    e = jnp.where(cmask, jnp.exp2(s - jnp.max(s, axis=-1, keepdims=True)), 0.0)
    l = jnp.sum(e, axis=-1, keepdims=True)
    pc = e * jnp.where(l > 0.0, 1.0 / l, 0.0)
    oc_out[...] = _dot_t(pc.astype(BF16), vc_t)
    p_sum = pc[0:KV_HEADS]
    for r in range(1, Q_PER_KV):
        p_sum = p_sum + pc[r * KV_HEADS:(r + 1) * KV_HEADS]
    p_sum = jnp.concatenate([p_sum, jnp.zeros((DROWS - KV_HEADS, n_cmp_pad), F32)], axis=0)
    hi, lo = _split_bf16(p_sum)
    imp = _dot(hi, c_ref[...]) + _dot(lo, c_ref[...])
    n_pad = imp.shape[1]
    blk = _iota((1, n_pad), 1)
    qblk = qpos // SLC_BLOCK
    forced = (blk == 0) | (blk == qblk) | (blk == qblk - 1)
    score = jnp.where(forced, FORCE_SCORE, jnp.where(blk * SLC_BLOCK <= qpos, imp, NEG_INF))
    score = jnp.where(blk < n_slc, score, NEVER)
    rank = jnp.zeros(score.shape, F32)
    for i in range(n_slc):
        si = score[:, i:i + 1]
        rank = rank + jnp.where(si > score, 1.0, jnp.where(si == score, jnp.where(blk > i, 1.0, 0.0), 0.0))
    member_out[...] = jnp.where(rank < float(min(N_SELECT, n_slc)), 1.0, 0.0)


def _decode_cmp(q3, slopes, a0, a1, cb_col, kg_col, cmat, *, qpos, n_slc):
    nb = q3.shape[0]
    n_cmp_pad = a0.shape[2]
    n_pad = cmat.shape[1]
    per_b = lambda r, c: pl.BlockSpec((None, r, c), lambda b: (b, 0, 0))
    full = lambda a: pl.BlockSpec(a.shape, lambda b: (0,) * a.ndim)
    return pl.pallas_call(
        functools.partial(_decode_cmp_kernel, qpos=qpos, n_slc=n_slc),
        grid=(nb,),
        in_specs=[per_b(1, ATTN_WIDTH), full(slopes), per_b(2 * KV_WIDTH, n_cmp_pad), per_b(2 * KV_WIDTH, n_cmp_pad),
                  full(cb_col), full(kg_col), full(cmat)],
        out_specs=[per_b(DROWS, KV_WIDTH), per_b(DROWS, n_pad)],
        out_shape=[jax.ShapeDtypeStruct((nb, DROWS, KV_WIDTH), F32), jax.ShapeDtypeStruct((nb, DROWS, n_pad), F32)],
        compiler_params=_cparams(("arbitrary",)),
        name="decode_cmp",
    )(q3, slopes, a0, a1, cb_col, kg_col, cmat)


def _decode_sw_kernel(pt_ref, q_ref, slope_ref, oc_ref, member_ref, gate_ref, gexp_ref, win_ref, news_ref, neww_ref,
                      *refs, pages_per_step, qpos, past, n_buf):
    del pt_ref
    p = pages_per_step
    page_refs = refs[:p]
    o_ref, m_ref, l_ref, acc_ref, ow_ref = refs[p:]
    c = pl.program_id(1)
    qpad = _decode_qpad(q_ref)
    qf = qpad.astype(F32)
    slope = slope_ref[...]

    def new_row(row_ref):
        kn = row_ref[:, :KV_WIDTH].astype(BF16).astype(F32)
        vn = row_ref[:, KV_WIDTH:].astype(BF16).astype(F32)
        return jnp.sum(qf * kn, axis=-1, keepdims=True) + slope * float(qpos), vn

    @pl.when(c == 0)
    def _():
        m_ref[...] = jnp.full((DROWS, 1), NEG_INF, F32)
        l_ref[...] = jnp.zeros((DROWS, 1), F32)
        acc_ref[...] = jnp.zeros((DROWS, KV_WIDTH), F32)
        kw_t = win_ref[:KV_WIDTH, :].astype(BF16)
        vw_t = win_ref[KV_WIDTH:, :].astype(BF16)
        wpos = past - n_buf + _iota((1, n_buf), 1)
        ok = (wpos <= qpos) & (wpos > qpos - WINDOW) & (wpos >= 0)
        s = jnp.where(ok, _dot(qpad, kw_t) + slope * wpos.astype(F32), NEG_INF)
        s_new, v_new = new_row(neww_ref)
        m = jnp.maximum(jnp.max(s, axis=-1, keepdims=True), s_new)
        e = jnp.where(ok, jnp.exp2(s - m), 0.0)
        e_new = jnp.exp2(s_new - m)
        l = jnp.sum(e, axis=-1, keepdims=True) + e_new
        ow_ref[...] = (_dot_t(e.astype(BF16), vw_t) + e_new * v_new) / l

    nk = p * PAGE_SIZE
    kt_t = jnp.concatenate([page_refs[k][:KV_WIDTH, :].astype(BF16) for k in range(p)], axis=1)
    vt_t = jnp.concatenate([page_refs[k][KV_WIDTH:, :].astype(BF16) for k in range(p)], axis=1)
    member = member_ref[0:KV_HEADS, :].astype(BF16)
    member = jnp.concatenate([member] * Q_PER_KV, axis=0)
    n_pad = member.shape[1]
    kidx = c * nk + _iota((n_pad, nk), 1)
    expand = jnp.where(kidx // SLC_BLOCK == _iota((n_pad, nk), 0), 1.0, 0.0).astype(BF16)
    mk = _dot(member, expand)
    kpos = (c * nk + _iota((1, nk), 1)).astype(F32)
    s = jnp.where(mk > 0.5, _dot(qpad, kt_t) + slope * kpos, NEG_INF)
    m_old = m_ref[...]
    m_new = jnp.maximum(m_old, jnp.max(s, axis=-1, keepdims=True))
    alpha = jnp.exp2(m_old - m_new)
    pr = jnp.exp2(s - m_new)
    l_ref[...] = alpha * l_ref[...] + jnp.sum(pr, axis=-1, keepdims=True)
    acc_ref[...] = alpha * acc_ref[...] + _dot_t(pr.astype(BF16), vt_t)
    m_ref[...] = m_new

    @pl.when(c == pl.num_programs(1) - 1)
    def _():
        new_blk = qpos // SLC_BLOCK
        is_member = jnp.concatenate([member_ref[0:KV_HEADS, new_blk:new_blk + 1]] * Q_PER_KV, axis=0) > 0.5
        s_new, v_new = new_row(news_ref)
        s_new = jnp.where(is_member, s_new, NEG_INF)
        m_old = m_ref[...]
        m_new = jnp.maximum(m_old, s_new)
        alpha = jnp.exp2(m_old - m_new)
        e_new = jnp.where(is_member, jnp.exp2(s_new - m_new), 0.0)
        l = alpha * l_ref[...] + e_new
        os = (alpha * acc_ref[...] + e_new * v_new) / l

        lane_g = _iota((1, KV_WIDTH), 1) // HEAD_DIM

        def flat(o):
            segs = []
            for r in range(Q_PER_KV):
                seg = jnp.zeros((1, KV_WIDTH), F32)
                for g in range(KV_HEADS):
                    i = r * KV_HEADS + g
                    seg = seg + jnp.where(lane_g == g, o[i:i + 1, :], 0.0)
                segs.append(seg)
            return jnp.concatenate(segs, axis=1)

        ghi, glo = _split_bf16(jnp.broadcast_to(gate_ref[...], (DROWS, GATE_PAD)))
        gx = (_dot(ghi, gexp_ref[...]) + _dot(glo, gexp_ref[...]))[0:1]
        o = (gx[:, 0:ATTN_WIDTH] * flat(oc_ref[...])
             + gx[:, ATTN_WIDTH:2 * ATTN_WIDTH] * flat(os)
             + gx[:, 2 * ATTN_WIDTH:] * flat(ow_ref[...]))
        o_ref[...] = o.astype(BF16)


def _decode_sw(page_table, q3, slopes, oc, member, gate3, gexp, win_state_t, new_s, new_w, pages_t,
               *, pages_per_step, qpos, past):
    nb, npg = page_table.shape
    p = pages_per_step
    n_buf = win_state_t.shape[2]
    n_pad = member.shape[2]
    per_b = lambda r, c: pl.BlockSpec((None, r, c), lambda b, i, pt: (b, 0, 0))
    full = lambda a: pl.BlockSpec(a.shape, lambda b, i, pt: (0,) * a.ndim)

    def page_spec(k):
        return pl.BlockSpec((None, 2 * KV_WIDTH, PAGE_SIZE), lambda b, i, pt: (pt[b, i * p + k], 0, 0))

    grid_spec = pltpu.PrefetchScalarGridSpec(
        num_scalar_prefetch=1,
        grid=(nb, npg // p),
        in_specs=[per_b(1, ATTN_WIDTH), full(slopes), per_b(DROWS, KV_WIDTH), per_b(DROWS, n_pad), per_b(1, GATE_PAD),
                  full(gexp), per_b(2 * KV_WIDTH, n_buf), per_b(1, 2 * KV_WIDTH), per_b(1, 2 * KV_WIDTH)]
                 + [page_spec(k) for k in range(p)],
        out_specs=per_b(1, ATTN_WIDTH),
        scratch_shapes=[pltpu.VMEM((DROWS, 1), F32), pltpu.VMEM((DROWS, 1), F32),
                        pltpu.VMEM((DROWS, KV_WIDTH), F32), pltpu.VMEM((DROWS, KV_WIDTH), F32)],
    )
    return pl.pallas_call(
        functools.partial(_decode_sw_kernel, pages_per_step=p, qpos=qpos, past=past, n_buf=n_buf),
        grid_spec=grid_spec,
        out_shape=jax.ShapeDtypeStruct((nb, 1, ATTN_WIDTH), BF16),
        compiler_params=_cparams(("arbitrary", "arbitrary")),
        name="decode_slc_win",
    )(page_table, q3, slopes, oc, member, gate3, gexp, win_state_t, new_s, new_w, *([pages_t] * p))


def _outproj_kernel(x_ref, oa_ref, d_ref, wpm_ref, spm_ref, woa_ref, wop_ref, g2_ref, wrh_ref, wrl_ref, br_ref,
                    x1_out, h2_out, tope_out, gate_out):
    n_grp, pm_group = wpm_ref.shape[0], wpm_ref.shape[1]
    pm = jnp.concatenate([_dot(d_ref[:, gi * pm_group:(gi + 1) * pm_group], wpm_ref[gi]) for gi in range(n_grp)], axis=1)
    pm = pm * spm_ref[...]
    x1 = x_ref[...] + _dot(oa_ref[...], woa_ref[...]) + _dot(pm.astype(BF16), wop_ref[...])
    x1_out[...] = x1
    ms = jnp.mean(x1 * x1, axis=-1, keepdims=True)
    h2 = (x1 * lax.rsqrt(ms + EPS)) * g2_ref[...]
    half = h2.shape[1] // 2
    bits = lax.bitcast_convert_type(h2.astype(BF16).astype(F32), jnp.uint32)
    h2_out[...] = (bits[:, half:] & jnp.uint32(0xFFFF0000)) | (bits[:, :half] >> 16)
    hi, lo = _split_bf16(h2)
    logits = _dot(hi, wrh_ref[...]) + _dot(lo, wrh_ref[...]) + _dot(hi, wrl_ref[...]) + br_ref[...]
    tm = logits.shape[0]
    lane = _iota((tm, LANES), 1)
    logits = jnp.where(lane < N_EXPERTS, logits, NEVER)
    vals, idxs = [], []
    for _ in range(TOP_K):
        m = jnp.max(logits, axis=-1, keepdims=True)
        idx = jnp.min(jnp.where(logits == m, lane, LANES), axis=-1, keepdims=True)
        vals.append(m)
        idxs.append(idx)
        logits = jnp.where(lane == idx, NEVER, logits)
    es = [jnp.exp(v - vals[0]) for v in vals]
    den = es[0]
    for e in es[1:]:
        den = den + e
    tope = jnp.full((tm, LANES), -1, I32)
    gts = jnp.zeros((tm, LANES), F32)
    for k in range(TOP_K):
        tope = jnp.where(lane == k, idxs[k], tope)
        gts = jnp.where(lane == k, es[k] / den, gts)
    tope_out[...] = tope
    gate_out[...] = gts


def _outproj(x2d, o_attn, dpool, wpm, spm, woa, wop, g2, wrh, wrl, br, *, tm):
    n, d_model = x2d.shape
    row = lambda w: pl.BlockSpec((tm, w), lambda i: (i, 0))
    full = lambda a: pl.BlockSpec(a.shape, lambda i: (0,) * a.ndim)
    consts = [wpm, spm, woa, wop, g2, wrh, wrl, br]
    return pl.pallas_call(
        _outproj_kernel,
        grid=(n // tm,),
        in_specs=[row(d_model), row(o_attn.shape[1]), row(dpool.shape[1])] + [full(a) for a in consts],
        out_specs=[row(d_model), row(d_model // 2), row(LANES), row(LANES)],
        out_shape=[jax.ShapeDtypeStruct((n, d_model), F32), jax.ShapeDtypeStruct((n, d_model // 2), jnp.uint32),
                   jax.ShapeDtypeStruct((n, LANES), I32), jax.ShapeDtypeStruct((n, LANES), F32)],
        compiler_params=_cparams(("arbitrary",)),
        name="outproj",
    )(x2d, o_attn, dpool, *consts)


MOE_SUB = 256
MOE_ROWS = 5 * MOE_SUB
MOE_FC = 256
ROUTE_TILE = 512


def _route_kernel(e_ref, rank_out, cnt_out, carry_ref):
    @pl.when(pl.program_id(0) == 0)
    def _():
        carry_ref[...] = jnp.zeros(carry_ref.shape, F32)

    tr = e_ref.shape[0]
    lane = _iota((tr, LANES), 1)
    e = e_ref[...]
    ohs = [jnp.where(e[:, k:k + 1] == lane, 1.0, 0.0) for k in range(TOP_K)]
    tot = ohs[0]
    for oh in ohs[1:]:
        tot = tot + oh
    lower = jnp.where(_iota((tr, tr), 1) < _iota((tr, tr), 0), 1.0, 0.0).astype(BF16)
    before = _dot(lower, tot.astype(BF16)) + carry_ref[...]
    rank = jnp.zeros((tr, LANES), I32)
    for k in range(TOP_K):
        rk = jnp.sum(ohs[k] * before, axis=-1, keepdims=True).astype(I32)
        rank = jnp.where(lane == k, rk, rank)
    rank_out[...] = rank
    carry_ref[...] = carry_ref[...] + jnp.sum(tot, axis=0, keepdims=True)
    cnt_out[...] = carry_ref[...]


def _route(tope):
    n = tope.shape[0]
    return pl.pallas_call(
        _route_kernel,
        grid=(n // ROUTE_TILE,),
        in_specs=[pl.BlockSpec((ROUTE_TILE, LANES), lambda i: (i, 0))],
        out_specs=[pl.BlockSpec((ROUTE_TILE, LANES), lambda i: (i, 0)), pl.BlockSpec((1, LANES), lambda i: (0, 0))],
        out_shape=[jax.ShapeDtypeStruct((n, LANES), I32), jax.ShapeDtypeStruct((1, LANES), F32)],
        scratch_shapes=[pltpu.VMEM((1, LANES), F32)],
        compiler_params=_cparams(("arbitrary",)),
        name="moe_route",
    )(tope)


def _dispatch_kernel(slot_ref, h_ref, *refs):
    xbuf, sem = refs[-2], refs[-1]
    tr = h_ref.shape[0]

    def row_copy(i, s):
        return pltpu.make_async_copy(h_ref.at[pl.ds(i, 1)], xbuf.at[pl.ds(s, 1)], sem)

    def issue(i, carry):
        for k in range(TOP_K):
            row_copy(i, slot_ref[i * TOP_K + k]).start()
        return carry

    lax.fori_loop(0, tr, issue, 0)
    n_rows = tr * TOP_K
    pltpu.make_async_copy(xbuf.at[pl.ds(0, n_rows)], xbuf.at[pl.ds(0, n_rows)], sem).wait()


def _dispatch(slots_flat, h2, xbuf, n_rows, *, tr):
    n, d = h2.shape
    in_specs = [pl.BlockSpec((tr * TOP_K,), lambda i: (i,), memory_space=pltpu.SMEM),
                pl.BlockSpec((tr, d), lambda i: (i, 0))]
    args = [slots_flat, h2]
    aliases = {}
    if xbuf is not None:
        in_specs.append(pl.BlockSpec(memory_space=pl.ANY))
        args.append(xbuf)
        aliases = {2: 0}
    return pl.pallas_call(
        _dispatch_kernel,
        grid=(n // tr,),
        in_specs=in_specs,
        out_specs=pl.BlockSpec(memory_space=pl.ANY),
        out_shape=jax.ShapeDtypeStruct((n_rows, d), h2.dtype),
        scratch_shapes=[pltpu.SemaphoreType.DMA(())],
        input_output_aliases=aliases,
        compiler_params=pltpu.CompilerParams(dimension_semantics=("arbitrary",), vmem_limit_bytes=VMEM_LIMIT,
                                             has_side_effects=True),
        name="moe_dispatch",
    )(*args)


def _experts_kernel(we_ref, wr_ref, wb_ref, x_ref, wg_ref, wu_ref, bg_ref, bu_ref, wd_ref, bd_ref, o_ref, xb_ref):
    del we_ref, wb_ref
    w = pl.program_id(0)
    c = pl.program_id(1)
    rows = wr_ref[w]

    half = x_ref.shape[1]

    @pl.when(rows > 0)
    def _():
        @pl.when(c == 0)
        def _():
            valid = _iota((MOE_ROWS, 1), 0) < rows
            x = x_ref[...]
            lo = lax.bitcast_convert_type(x << 16, F32)
            hi = lax.bitcast_convert_type(x & jnp.uint32(0xFFFF0000), F32)
            xb_ref[:, :half] = jnp.where(valid, lo, 0.0).astype(BF16)
            xb_ref[:, half:] = jnp.where(valid, hi, 0.0).astype(BF16)
            o_ref[...] = jnp.broadcast_to(bd_ref[...], o_ref.shape)

        def sub_tile(i, wg, wu, wd):
            rs = slice(i * MOE_SUB, (i + 1) * MOE_SUB)
            xs = xb_ref[rs, :]
            g = _dot(xs, wg) + bg_ref[...]
            u = _dot(xs, wu) + bu_ref[...]
            gh = jnp.minimum(g, SWIGLU_LIMIT)
            up = jnp.clip(u, -SWIGLU_LIMIT, SWIGLU_LIMIT)
            act = (up + 1.0) * gh * jax.nn.sigmoid(SWIGLU_ALPHA * gh)
            o_ref[rs, :] = o_ref[rs, :] + _dot(act.astype(BF16), wd)

        n_sub = (rows + MOE_SUB - 1) // MOE_SUB
        for n in range(1, MOE_ROWS // MOE_SUB + 1):
            @pl.when(n_sub == n)
            def _():
                ws = (wg_ref[...].astype(BF16), wu_ref[...].astype(BF16), wd_ref[...].astype(BF16))
                for i in range(n):
                    sub_tile(i, *ws)


def _experts(work_e, work_rows, work_blk, xbuf, w_gu, b_gu3, w_down, b_down3):
    n_work = work_e.shape[0]
    n_exp, d_model, two_ff = w_gu.shape
    d_ff = two_ff // 2
    nc = d_ff // MOE_FC

    def cidx(w, c, wr):
        return jnp.where(wr[w] > 0, c, nc - 1)

    grid_spec = pltpu.PrefetchScalarGridSpec(
        num_scalar_prefetch=3,
        grid=(n_work, nc),
        in_specs=[
            pl.BlockSpec((MOE_ROWS, d_model // 2), lambda w, c, we, wr, wb: (wb[w], 0)),
            pl.BlockSpec((None, d_model, MOE_FC), lambda w, c, we, wr, wb: (we[w], 0, cidx(w, c, wr))),
            pl.BlockSpec((None, d_model, MOE_FC), lambda w, c, we, wr, wb: (we[w], 0, nc + cidx(w, c, wr))),
            pl.BlockSpec((None, 1, MOE_FC), lambda w, c, we, wr, wb: (we[w], 0, cidx(w, c, wr))),
            pl.BlockSpec((None, 1, MOE_FC), lambda w, c, we, wr, wb: (we[w], 0, nc + cidx(w, c, wr))),
            pl.BlockSpec((None, MOE_FC, d_model), lambda w, c, we, wr, wb: (we[w], cidx(w, c, wr), 0)),
            pl.BlockSpec((None, 1, d_model), lambda w, c, we, wr, wb: (we[w], 0, 0)),
        ],
        out_specs=pl.BlockSpec((MOE_ROWS, d_model), lambda w, c, we, wr, wb: (wb[w], 0)),
        scratch_shapes=[pltpu.VMEM((MOE_ROWS, d_model), BF16)],
    )
    return pl.pallas_call(
        _experts_kernel,
        grid_spec=grid_spec,
        out_shape=jax.ShapeDtypeStruct((xbuf.shape[0], d_model), F32),
        compiler_params=_cparams(("arbitrary", "arbitrary")),
        name="moe_experts",
    )(work_e, work_rows, work_blk, xbuf, w_gu, w_gu, b_gu3, b_gu3, w_down, b_down3)


def _combine_kernel(slot_ref, x1_ref, gate_ref, ybuf, o_ref, rows_ref, sem):
    tc = x1_ref.shape[0]

    def row_copy(i, k):
        return pltpu.make_async_copy(ybuf.at[pl.ds(slot_ref[i * TOP_K + k], 1)], rows_ref.at[k, pl.ds(i, 1)], sem)

    def issue(i, carry):
        for k in range(TOP_K):
            row_copy(i, k).start()
        return carry

    lax.fori_loop(0, tc, issue, 0)
    pltpu.make_async_copy(rows_ref, rows_ref, sem).wait()
    gates = gate_ref[...]
    y = x1_ref[...]
    for k in range(TOP_K):
        y = y + gates[:, k:k + 1] * rows_ref[k]
    o_ref[...] = y


def _combine(slots_flat, x1, gates, ybuf, *, tc):
    n, d = x1.shape
    return pl.pallas_call(
        _combine_kernel,
        grid=(n // tc,),
        in_specs=[pl.BlockSpec((tc * TOP_K,), lambda i: (i,), memory_space=pltpu.SMEM),
                  pl.BlockSpec((tc, d), lambda i: (i, 0)),
                  pl.BlockSpec((tc, LANES), lambda i: (i, 0)),
                  pl.BlockSpec(memory_space=pl.ANY)],
        out_specs=pl.BlockSpec((tc, d), lambda i: (i, 0)),
        out_shape=jax.ShapeDtypeStruct((n, d), F32),
        scratch_shapes=[pltpu.VMEM((TOP_K, tc, d), F32), pltpu.SemaphoreType.DMA(())],
        compiler_params=_cparams(("arbitrary",)),
        name="moe_combine",
    )(slots_flat, x1, gates, ybuf)


def _moe(h2_p, h2_s, tope_p, tope_s, gate_p, gate_s, x1_p, x1_s, w_gu, b_gu, w_down, b_down):
    n_p, n_s = h2_p.shape[0], h2_s.shape[0]
    n_exp = w_gu.shape[0]
    pad = (-(n_p + n_s)) % ROUTE_TILE
    tope_all = jnp.concatenate([tope_p, tope_s, jnp.full((pad, LANES), -1, I32)], axis=0)
    rank, counts = _route(tope_all)
    counts = counts[0, :n_exp].astype(I32)
    n_assign = (n_p + n_s) * TOP_K
    n_work = n_assign // MOE_ROWS + n_exp
    items = (counts + MOE_ROWS - 1) // MOE_ROWS
    per = -(-counts // jnp.maximum(items, 1))
    per = jnp.maximum(-(-per // MOE_SUB) * MOE_SUB, MOE_SUB)
    item_end = jnp.cumsum(items)
    item_start = item_end - items
    n_used = item_end[-1]
    w_ids = jnp.arange(n_work, dtype=I32)
    used = w_ids < n_used
    w_eff = jnp.where(used, w_ids, n_used - 1)
    work_e = jnp.minimum(jnp.searchsorted(item_end, w_eff, side="right"), n_exp - 1).astype(I32)
    work_rows = jnp.clip(counts[work_e] - (w_eff - item_start[work_e]) * per[work_e], 0, per[work_e])
    work_rows = jnp.where(used, work_rows, 0).astype(I32)
    e_all = tope_all[:n_p + n_s, :TOP_K]
    rank_all = rank[:n_p + n_s, :TOP_K]
    item_of = rank_all // per[e_all]
    slots = ((item_start[e_all] + item_of) * MOE_ROWS + rank_all - item_of * per[e_all]).astype(I32).reshape(-1)
    slots_p, slots_s = slots[:n_p * TOP_K], slots[n_p * TOP_K:]
    n_rows = n_work * MOE_ROWS
    xbuf = _dispatch(slots_p, h2_p, None, n_rows, tr=512)
    xbuf = _dispatch(slots_s, h2_s, xbuf, n_rows, tr=n_s)
    ybuf = _experts(work_e, work_rows, w_eff.astype(I32), xbuf, w_gu, b_gu[:, None, :], w_down, b_down[:, None, :])
    y_p = _combine(slots_p, x1_p, gate_p, ybuf, tc=128)
    y_s = _combine(slots_s, x1_s, gate_s, ybuf, tc=n_s)
    return y_p, y_s


def _head_perm():
    return [g * Q_PER_KV + r for r in range(Q_PER_KV) for g in range(KV_HEADS)]


def _pack_w_in(w_in):
    d_model = w_in.shape[0]
    pm_width = d_model - ATTN_WIDTH
    s0 = ATTN_WIDTH
    s1 = s0 + 2 * KV_WIDTH
    s2 = s1 + 2 * KV_WIDTH
    s3 = s2 + 2 * KV_WIDTH
    s4 = s3 + 3 * N_HEADS
    wq = w_in[:, :s0].reshape(d_model, N_HEADS, HEAD_DIM)[:, jnp.array(_head_perm())].reshape(d_model, ATTN_WIDTH)
    wg = w_in[:, s3:s4].reshape(d_model, N_HEADS, 3).transpose(0, 2, 1).reshape(d_model, 3 * N_HEADS)
    wg = jnp.pad(wg, ((0, 0), (0, GATE_PAD - 3 * N_HEADS)))
    return jnp.concatenate([wq, w_in[:, s0:s3], w_in[:, s4:s4 + pm_width], wg], axis=1).astype(BF16)


def _cmp_weights(w_cmp_k, b_cmp_k, w_cmp_v, b_cmp_v):
    def half(o):
        wk = jnp.tile(w_cmp_k[o * CMP_STRIDE:(o + 1) * CMP_STRIDE], (1, KV_HEADS))
        wv = jnp.tile(w_cmp_v[o * CMP_STRIDE:(o + 1) * CMP_STRIDE], (1, KV_HEADS))
        return jnp.concatenate([wk, wv], axis=1)
    bias = jnp.concatenate([jnp.tile(b_cmp_k, KV_HEADS), jnp.tile(b_cmp_v, KV_HEADS)])[None, :]
    return half(0), half(1), bias


def _cmp_to_slc_t(n_cmp_pad, n_cmp, n_slc, n_slc_pad):
    i0 = jnp.arange(n_cmp_pad)[None, :] * CMP_STRIDE
    j0 = jnp.arange(n_slc_pad)[:, None] * SLC_BLOCK
    shared = jnp.minimum(i0 + CMP_LEN, j0 + SLC_BLOCK) - jnp.maximum(i0, j0)
    frac = jnp.clip(shared, 0, None).astype(F32) / CMP_LEN
    ok = (jnp.arange(n_cmp_pad)[None, :] < n_cmp) & (jnp.arange(n_slc_pad)[:, None] < n_slc)
    return jnp.where(ok, frac, 0.0).astype(BF16)


def _block_expand(n_blk_pad, n_keys):
    return (jnp.arange(n_blk_pad)[:, None] == (jnp.arange(n_keys)[None, :] // SLC_BLOCK)).astype(BF16)


def _prompt_mixer(x_prompt, p):
    batch, seq, d_model = x_prompt.shape
    n = batch * seq
    q, kvc, kvs, kvw, gate, u, dpool, ks_b, vs_t, kw_b, vw_t = _inproj(
        x_prompt.reshape(n, d_model), p["g1"], p["w_in"], p["qg"], p["ksg"], p["kwg"], tm=256, seq_len=seq)
    npg = seq // PAGE_SIZE
    pt = (jnp.arange(batch, dtype=I32)[:, None] * npg + jnp.arange(npg, dtype=I32)[None, :])
    kc, vc = _compress(pt, kvc.reshape(batch * npg, PAGE_SIZE, 2 * KV_WIDTH), p["cw0"], p["cw1"], p["cb"], p["kcg"],
                       pages_per_step=8)
    n_cmp = seq // CMP_STRIDE - 1
    n_slc = -(-seq // SLC_BLOCK)
    ct = _cmp_to_slc_t(kc.shape[1], n_cmp, n_slc, LANES)
    ext = _block_expand(LANES, seq).T

    o_attn = _prompt_attn(q, gate, kc, jnp.swapaxes(vc, 1, 2), ks_b, vs_t, kw_b, vw_t, ct, ext, batch=batch, seq=seq)
    return o_attn, dpool, kvc, kvs, kvw, u


def _sample_mixer(x_sample, cache_cmp, cache_slc, state_win, state_pool, page_table, p):
    nb, t, d_model = x_sample.shape
    assert t == 1, "decode path handles one new row per sequence"
    npg = page_table.shape[1]
    past = npg * PAGE_SIZE
    qpos = past
    q, kvc, kvs, kvw, gate, u, dpool = _inproj(
        x_sample.reshape(nb, d_model), p["g1"], p["w_in"], p["qg"], p["ksg"], p["kwg"], tm=nb,
        pool_state=state_pool, hist_pos=qpos)
    def rows_on_lanes(a):
        return jnp.transpose(a, (0, 2, 3, 4, 1)).reshape(a.shape[0], 2 * KV_WIDTH, a.shape[1])

    cw0_t = jnp.tile(p["cw0"].T, (1, PAGE_SIZE // CMP_STRIDE))
    cw1_t = jnp.tile(p["cw1"].T, (1, PAGE_SIZE // CMP_STRIDE))
    a0, a1 = _compress_t(page_table, rows_on_lanes(cache_cmp), cw0_t, cw1_t, pages_per_step=LANES // CMP_SUB_PER_PAGE)
    n_cmp = (past + t) // CMP_STRIDE - 1
    n_slc = -(-(past + t) // SLC_BLOCK)
    n_pad = -(-n_slc // LANES) * LANES
    cmat = _cmp_to_slc_t(a0.shape[2], n_cmp, n_slc, n_pad).T
    slopes = jnp.array([SLOPES[g * Q_PER_KV + r] * LOG2E for r in range(Q_PER_KV) for g in range(KV_HEADS)], F32)[:, None]
    q3 = q.reshape(nb, 1, ATTN_WIDTH)
    oc, member = _decode_cmp(q3, slopes, a0, a1, p["cb"].T, p["kcg"][:, :HEAD_DIM].T, cmat, qpos=qpos, n_slc=n_slc)
    rows = jnp.arange(3 * N_HEADS)
    k_i, g_i, r_i = rows // N_HEADS, (rows % N_HEADS) // Q_PER_KV, rows % Q_PER_KV
    col_head = k_i * N_HEADS + r_i * KV_HEADS + g_i
    gexp = (jnp.arange(3 * ATTN_WIDTH)[None, :] // HEAD_DIM == col_head[:, None])
    gexp = jnp.pad(gexp, ((0, GATE_PAD - 3 * N_HEADS), (0, 0))).astype(BF16)
    o = _decode_sw(page_table, q3, slopes, oc, member, gate.reshape(nb, 1, GATE_PAD), gexp,
                   rows_on_lanes(state_win), kvs.reshape(nb, 1, 2 * KV_WIDTH), kvw.reshape(nb, 1, 2 * KV_WIDTH),
                   rows_on_lanes(cache_slc), pages_per_step=16, qpos=qpos, past=past)
    return o.reshape(nb, ATTN_WIDTH), dpool, kvc, kvs, kvw, u


def _prep_params(norm1_g, w_in, q_gain, k_cmp_gain, k_slc_gain, k_win_gain, w_cmp_k, b_cmp_k, w_cmp_v, b_cmp_v):
    cw0, cw1, cb = _cmp_weights(w_cmp_k, b_cmp_k, w_cmp_v, b_cmp_v)
    return {
        "g1": norm1_g[None, :],
        "w_in": _pack_w_in(w_in),
        "qg": jnp.tile(q_gain, N_HEADS)[None, :],
        "ksg": jnp.tile(k_slc_gain, KV_HEADS)[None, :],
        "kwg": jnp.tile(k_win_gain, KV_HEADS)[None, :],
        "kcg": jnp.tile(k_cmp_gain, KV_HEADS)[None, :],
        "cw0": cw0, "cw1": cw1, "cb": cb,
    }


def _prep_out_params(w_pm, s_pm, w_out, norm2_g, w_router, b_router):
    d_model = w_out.shape[0]
    woa = w_out[:ATTN_WIDTH].reshape(N_HEADS, HEAD_DIM, d_model)[jnp.array(_head_perm())].reshape(ATTN_WIDTH, d_model)
    wr = jnp.pad(w_router, ((0, 0), (0, LANES - w_router.shape[1])))
    wrh, wrl = _split_bf16(wr)
    return (w_pm.astype(BF16), s_pm.reshape(1, -1), woa.astype(BF16), w_out[ATTN_WIDTH:].astype(BF16),
            norm2_g[None, :], wrh, wrl, jnp.pad(b_router, (0, LANES - b_router.shape[0]))[None, :])


def _layer(x_prompt, x_sample, cache_cmp, cache_slc, state_win, state_pool, page_table,
           norm1_g, w_in, q_gain, k_cmp_gain, k_slc_gain, k_win_gain, w_cmp_k, b_cmp_k, w_cmp_v, b_cmp_v,
           w_pm, s_pm, w_out, norm2_g, w_router, b_router, w_gu, b_gu, w_down, b_down):
    batch, seq, d_model = x_prompt.shape
    nb, t = x_sample.shape[:2]
    p = _prep_params(norm1_g, w_in, q_gain, k_cmp_gain, k_slc_gain, k_win_gain, w_cmp_k, b_cmp_k, w_cmp_v, b_cmp_v)
    oa_p, d_p, kvc_p, kvs_p, kvw_p, u_p = _prompt_mixer(x_prompt, p)
    oa_s, d_s, kvc_s, kvs_s, kvw_s, u_s = _sample_mixer(x_sample, cache_cmp, cache_slc, state_win, state_pool, page_table, p)
    op = _prep_out_params(w_pm, s_pm, w_out, norm2_g, w_router, b_router)
    x1_p, h2_p, te_p, gt_p = _outproj(x_prompt.reshape(batch * seq, d_model), oa_p, d_p, *op, tm=256)
    x1_s, h2_s, te_s, gt_s = _outproj(x_sample.reshape(nb * t, d_model), oa_s, d_s, *op, tm=nb * t)
    y_p, y_s = _moe(h2_p, h2_s, te_p, te_s, gt_p, gt_s, x1_p, x1_s, w_gu, b_gu, w_down, b_down)
    kv_shape = (2, KV_HEADS, HEAD_DIM)
    n_win = min(WINDOW, seq)
    st_p = (kvc_p.reshape(batch, seq, *kv_shape), kvs_p.reshape(batch, seq, *kv_shape),
            kvw_p.reshape(batch, seq, *kv_shape)[:, seq - n_win:], u_p.reshape(batch, seq, -1)[:, seq - POOL_HIST:])
    kvw_s5 = kvw_s.reshape(nb, t, *kv_shape)
    u_s3 = u_s.reshape(nb, t, -1)
    st_s = (kvc_s.reshape(nb, t, *kv_shape), kvs_s.reshape(nb, t, *kv_shape),
            jnp.concatenate([state_win, kvw_s5], axis=1)[:, t:], jnp.concatenate([state_pool, u_s3], axis=1)[:, t:])
    return y_p.reshape(batch, seq, d_model), y_s.reshape(nb, t, d_model), st_p, st_s


def kernel(x_prompt, x_sample, cache_cmp_kv, cache_slc_kv, state_win_kv, state_pool, page_table, norm1_g, w_in, q_gain, k_cmp_gain, k_slc_gain, k_win_gain, w_cmp_k, b_cmp_k, w_cmp_v, b_cmp_v, w_pm, s_pm, w_out, norm2_g, w_router, b_router, w_gu, b_gu, w_down, b_down):
    layer_params = (norm1_g, w_in, q_gain, k_cmp_gain, k_slc_gain, k_win_gain, w_cmp_k, b_cmp_k, w_cmp_v, b_cmp_v,
                    w_pm, s_pm, w_out, norm2_g, w_router, b_router, w_gu, b_gu, w_down, b_down)
    y_p, y_s = x_prompt, x_sample
    p_states, s_states = [], []
    for layer in range(norm1_g.shape[0]):
        lw = [w[layer] for w in layer_params]
        y_p, y_s, st_p, st_s = _layer(y_p, y_s, cache_cmp_kv[layer], cache_slc_kv[layer], state_win_kv[layer],
                                      state_pool[layer], page_table, *lw)
        p_states.append(st_p)
        s_states.append(st_s)
    stack = lambda states, i: jnp.stack([s[i] for s in states], axis=0)
    return (y_p, y_s,
            stack(p_states, 0), stack(p_states, 1), stack(p_states, 2), stack(p_states, 3),
            stack(s_states, 0), stack(s_states, 1), stack(s_states, 2), stack(s_states, 3))
```

```python
import functools
import math

import jax
import jax.numpy as jnp
from jax import lax
from jax.experimental import pallas as pl
from jax.experimental.pallas import tpu as pltpu

F32 = jnp.float32
BF16 = jnp.bfloat16
I32 = jnp.int32

N_HEADS = 16
HEAD_DIM = 64
KV_HEADS = 4
Q_PER_KV = N_HEADS // KV_HEADS
ATTN_WIDTH = N_HEADS * HEAD_DIM
KV_WIDTH = KV_HEADS * HEAD_DIM
CMP_LEN = 32
CMP_STRIDE = 16
SLC_BLOCK = 64
N_SELECT = 16
WINDOW = 512
FORCE_SCORE = 1e4
NEG_INF = -1e30
POOL_WINDOWS = (2, 4, 8, 16)
MAX_POOL_W = max(POOL_WINDOWS)
POOL_HIST = MAX_POOL_W - 1
N_EXPERTS = 32
TOP_K = 4
SWIGLU_LIMIT = 7.0
SWIGLU_ALPHA = 1.702
EPS = 1e-6
PAGE_SIZE = 128

LANES = 128
VMEM_LIMIT = 56 * 1024 * 1024

GATE_PAD = LANES
SLOPES = [2.0 ** (-8.0 * (h + 1) / N_HEADS) for h in range(N_HEADS)]
LOG2E = math.log2(math.e)


def _cparams(sem):
    return pltpu.CompilerParams(dimension_semantics=sem, vmem_limit_bytes=VMEM_LIMIT)


def _iota(shape, dim):
    return lax.broadcasted_iota(I32, shape, dim)


def _split_bf16(x):
    hi = x.astype(BF16)
    lo = (x - hi.astype(F32)).astype(BF16)
    return hi, lo


def _dot(a, b):
    return jnp.dot(a, b, preferred_element_type=F32)


def _dot_t(a, b):
    return lax.dot_general(a, b, (((1,), (1,)), ((), ())), preferred_element_type=F32)


def _head_mean_sq(z):
    m, w = z.shape
    ones_bd = jnp.where(_iota((256, 256), 0) // HEAD_DIM == _iota((256, 256), 1) // HEAD_DIM, 1.0, 0.0).astype(BF16)
    zz = z * z
    hi, lo = _split_bf16(zz)
    parts = []
    for c in range(w // 256):
        sl = slice(c * 256, (c + 1) * 256)
        parts.append(_dot(hi[:, sl], ones_bd) + _dot(lo[:, sl], ones_bd))
    ss = parts[0] if len(parts) == 1 else jnp.concatenate(parts, axis=1)
    return ss * (1.0 / HEAD_DIM)


C_Q = 0
C_KVC = ATTN_WIDTH
C_KVS = C_KVC + 2 * KV_WIDTH
C_KVW = C_KVS + 2 * KV_WIDTH
C_U = C_KVW + 2 * KV_WIDTH


def _inproj_kernel(x_ref, g1_ref, w_ref, qg_ref, ksg_ref, kwg_ref, *refs, tm, pm_width, seq_tiles, hist_pos):
    if seq_tiles is None:
        sp_ref, refs = refs[0], refs[1:]
    q_out, kvc_out, kvs_out, kvw_out, gate_out, u_out, d_out = refs[:7]
    x = x_ref[...]
    ms = jnp.mean(x * x, axis=-1, keepdims=True)
    h = (x * lax.rsqrt(ms + EPS)) * g1_ref[...]
    hb = h.astype(BF16)
    c_gate = C_U + pm_width

    zq = _dot(hb, w_ref[:, C_Q:C_KVC])
    qn = (zq * lax.rsqrt(_head_mean_sq(zq) + EPS)) * qg_ref[...]
    q_out[...] = (qn * (HEAD_DIM ** -0.5 * LOG2E)).astype(BF16)

    kvc_out[...] = _dot(hb, w_ref[:, C_KVC:C_KVS])

    zs = _dot(hb, w_ref[:, C_KVS:C_KVW])
    ks = zs[:, :KV_WIDTH]
    ks = (ks * lax.rsqrt(_head_mean_sq(ks) + EPS)) * ksg_ref[...]
    kvs = jnp.concatenate([ks, zs[:, KV_WIDTH:]], axis=1)
    kvs_out[...] = kvs

    zw = _dot(hb, w_ref[:, C_KVW:C_U])
    kw = zw[:, :KV_WIDTH]
    kw = (kw * lax.rsqrt(_head_mean_sq(kw) + EPS)) * kwg_ref[...]
    kvw = jnp.concatenate([kw, zw[:, KV_WIDTH:]], axis=1)
    kvw_out[...] = kvw
    if seq_tiles is not None:
        ksb_out, vst_out, kwb_out, vwt_out = refs[7:11]
        ksb_out[...] = ks.astype(BF16)
        vst_out[...] = zs[:, KV_WIDTH:].T.astype(BF16)
        kwb_out[...] = kw.astype(BF16)
        vw_t = zw[:, KV_WIDTH:].T.astype(BF16)
        for j in range(tm // TKW):
            vwt_out[j] = vw_t[:, j * TKW:(j + 1) * TKW]

    gate_out[...] = jax.nn.sigmoid(_dot(hb, w_ref[:, c_gate:c_gate + GATE_PAD]))

    u = _dot(hb, w_ref[:, C_U:c_gate])
    u_out[...] = u

    pm_group = pm_width // len(POOL_WINDOWS)
    if seq_tiles is None:
        tpos = float(hist_pos + 1)
        for gi, w in enumerate(POOL_WINDOWS):
            cs = slice(gi * pm_group, (gi + 1) * pm_group)
            s = u[:, cs]
            for k in range(1, w):
                s = s + sp_ref[:, POOL_HIST - k, cs]
            d_out[:, cs] = (s / min(float(w), tpos) - u[:, cs]).astype(BF16)
    else:
        ext_ref = refs[11]
        j = pl.program_id(0) % seq_tiles

        @pl.when(j == 0)
        def _():
            ext_ref[0:MAX_POOL_W, :] = jnp.zeros((MAX_POOL_W, pm_width), F32)

        ext_ref[MAX_POOL_W:MAX_POOL_W + tm, :] = u
        tpos = (j * tm + _iota((tm, 1), 0) + 1).astype(F32)
        for gi, w in enumerate(POOL_WINDOWS):
            cs = slice(gi * pm_group, (gi + 1) * pm_group)
            s = ext_ref[MAX_POOL_W:MAX_POOL_W + tm, cs]
            for k in range(1, w):
                s = s + ext_ref[MAX_POOL_W - k:MAX_POOL_W - k + tm, cs]
            cnt = jnp.minimum(float(w), tpos)
            d_out[:, cs] = (s / cnt - u[:, cs]).astype(BF16)
        ext_ref[0:MAX_POOL_W, :] = ext_ref[tm:tm + MAX_POOL_W, :]


def _inproj(x2d, g1, w_packed, qg, ksg, kwg, *, tm, seq_len=None, pool_state=None, hist_pos=None):
    n, d_model = x2d.shape
    pm_width = d_model - ATTN_WIDTH
    seq_tiles = None if seq_len is None else seq_len // tm
    row = lambda w: pl.BlockSpec((tm, w), lambda i: (i, 0))
    full = lambda a: pl.BlockSpec(a.shape, lambda i: (0,) * a.ndim)
    out_shape = [
        jax.ShapeDtypeStruct((n, ATTN_WIDTH), BF16),
        jax.ShapeDtypeStruct((n, 2 * KV_WIDTH), F32),
        jax.ShapeDtypeStruct((n, 2 * KV_WIDTH), F32),
        jax.ShapeDtypeStruct((n, 2 * KV_WIDTH), F32),
        jax.ShapeDtypeStruct((n, GATE_PAD), F32),
        jax.ShapeDtypeStruct((n, pm_width), F32),
        jax.ShapeDtypeStruct((n, pm_width), BF16),
    ]
    out_specs = [row(ATTN_WIDTH), row(2 * KV_WIDTH), row(2 * KV_WIDTH), row(2 * KV_WIDTH),
                 row(GATE_PAD), row(pm_width), row(pm_width)]
    in_specs = [row(d_model), full(g1), full(w_packed), full(qg), full(ksg), full(kwg)]
    args = [x2d, g1, w_packed, qg, ksg, kwg]
    scratch = []
    if seq_tiles is None:
        in_specs.append(pl.BlockSpec((tm, POOL_HIST, pm_width), lambda i: (i, 0, 0)))
        args.append(pool_state)
    else:
        per_tile = TK // tm
        out_shape += [jax.ShapeDtypeStruct((n, KV_WIDTH), BF16), jax.ShapeDtypeStruct((n // TK, KV_WIDTH, TK), BF16),
                      jax.ShapeDtypeStruct((n, KV_WIDTH), BF16), jax.ShapeDtypeStruct((n // TKW, KV_WIDTH, TKW), BF16)]
        out_specs += [row(KV_WIDTH), pl.BlockSpec((None, KV_WIDTH, tm), lambda i: (i // per_tile, 0, i % per_tile)),
                      row(KV_WIDTH), pl.BlockSpec((tm // TKW, KV_WIDTH, TKW), lambda i: (i, 0, 0))]
        scratch.append(pltpu.VMEM((tm + MAX_POOL_W, pm_width), F32))
    return pl.pallas_call(
        functools.partial(_inproj_kernel, tm=tm, pm_width=pm_width, seq_tiles=seq_tiles, hist_pos=hist_pos),
        grid=(n // tm,),
        in_specs=in_specs,
        out_specs=out_specs,
        out_shape=out_shape,
        scratch_shapes=scratch,
        compiler_params=_cparams(("arbitrary",)),
        name="inproj",
    )(*args)


CMP_SUB_PER_PAGE = PAGE_SIZE // CMP_STRIDE


def _compress_kernel(pt_ref, *refs, pages_per_step):
    del pt_ref
    p = pages_per_step
    page_refs = refs[:p]
    halo_ref, w0_ref, w1_ref, b_ref, kg_ref, kc_out, vc_out, xs_ref = refs[p:]
    n_lt = 2 * KV_WIDTH // LANES
    nb = p * CMP_SUB_PER_PAGE
    parts = []
    for c in range(n_lt):
        cs = slice(c * LANES, (c + 1) * LANES)
        for k in range(p):
            xs_ref[c, k * PAGE_SIZE:(k + 1) * PAGE_SIZE, :] = page_refs[k][:, cs]
        xs_ref[c, p * PAGE_SIZE:p * PAGE_SIZE + CMP_STRIDE, :] = halo_ref[:, cs]
        acc = jnp.zeros((nb, LANES), F32) + b_ref[:, cs]
        for j in range(CMP_STRIDE):
            acc = acc + xs_ref[c, pl.ds(j, nb, stride=CMP_STRIDE), :] * w0_ref[j:j + 1, cs]
            acc = acc + xs_ref[c, pl.ds(CMP_STRIDE + j, nb, stride=CMP_STRIDE), :] * w1_ref[j:j + 1, cs]
        parts.append(acc)
    acc = jnp.concatenate(parts, axis=1)
    kc = acc[:, :KV_WIDTH]
    kc = (kc * lax.rsqrt(_head_mean_sq(kc) + EPS)) * kg_ref[...]
    kc_out[...] = kc.astype(BF16)
    vc_out[...] = acc[:, KV_WIDTH:].astype(BF16)


def _compress(page_table, pages, w0t, w1t, bias, kgain, *, pages_per_step):
    nb, npg = page_table.shape
    p = pages_per_step
    steps = npg // p

    def page_spec(k):
        return pl.BlockSpec((None, PAGE_SIZE, 2 * KV_WIDTH), lambda b, i, pt: (pt[b, i * p + k], 0, 0))

    halo_spec = pl.BlockSpec((None, CMP_STRIDE, 2 * KV_WIDTH),
                             lambda b, i, pt: (pt[b, jnp.minimum(i * p + p, npg - 1)], 0, 0))
    full = lambda a: pl.BlockSpec(a.shape, lambda b, i, pt: (0,) * a.ndim)
    out_spec = pl.BlockSpec((None, p * CMP_SUB_PER_PAGE, KV_WIDTH), lambda b, i, pt: (b, i, 0))
    grid_spec = pltpu.PrefetchScalarGridSpec(
        num_scalar_prefetch=1,
        grid=(nb, steps),
        in_specs=[page_spec(k) for k in range(p)] + [halo_spec, full(w0t), full(w1t), full(bias), full(kgain)],
        out_specs=[out_spec, out_spec],
        scratch_shapes=[pltpu.VMEM((2 * KV_WIDTH // LANES, p * PAGE_SIZE + CMP_STRIDE, LANES), F32)],
    )
    return pl.pallas_call(
        functools.partial(_compress_kernel, pages_per_step=p),
        grid_spec=grid_spec,
        out_shape=[jax.ShapeDtypeStruct((nb, npg * CMP_SUB_PER_PAGE, KV_WIDTH), BF16)] * 2,
        compiler_params=_cparams(("arbitrary", "arbitrary")),
        name="compress",
    )(page_table, *([pages] * p), pages, w0t, w1t, bias, kgain)


TQ = 128
TK = 512
TKW = 128
QROWS = Q_PER_KV * TQ


def _select_members_t(score_t, n_cand):
    rows = _iota((n_cand, 1), 0)
    rank = jnp.zeros(score_t.shape, F32)
    for i in range(n_cand):
        si = score_t[i:i + 1, :]
        ahead = jnp.where(si > score_t, 1.0, jnp.where(si == score_t, jnp.where(rows > i, 1.0, 0.0), 0.0))
        rank = rank + ahead
    return jnp.where(rank < float(N_SELECT), 1.0, 0.0)


def _prompt_attn_kernel(q_ref, gate_ref, kc_ref, vct_ref, ks_ref, vst_ref, kw_ref, vwt_ref, ct_ref, ext_ref,
                        o_ref, qpad_ref, memb_ref, tot_ref, *stat_refs, n_slc):
    qi = pl.program_id(1)
    q0 = qi * TQ
    lane = _iota((1, QROWS), 1)
    r_lane = lane // TQ
    qidx = q0 + lane % TQ
    qpos_l = q0 + _iota((1, TQ), 1)
    lane_g = _iota((1, KV_WIDTH), 1) // HEAD_DIM
    n_cmp_pad = kc_ref.shape[0]
    n_kt = memb_ref.shape[1]
    gates_t = gate_ref[...].T

    def slope_row(g):
        row = jnp.zeros((1, QROWS), F32)
        for r in range(Q_PER_KV):
            row = jnp.where(r_lane == r, SLOPES[g * Q_PER_KV + r] * LOG2E, row)
        return row

    def gate_row(g, k):
        c = k * N_HEADS + g * Q_PER_KV
        return jnp.concatenate([gates_t[c + r:c + r + 1, :] for r in range(Q_PER_KV)], axis=1)

    c_start = _iota((n_cmp_pad, QROWS), 0) * CMP_STRIDE
    c_mid = c_start.astype(F32) + 0.5 * (CMP_LEN - 1)
    cmask = (c_start + (CMP_LEN - 1)) <= qidx
    kidx = _iota((TK, TQ), 0)

    for g in range(KV_HEADS):
        qpad = jnp.concatenate(
            [jnp.where(lane_g == g, q_ref[:, r * KV_WIDTH:(r + 1) * KV_WIDTH], jnp.zeros((), BF16))
             for r in range(Q_PER_KV)], axis=0)
        qpad_ref[g] = qpad
        s = _dot_t(kc_ref[...], qpad) + slope_row(g) * c_mid
        s = jnp.where(cmask, s, NEG_INF)
        e = jnp.where(cmask, jnp.exp2(s - jnp.max(s, axis=0, keepdims=True)), 0.0)
        l = jnp.sum(e, axis=0, keepdims=True)
        pc = e * jnp.where(l > 0.0, 1.0 / l, 0.0)
        oc = _dot(vct_ref[g * HEAD_DIM:(g + 1) * HEAD_DIM, :], pc.astype(BF16))
        tot_ref[g] = oc * gate_row(g, 0)
        p_sum = pc[:, 0:TQ]
        for r in range(1, Q_PER_KV):
            p_sum = p_sum + pc[:, r * TQ:(r + 1) * TQ]
        hi, lo = _split_bf16(p_sum)
        imp_t = (_dot(ct_ref[...], hi) + _dot(ct_ref[...], lo))[0:n_slc]
        blk = _iota((n_slc, 1), 0)
        qblk = qpos_l // SLC_BLOCK
        forced = (blk == 0) | (blk == qblk) | (blk == qblk - 1)
        score_t = jnp.where(forced, FORCE_SCORE, jnp.where(blk * SLC_BLOCK <= qpos_l, imp_t, NEG_INF))
        member_t = _select_members_t(score_t, n_slc)
        member_t = jnp.concatenate([member_t, jnp.zeros((LANES - n_slc, TQ), F32)], axis=0).astype(BF16)
        memb_keys = _dot(ext_ref[...], member_t)
        for j in range(n_kt):
            ok = jnp.where(kidx + j * TK <= qpos_l, memb_keys[j * TK:(j + 1) * TK, :], 0.0)
            memb_ref[g, j] = jnp.where(ok > 0.5, 0.0, NEG_INF)

    m_refs, l_refs, acc_refs = (stat_refs[i * KV_HEADS:(i + 1) * KV_HEADS] for i in range(3))

    def sweep(k_ref, vt_ref, lo_t, hi_t, shared_fn, mask_fn, gate_k):
        for g in range(KV_HEADS):
            m_refs[g][...] = jnp.full(m_refs[g].shape, NEG_INF, F32)
            l_refs[g][...] = jnp.zeros(l_refs[g].shape, F32)
            acc_refs[g][...] = jnp.zeros(acc_refs[g].shape, F32)

        def body(kj, carry):
            k0 = pl.multiple_of(kj * TK, TK)
            kt = k_ref[pl.ds(k0, TK), :]
            vt = vt_ref[kj]
            kpos = (kidx + k0).astype(F32)
            shared = shared_fn(k0)
            for g in range(KV_HEADS):
                s_all = _dot_t(kt, qpad_ref[g])
                mask = mask_fn(g, kj, shared)
                vg = vt[g * HEAD_DIM:(g + 1) * HEAD_DIM, :]
                for r in range(Q_PER_KV):
                    cs = slice(r * TQ, (r + 1) * TQ)
                    s = s_all[:, cs] + (SLOPES[g * Q_PER_KV + r] * LOG2E) * kpos + mask
                    m_old = m_refs[g][:, cs]
                    m_new = jnp.maximum(m_old, jnp.max(s, axis=0, keepdims=True))
                    alpha = jnp.exp2(m_old - m_new)
                    p = jnp.exp2(s - m_new)
                    l_refs[g][:, cs] = alpha * l_refs[g][:, cs] + jnp.sum(p, axis=0, keepdims=True)
                    acc_refs[g][:, cs] = alpha * acc_refs[g][:, cs] + _dot(vg, p.astype(BF16))
                    m_refs[g][:, cs] = m_new
            return carry

        lax.fori_loop(lo_t, hi_t, body, 0)
        for g in range(KV_HEADS):
            l = l_refs[g][...]
            tot_ref[g] = tot_ref[g] + acc_refs[g][...] * (jnp.where(l > 0.0, 1.0 / l, 0.0) * gate_row(g, gate_k))

    n_hi = (q0 + TQ + TK - 1) // TK
    sweep(ks_ref, vst_ref, 0, n_hi, lambda k0: None, lambda g, kj, shared: memb_ref[g, kj], 1)

    n_wt = (WINDOW + TQ) // TKW
    nwk = n_wt * TKW
    j0 = jnp.maximum(q0 - WINDOW, 0) // TKW
    w0 = pl.multiple_of(j0 * TKW, TKW)
    kt = kw_ref[pl.ds(w0, nwk), :]
    vt = jnp.concatenate([vwt_ref[j0 + i] for i in range(n_wt)], axis=1)
    d = _iota((nwk, TQ), 0) - _iota((nwk, TQ), 1) + (w0 - q0)
    wmask = jnp.where(d <= 0, jnp.where(d > -WINDOW, 0.0, NEG_INF), NEG_INF)
    wpos = (_iota((nwk, TQ), 0) + w0).astype(F32)
    for g in range(KV_HEADS):
        s_all = _dot_t(kt, qpad_ref[g])
        vg = vt[g * HEAD_DIM:(g + 1) * HEAD_DIM, :]
        for r in range(Q_PER_KV):
            cs = slice(r * TQ, (r + 1) * TQ)
            s = s_all[:, cs] + (SLOPES[g * Q_PER_KV + r] * LOG2E) * wpos + wmask
            e = jnp.exp2(s - jnp.max(s, axis=0, keepdims=True))
            scale = gate_row(g, 2)[:, cs] / jnp.sum(e, axis=0, keepdims=True)
            tot_ref[g, :, cs] = tot_ref[g, :, cs] + _dot(vg, e.astype(BF16)) * scale

    total = jnp.concatenate([tot_ref[g] for g in range(KV_HEADS)], axis=0)
    for r in range(Q_PER_KV):
        o_ref[:, r * KV_WIDTH:(r + 1) * KV_WIDTH] = total[:, r * TQ:(r + 1) * TQ].T.astype(BF16)


def _prompt_attn(q, gate, kc, vct, kvs_b, vst, kvw_b, vwt, ct, ext, *, batch, seq):
    n_slc = -(-seq // SLC_BLOCK)
    n_cmp_pad = kc.shape[1]
    n_qt = seq // TQ
    n_kt = seq // TK
    rowblk = lambda w: pl.BlockSpec((TQ, w), lambda b, i: (b * n_qt + i, 0))
    kblk = pl.BlockSpec((seq, KV_WIDTH), lambda b, i: (b, 0))
    assert seq >= WINDOW + TQ
    vtblk = pl.BlockSpec((n_kt, KV_WIDTH, TK), lambda b, i: (b, 0, 0))
    vwblk = pl.BlockSpec((seq // TKW, KV_WIDTH, TKW), lambda b, i: (b, 0, 0))
    full = lambda a: pl.BlockSpec(a.shape, lambda b, i: (0,) * a.ndim)
    return pl.pallas_call(
        functools.partial(_prompt_attn_kernel, n_slc=n_slc),
        grid=(batch, n_qt),
        in_specs=[rowblk(ATTN_WIDTH), rowblk(GATE_PAD),
                  pl.BlockSpec((None, n_cmp_pad, KV_WIDTH), lambda b, i: (b, 0, 0)),
                  pl.BlockSpec((None, KV_WIDTH, n_cmp_pad), lambda b, i: (b, 0, 0)),
                  kblk, vtblk, kblk, vwblk, full(ct), full(ext)],
        out_specs=rowblk(ATTN_WIDTH),
        out_shape=jax.ShapeDtypeStruct((batch * seq, ATTN_WIDTH), BF16),
        scratch_shapes=[pltpu.VMEM((KV_HEADS, QROWS, KV_WIDTH), BF16),
                        pltpu.VMEM((KV_HEADS, n_kt, TK, TQ), F32),
                        pltpu.VMEM((KV_HEADS, HEAD_DIM, QROWS), F32)]
                       + [pltpu.VMEM((1, QROWS), F32)] * (2 * KV_HEADS) + [pltpu.VMEM((HEAD_DIM, QROWS), F32)] * KV_HEADS,
        compiler_params=_cparams(("arbitrary", "arbitrary")),
        name="prompt_attn",
    )(q, gate, kc, vct, kvs_b, vst, kvw_b, vwt, ct, ext)


DROWS = N_HEADS
NEVER = -3e38


def _decode_qpad(q_ref):
    lane_g = _iota((KV_HEADS, KV_WIDTH), 1) // HEAD_DIM
    row_g = _iota((KV_HEADS, KV_WIDTH), 0)
    parts = []
    for r in range(Q_PER_KV):
        qr = jnp.broadcast_to(q_ref[:, r * KV_WIDTH:(r + 1) * KV_WIDTH].astype(F32), (KV_HEADS, KV_WIDTH))
        parts.append(jnp.where(lane_g == row_g, qr, 0.0))
    return jnp.concatenate(parts, axis=0).astype(BF16)


def _compress_t_kernel(pt_ref, *refs, pages_per_step):
    del pt_ref
    p = pages_per_step
    page_refs = refs[:p]
    w0_ref, w1_ref, a0_out, a1_out = refs[p:]
    prow = _iota((PAGE_SIZE, LANES), 0)
    ocol = _iota((PAGE_SIZE, LANES), 1)
    a0 = jnp.zeros(a0_out.shape, F32)
    a1 = jnp.zeros(a1_out.shape, F32)
    for k in range(p):
        sel = jnp.where(ocol == k * CMP_SUB_PER_PAGE + prow // CMP_STRIDE, 1.0, 0.0).astype(BF16)
        x = page_refs[k][...]
        a0 = a0 + _dot((x * w0_ref[...]).astype(BF16), sel)
        a1 = a1 + _dot((x * w1_ref[...]).astype(BF16), sel)
    a0_out[...] = a0
    a1_out[...] = a1


def _compress_t(page_table, pages_t, w0t, w1t, *, pages_per_step):
    nb, npg = page_table.shape
    p = pages_per_step
    assert p * CMP_SUB_PER_PAGE == LANES

    def page_spec(k):
        return pl.BlockSpec((None, 2 * KV_WIDTH, PAGE_SIZE), lambda b, i, pt: (pt[b, i * p + k], 0, 0))

    full = lambda a: pl.BlockSpec(a.shape, lambda b, i, pt: (0,) * a.ndim)
    out_spec = pl.BlockSpec((None, 2 * KV_WIDTH, LANES), lambda b, i, pt: (b, 0, i))
    grid_spec = pltpu.PrefetchScalarGridSpec(
        num_scalar_prefetch=1,
        grid=(nb, npg // p),
        in_specs=[page_spec(k) for k in range(p)] + [full(w0t), full(w1t)],
        out_specs=[out_spec, out_spec],
    )
    return pl.pallas_call(
        functools.partial(_compress_t_kernel, pages_per_step=p),
        grid_spec=grid_spec,
        out_shape=[jax.ShapeDtypeStruct((nb, 2 * KV_WIDTH, npg * CMP_SUB_PER_PAGE), F32)] * 2,
        compiler_params=_cparams(("arbitrary", "arbitrary")),
        name="compress_t",
    )(page_table, *([pages_t] * p), w0t, w1t)


def _decode_cmp_kernel(q_ref, slope_ref, a0_ref, a1_ref, cb_ref, kg_ref, c_ref, oc_out, member_out, *, qpos, n_slc):
    qpad = _decode_qpad(q_ref)
    slope = slope_ref[...]
    n_cmp_pad = a0_ref.shape[1]
    acc = a0_ref[...] + pltpu.roll(a1_ref[...], n_cmp_pad - 1, 1) + cb_ref[...]
    kparts = []
    for g in range(KV_HEADS):
        kg = acc[g * HEAD_DIM:(g + 1) * HEAD_DIM, :]
        ms = jnp.mean(kg * kg, axis=0, keepdims=True)
        kparts.append((kg * lax.rsqrt(ms + EPS)) * kg_ref[...])
    kc_t = jnp.concatenate(kparts, axis=0).astype(BF16)
    vc_t = acc[KV_WIDTH:, :].astype(BF16)
    c_start = _iota((1, n_cmp_pad), 1) * CMP_STRIDE
    c_mid = c_start.astype(F32) + 0.5 * (CMP_LEN - 1)
    cmask = (c_start + (CMP_LEN - 1)) <= qpos
    s = _dot(qpad, kc_t) + slope * c_mid
    s = jnp.where(cmask, s, NEG_INF)
    e = jnp.where(cmask, jnp.exp2(s - jnp.max(s, axis=-1, keepdims=True)), 0.0)
    l = jnp.sum(e, axis=-1, keepdims=True)
    pc = e * jnp.where(l > 0.0, 1.0 / l, 0.0)
    oc_out[...] = _dot_t(pc.astype(BF16), vc_t)
    p_sum = pc[0:KV_HEADS]
    for r in range(1, Q_PER_KV):
        p_sum = p_sum + pc[r * KV_HEADS:(r + 1) * KV_HEADS]
    p_sum = jnp.concatenate([p_sum, jnp.zeros((DROWS - KV_HEADS, n_cmp_pad), F32)], axis=0)
    hi, lo = _split_bf16(p_sum)
    imp = _dot(hi, c_ref[...]) + _dot(lo, c_ref[...])
    n_pad = imp.shape[1]
    blk = _iota((1, n_pad), 1)
    qblk = qpos // SLC_BLOCK
    forced = (blk == 0) | (blk == qblk) | (blk == qblk - 1)
    score = jnp.where(forced, FORCE_SCORE, jnp.where(blk * SLC_BLOCK <= qpos, imp, NEG_INF))
    score = jnp.where(blk < n_slc, score, NEVER)
    rank = jnp.zeros(score.shape, F32)
    for i in range(n_slc):
        si = score[:, i:i + 1]
        rank = rank + jnp.where(si > score, 1.0, jnp.where(si == score, jnp.where(blk > i, 1.0, 0.0), 0.0))
    member_out[...] = jnp.where(rank < float(min(N_SELECT, n_slc)), 1.0, 0.0)


def _decode_cmp(q3, slopes, a0, a1, cb_col, kg_col, cmat, *, qpos, n_slc):
    nb = q3.shape[0]
    n_cmp_pad = a0.shape[2]
    n_pad = cmat.shape[1]
    per_b = lambda r, c: pl.BlockSpec((None, r, c), lambda b: (b, 0, 0))
    full = lambda a: pl.BlockSpec(a.shape, lambda b: (0,) * a.ndim)
    return pl.pallas_call(
        functools.partial(_decode_cmp_kernel, qpos=qpos, n_slc=n_slc),
        grid=(nb,),
        in_specs=[per_b(1, ATTN_WIDTH), full(slopes), per_b(2 * KV_WIDTH, n_cmp_pad), per_b(2 * KV_WIDTH, n_cmp_pad),
                  full(cb_col), full(kg_col), full(cmat)],
        out_specs=[per_b(DROWS, KV_WIDTH), per_b(DROWS, n_pad)],
        out_shape=[jax.ShapeDtypeStruct((nb, DROWS, KV_WIDTH), F32), jax.ShapeDtypeStruct((nb, DROWS, n_pad), F32)],
        compiler_params=_cparams(("arbitrary",)),
        name="decode_cmp",
    )(q3, slopes, a0, a1, cb_col, kg_col, cmat)


def _decode_sw_kernel(pt_ref, q_ref, slope_ref, oc_ref, member_ref, gate_ref, gexp_ref, win_ref, news_ref, neww_ref,
                      *refs, pages_per_step, qpos, past, n_buf):
    del pt_ref
    p = pages_per_step
    page_refs = refs[:p]
    o_ref, m_ref, l_ref, acc_ref, ow_ref = refs[p:]
    c = pl.program_id(1)
    qpad = _decode_qpad(q_ref)
    qf = qpad.astype(F32)
    slope = slope_ref[...]

    def new_row(row_ref):
        kn = row_ref[:, :KV_WIDTH].astype(BF16).astype(F32)
        vn = row_ref[:, KV_WIDTH:].astype(BF16).astype(F32)
        return jnp.sum(qf * kn, axis=-1, keepdims=True) + slope * float(qpos), vn

    @pl.when(c == 0)
    def _():
        m_ref[...] = jnp.full((DROWS, 1), NEG_INF, F32)
        l_ref[...] = jnp.zeros((DROWS, 1), F32)
        acc_ref[...] = jnp.zeros((DROWS, KV_WIDTH), F32)
        kw_t = win_ref[:KV_WIDTH, :].astype(BF16)
        vw_t = win_ref[KV_WIDTH:, :].astype(BF16)
        wpos = past - n_buf + _iota((1, n_buf), 1)
        ok = (wpos <= qpos) & (wpos > qpos - WINDOW) & (wpos >= 0)
        s = jnp.where(ok, _dot(qpad, kw_t) + slope * wpos.astype(F32), NEG_INF)
        s_new, v_new = new_row(neww_ref)
        m = jnp.maximum(jnp.max(s, axis=-1, keepdims=True), s_new)
        e = jnp.where(ok, jnp.exp2(s - m), 0.0)
        e_new = jnp.exp2(s_new - m)
        l = jnp.sum(e, axis=-1, keepdims=True) + e_new
        ow_ref[...] = (_dot_t(e.astype(BF16), vw_t) + e_new * v_new) / l

    nk = p * PAGE_SIZE
    kt_t = jnp.concatenate([page_refs[k][:KV_WIDTH, :].astype(BF16) for k in range(p)], axis=1)
    vt_t = jnp.concatenate([page_refs[k][KV_WIDTH:, :].astype(BF16) for k in range(p)], axis=1)
    member = member_ref[0:KV_HEADS, :].astype(BF16)
    member = jnp.concatenate([member] * Q_PER_KV, axis=0)
    n_pad = member.shape[1]
    kidx = c * nk + _iota((n_pad, nk), 1)
    expand = jnp.where(kidx // SLC_BLOCK == _iota((n_pad, nk), 0), 1.0, 0.0).astype(BF16)
    mk = _dot(member, expand)
    kpos = (c * nk + _iota((1, nk), 1)).astype(F32)
    s = jnp.where(mk > 0.5, _dot(qpad, kt_t) + slope * kpos, NEG_INF)
    m_old = m_ref[...]
    m_new = jnp.maximum(m_old, jnp.max(s, axis=-1, keepdims=True))
    alpha = jnp.exp2(m_old - m_new)
    pr = jnp.exp2(s - m_new)
    l_ref[...] = alpha * l_ref[...] + jnp.sum(pr, axis=-1, keepdims=True)
    acc_ref[...] = alpha * acc_ref[...] + _dot_t(pr.astype(BF16), vt_t)
    m_ref[...] = m_new

    @pl.when(c == pl.num_programs(1) - 1)
    def _():
        new_blk = qpos // SLC_BLOCK
        is_member = jnp.concatenate([member_ref[0:KV_HEADS, new_blk:new_blk + 1]] * Q_PER_KV, axis=0) > 0.5
        s_new, v_new = new_row(news_ref)
        s_new = jnp.where(is_member, s_new, NEG_INF)
        m_old = m_ref[...]
        m_new = jnp.maximum(m_old, s_new)
        alpha = jnp.exp2(m_old - m_new)
        e_new = jnp.where(is_member, jnp.exp2(s_new - m_new), 0.0)
        l = alpha * l_ref[...] + e_new
        os = (alpha * acc_ref[...] + e_new * v_new) / l

        lane_g = _iota((1, KV_WIDTH), 1) // HEAD_DIM

        def flat(o):
            segs = []
            for r in range(Q_PER_KV):
                seg = jnp.zeros((1, KV_WIDTH), F32)
                for g in range(KV_HEADS):
                    i = r * KV_HEADS + g
                    seg = seg + jnp.where(lane_g == g, o[i:i + 1, :], 0.0)
                segs.append(seg)
            return jnp.concatenate(segs, axis=1)

        ghi, glo = _split_bf16(jnp.broadcast_to(gate_ref[...], (DROWS, GATE_PAD)))
        gx = (_dot(ghi, gexp_ref[...]) + _dot(glo, gexp_ref[...]))[0:1]
        o = (gx[:, 0:ATTN_WIDTH] * flat(oc_ref[...])
             + gx[:, ATTN_WIDTH:2 * ATTN_WIDTH] * flat(os)
             + gx[:, 2 * ATTN_WIDTH:] * flat(ow_ref[...]))
        o_ref[...] = o.astype(BF16)


def _decode_sw(page_table, q3, slopes, oc, member, gate3, gexp, win_state_t, new_s, new_w, pages_t,
               *, pages_per_step, qpos, past):
    nb, npg = page_table.shape
    p = pages_per_step
    n_buf = win_state_t.shape[2]
    n_pad = member.shape[2]
    per_b = lambda r, c: pl.BlockSpec((None, r, c), lambda b, i, pt: (b, 0, 0))
    full = lambda a: pl.BlockSpec(a.shape, lambda b, i, pt: (0,) * a.ndim)

    def page_spec(k):
        return pl.BlockSpec((None, 2 * KV_WIDTH, PAGE_SIZE), lambda b, i, pt: (pt[b, i * p + k], 0, 0))

    grid_spec = pltpu.PrefetchScalarGridSpec(
        num_scalar_prefetch=1,
        grid=(nb, npg // p),
        in_specs=[per_b(1, ATTN_WIDTH), full(slopes), per_b(DROWS, KV_WIDTH), per_b(DROWS, n_pad), per_b(1, GATE_PAD),
                  full(gexp), per_b(2 * KV_WIDTH, n_buf), per_b(1, 2 * KV_WIDTH), per_b(1, 2 * KV_WIDTH)]
                 + [page_spec(k) for k in range(p)],
        out_specs=per_b(1, ATTN_WIDTH),
        scratch_shapes=[pltpu.VMEM((DROWS, 1), F32), pltpu.VMEM((DROWS, 1), F32),
                        pltpu.VMEM((DROWS, KV_WIDTH), F32), pltpu.VMEM((DROWS, KV_WIDTH), F32)],
    )
    return pl.pallas_call(
        functools.partial(_decode_sw_kernel, pages_per_step=p, qpos=qpos, past=past, n_buf=n_buf),
        grid_spec=grid_spec,
        out_shape=jax.ShapeDtypeStruct((nb, 1, ATTN_WIDTH), BF16),
        compiler_params=_cparams(("arbitrary", "arbitrary")),
        name="decode_slc_win",
    )(page_table, q3, slopes, oc, member, gate3, gexp, win_state_t, new_s, new_w, *([pages_t] * p))


def _outproj_kernel(x_ref, oa_ref, d_ref, wpm_ref, spm_ref, woa_ref, wop_ref, g2_ref, wrh_ref, wrl_ref, br_ref,
                    x1_out, h2_out, tope_out, gate_out):
    n_grp, pm_group = wpm_ref.shape[0], wpm_ref.shape[1]
    pm = jnp.concatenate([_dot(d_ref[:, gi * pm_group:(gi + 1) * pm_group], wpm_ref[gi]) for gi in range(n_grp)], axis=1)
    pm = pm * spm_ref[...]
    x1 = x_ref[...] + _dot(oa_ref[...], woa_ref[...]) + _dot(pm.astype(BF16), wop_ref[...])
    x1_out[...] = x1
    ms = jnp.mean(x1 * x1, axis=-1, keepdims=True)
    h2 = (x1 * lax.rsqrt(ms + EPS)) * g2_ref[...]
    half = h2.shape[1] // 2
    bits = lax.bitcast_convert_type(h2.astype(BF16).astype(F32), jnp.uint32)
    h2_out[...] = (bits[:, half:] & jnp.uint32(0xFFFF0000)) | (bits[:, :half] >> 16)
    hi, lo = _split_bf16(h2)
    logits = _dot(hi, wrh_ref[...]) + _dot(lo, wrh_ref[...]) + _dot(hi, wrl_ref[...]) + br_ref[...]
    tm = logits.shape[0]
    lane = _iota((tm, LANES), 1)
    logits = jnp.where(lane < N_EXPERTS, logits, NEVER)
    vals, idxs = [], []
    for _ in range(TOP_K):
        m = jnp.max(logits, axis=-1, keepdims=True)
        idx = jnp.min(jnp.where(logits == m, lane, LANES), axis=-1, keepdims=True)
        vals.append(m)
        idxs.append(idx)
        logits = jnp.where(lane == idx, NEVER, logits)
    es = [jnp.exp(v - vals[0]) for v in vals]
    den = es[0]
    for e in es[1:]:
        den = den + e
    tope = jnp.full((tm, LANES), -1, I32)
    gts = jnp.zeros((tm, LANES), F32)
    for k in range(TOP_K):
        tope = jnp.where(lane == k, idxs[k], tope)
        gts = jnp.where(lane == k, es[k] / den, gts)
    tope_out[...] = tope
    gate_out[...] = gts


def _outproj(x2d, o_attn, dpool, wpm, spm, woa, wop, g2, wrh, wrl, br, *, tm):
    n, d_model = x2d.shape
    row = lambda w: pl.BlockSpec((tm, w), lambda i: (i, 0))
    full = lambda a: pl.BlockSpec(a.shape, lambda i: (0,) * a.ndim)
    consts = [wpm, spm, woa, wop, g2, wrh, wrl, br]
    return pl.pallas_call(
        _outproj_kernel,
        grid=(n // tm,),
        in_specs=[row(d_model), row(o_attn.shape[1]), row(dpool.shape[1])] + [full(a) for a in consts],
        out_specs=[row(d_model), row(d_model // 2), row(LANES), row(LANES)],
        out_shape=[jax.ShapeDtypeStruct((n, d_model), F32), jax.ShapeDtypeStruct((n, d_model // 2), jnp.uint32),
                   jax.ShapeDtypeStruct((n, LANES), I32), jax.ShapeDtypeStruct((n, LANES), F32)],
        compiler_params=_cparams(("arbitrary",)),
        name="outproj",
    )(x2d, o_attn, dpool, *consts)


MOE_SUB = 256
MOE_ROWS = 5 * MOE_SUB
MOE_FC = 256
ROUTE_TILE = 512


def _route_kernel(e_ref, rank_out, cnt_out, carry_ref):
    @pl.when(pl.program_id(0) == 0)
    def _():
        carry_ref[...] = jnp.zeros(carry_ref.shape, F32)

    tr = e_ref.shape[0]
    lane = _iota((tr, LANES), 1)
    e = e_ref[...]
    ohs = [jnp.where(e[:, k:k + 1] == lane, 1.0, 0.0) for k in range(TOP_K)]
    tot = ohs[0]
    for oh in ohs[1:]:
        tot = tot + oh
    lower = jnp.where(_iota((tr, tr), 1) < _iota((tr, tr), 0), 1.0, 0.0).astype(BF16)
    before = _dot(lower, tot.astype(BF16)) + carry_ref[...]
    rank = jnp.zeros((tr, LANES), I32)
    for k in range(TOP_K):
        rk = jnp.sum(ohs[k] * before, axis=-1, keepdims=True).astype(I32)
        rank = jnp.where(lane == k, rk, rank)
    rank_out[...] = rank
    carry_ref[...] = carry_ref[...] + jnp.sum(tot, axis=0, keepdims=True)
    cnt_out[...] = carry_ref[...]


def _route(tope):
    n = tope.shape[0]
    return pl.pallas_call(
        _route_kernel,
        grid=(n // ROUTE_TILE,),
        in_specs=[pl.BlockSpec((ROUTE_TILE, LANES), lambda i: (i, 0))],
        out_specs=[pl.BlockSpec((ROUTE_TILE, LANES), lambda i: (i, 0)), pl.BlockSpec((1, LANES), lambda i: (0, 0))],
        out_shape=[jax.ShapeDtypeStruct((n, LANES), I32), jax.ShapeDtypeStruct((1, LANES), F32)],
        scratch_shapes=[pltpu.VMEM((1, LANES), F32)],
        compiler_params=_cparams(("arbitrary",)),
        name="moe_route",
    )(tope)


def _slots_kernel(e_ref, rank_ref, per_ref, start_ref, slot_out):
    tr = e_ref.shape[0]
    lane = _iota((tr, LANES), 1)
    e = e_ref[...]
    rank = rank_ref[...]
    slot = jnp.zeros((tr, LANES), I32)
    for k in range(TOP_K):
        oh = jnp.where(e[:, k:k + 1] == lane, 1.0, 0.0)
        per_e = jnp.maximum(jnp.sum(oh * per_ref[...], axis=-1, keepdims=True), 1.0)
        start_e = jnp.sum(oh * start_ref[...], axis=-1, keepdims=True)
        rk = rank[:, k:k + 1].astype(F32)
        item = jnp.floor((rk + 0.5) / per_e)
        s = (start_e + item) * float(MOE_ROWS) + (rk - item * per_e)
        slot = jnp.where(lane == k, s.astype(I32), slot)
    slot_out[...] = slot


def _slots(tope, rank, per_row, start_row):
    n = tope.shape[0]
    blk = pl.BlockSpec((ROUTE_TILE, LANES), lambda i: (i, 0))
    row = pl.BlockSpec((1, LANES), lambda i: (0, 0))
    return pl.pallas_call(
        _slots_kernel,
        grid=(n // ROUTE_TILE,),
        in_specs=[blk, blk, row, row],
        out_specs=blk,
        out_shape=jax.ShapeDtypeStruct((n, LANES), I32),
        compiler_params=_cparams(("arbitrary",)),
        name="moe_slots",
    )(tope, rank, per_row, start_row)


DMA_UNROLL = 4


def _dispatch_kernel(slot_ref, h_ref, *refs):
    xbuf, sem = refs[-2], refs[-1]
    tr = h_ref.shape[0]

    def row_copy(i, s):
        return pltpu.make_async_copy(h_ref.at[pl.ds(i, 1)], xbuf.at[pl.ds(s, 1)], sem)

    def issue(i, carry):
        for k in range(TOP_K):
            row_copy(i, slot_ref[i * TOP_K + k]).start(priority=k % 2)
        return carry

    lax.fori_loop(0, tr, issue, 0, unroll=DMA_UNROLL)
    n_rows = tr * TOP_K
    pltpu.make_async_copy(xbuf.at[pl.ds(0, n_rows)], xbuf.at[pl.ds(0, n_rows)], sem).wait()


def _dispatch(slots_flat, h2, xbuf, n_rows, *, tr):
    n, d = h2.shape
    in_specs = [pl.BlockSpec((tr * TOP_K,), lambda i: (i,), memory_space=pltpu.SMEM),
                pl.BlockSpec((tr, d), lambda i: (i, 0))]
    args = [slots_flat, h2]
    aliases = {}
    if xbuf is not None:
        in_specs.append(pl.BlockSpec(memory_space=pl.ANY))
        args.append(xbuf)
        aliases = {2: 0}
    return pl.pallas_call(
        _dispatch_kernel,
        grid=(n // tr,),
        in_specs=in_specs,
        out_specs=pl.BlockSpec(memory_space=pl.ANY),
        out_shape=jax.ShapeDtypeStruct((n_rows, d), h2.dtype),
        scratch_shapes=[pltpu.SemaphoreType.DMA(())],
        input_output_aliases=aliases,
        compiler_params=pltpu.CompilerParams(dimension_semantics=("arbitrary",), vmem_limit_bytes=VMEM_LIMIT,
                                             has_side_effects=True),
        name="moe_dispatch",
    )(*args)


def _experts_kernel(we_ref, wr_ref, wb_ref, x_ref, wg_ref, wu_ref, bg_ref, bu_ref, wd_ref, bd_ref, o_ref, xb_ref):
    del we_ref, wb_ref
    w = pl.program_id(0)
    c = pl.program_id(1)
    rows = wr_ref[w]

    half = x_ref.shape[1]

    @pl.when(rows > 0)
    def _():
        @pl.when(c == 0)
        def _():
            valid = _iota((MOE_ROWS, 1), 0) < rows
            x = x_ref[...]
            lo = lax.bitcast_convert_type(x << 16, F32)
            hi = lax.bitcast_convert_type(x & jnp.uint32(0xFFFF0000), F32)
            xb_ref[:, :half] = jnp.where(valid, lo, 0.0).astype(BF16)
            xb_ref[:, half:] = jnp.where(valid, hi, 0.0).astype(BF16)
            o_ref[...] = jnp.broadcast_to(bd_ref[...], o_ref.shape)

        def sub_tile(start, size, wg, wu, wd):
            rs = slice(start * MOE_SUB, (start + size) * MOE_SUB)
            xs = xb_ref[rs, :]
            g = _dot(xs, wg) + bg_ref[...]
            u = _dot(xs, wu) + bu_ref[...]
            gh = jnp.minimum(g, SWIGLU_LIMIT)
            up = jnp.clip(u, -SWIGLU_LIMIT, SWIGLU_LIMIT)
            act = (up + 1.0) * gh * jax.nn.sigmoid(SWIGLU_ALPHA * gh)
            o_ref[rs, :] = o_ref[rs, :] + _dot(act.astype(BF16), wd)

        n_sub = (rows + MOE_SUB - 1) // MOE_SUB
        for n in range(1, MOE_ROWS // MOE_SUB + 1):
            @pl.when(n_sub == n)
            def _():
                ws = (wg_ref[...].astype(BF16), wu_ref[...].astype(BF16), wd_ref[...].astype(BF16))
                for i in range(0, n - 1, 2):
                    sub_tile(i, 2, *ws)
                if n % 2:
                    sub_tile(n - 1, 1, *ws)


def _experts(work_e, work_rows, work_blk, xbuf, w_gu, b_gu3, w_down, b_down3):
    n_work = work_e.shape[0]
    n_exp, d_model, two_ff = w_gu.shape
    d_ff = two_ff // 2
    nc = d_ff // MOE_FC

    def cidx(w, c, wr):
        return jnp.where(wr[w] > 0, c, nc - 1)

    grid_spec = pltpu.PrefetchScalarGridSpec(
        num_scalar_prefetch=3,
        grid=(n_work, nc),
        in_specs=[
            pl.BlockSpec((MOE_ROWS, d_model // 2), lambda w, c, we, wr, wb: (wb[w], 0)),
            pl.BlockSpec((None, d_model, MOE_FC), lambda w, c, we, wr, wb: (we[w], 0, cidx(w, c, wr))),
            pl.BlockSpec((None, d_model, MOE_FC), lambda w, c, we, wr, wb: (we[w], 0, nc + cidx(w, c, wr))),
            pl.BlockSpec((None, 1, MOE_FC), lambda w, c, we, wr, wb: (we[w], 0, cidx(w, c, wr))),
            pl.BlockSpec((None, 1, MOE_FC), lambda w, c, we, wr, wb: (we[w], 0, nc + cidx(w, c, wr))),
            pl.BlockSpec((None, MOE_FC, d_model), lambda w, c, we, wr, wb: (we[w], cidx(w, c, wr), 0)),
            pl.BlockSpec((None, 1, d_model), lambda w, c, we, wr, wb: (we[w], 0, 0)),
        ],
        out_specs=pl.BlockSpec((MOE_ROWS, d_model), lambda w, c, we, wr, wb: (wb[w], 0)),
        scratch_shapes=[pltpu.VMEM((MOE_ROWS, d_model), BF16)],
    )
    return pl.pallas_call(
        _experts_kernel,
        grid_spec=grid_spec,
        out_shape=jax.ShapeDtypeStruct((xbuf.shape[0], d_model), F32),
        compiler_params=_cparams(("arbitrary", "arbitrary")),
        name="moe_experts",
    )(work_e, work_rows, work_blk, xbuf, w_gu, w_gu, b_gu3, b_gu3, w_down, b_down3)


def _combine_kernel(slot_ref, x1_ref, gate_ref, ybuf, o_ref, rows_ref, sem):
    tc = x1_ref.shape[0]

    def row_copy(i, k):
        return pltpu.make_async_copy(ybuf.at[pl.ds(slot_ref[i * TOP_K + k], 1)], rows_ref.at[k, pl.ds(i, 1)], sem)

    def issue(i, carry):
        for k in range(TOP_K):
            row_copy(i, k).start(priority=k % 2)
        return carry

    lax.fori_loop(0, tc, issue, 0, unroll=DMA_UNROLL)
    pltpu.make_async_copy(rows_ref, rows_ref, sem).wait()
    gates = gate_ref[...]
    y = x1_ref[...]
    for k in range(TOP_K):
        y = y + gates[:, k:k + 1] * rows_ref[k]
    o_ref[...] = y


def _combine(slots_flat, x1, gates, ybuf, *, tc):
    n, d = x1.shape
    return pl.pallas_call(
        _combine_kernel,
        grid=(n // tc,),
        in_specs=[pl.BlockSpec((tc * TOP_K,), lambda i: (i,), memory_space=pltpu.SMEM),
                  pl.BlockSpec((tc, d), lambda i: (i, 0)),
                  pl.BlockSpec((tc, LANES), lambda i: (i, 0)),
                  pl.BlockSpec(memory_space=pl.ANY)],
        out_specs=pl.BlockSpec((tc, d), lambda i: (i, 0)),
        out_shape=jax.ShapeDtypeStruct((n, d), F32),
        scratch_shapes=[pltpu.VMEM((TOP_K, tc, d), F32), pltpu.SemaphoreType.DMA(())],
        compiler_params=_cparams(("arbitrary",)),
        name="moe_combine",
    )(slots_flat, x1, gates, ybuf)


def _moe(h2_p, h2_s, tope_p, tope_s, gate_p, gate_s, x1_p, x1_s, w_gu, b_gu, w_down, b_down):
    n_p, n_s = h2_p.shape[0], h2_s.shape[0]
    n_exp = w_gu.shape[0]
    pad = (-(n_p + n_s)) % ROUTE_TILE
    tope_all = jnp.concatenate([tope_p, tope_s, jnp.full((pad, LANES), -1, I32)], axis=0)
    rank, counts = _route(tope_all)
    counts = counts[0, :n_exp].astype(I32)
    n_assign = (n_p + n_s) * TOP_K
    n_work = n_assign // MOE_ROWS + n_exp
    items = (counts + MOE_ROWS - 1) // MOE_ROWS
    per = -(-counts // jnp.maximum(items, 1))
    per = jnp.maximum(-(-per // MOE_SUB) * MOE_SUB, MOE_SUB)
    item_end = jnp.cumsum(items)
    item_start = item_end - items
    n_used = item_end[-1]
    w_ids = jnp.arange(n_work, dtype=I32)
    used = w_ids < n_used
    w_eff = jnp.where(used, w_ids, n_used - 1)
    work_e = jnp.minimum(jnp.searchsorted(item_end, w_eff, side="right"), n_exp - 1).astype(I32)
    work_rows = jnp.clip(counts[work_e] - (w_eff - item_start[work_e]) * per[work_e], 0, per[work_e])
    work_rows = jnp.where(used, work_rows, 0).astype(I32)
    lane_pad = lambda v: jnp.pad(v.astype(F32), (0, LANES - n_exp))[None, :]
    slots = _slots(tope_all, rank, lane_pad(per), lane_pad(item_start))[:n_p + n_s, :TOP_K].reshape(-1)
    slots_p, slots_s = slots[:n_p * TOP_K], slots[n_p * TOP_K:]
    n_rows = n_work * MOE_ROWS
    xbuf = _dispatch(slots_p, h2_p, None, n_rows, tr=512)
    xbuf = _dispatch(slots_s, h2_s, xbuf, n_rows, tr=n_s)
    ybuf = _experts(work_e, work_rows, w_eff.astype(I32), xbuf, w_gu, b_gu[:, None, :], w_down, b_down[:, None, :])
    y_p = _combine(slots_p, x1_p, gate_p, ybuf, tc=128)
    y_s = _combine(slots_s, x1_s, gate_s, ybuf, tc=n_s)
    return y_p, y_s


def _head_perm():
    return [g * Q_PER_KV + r for r in range(Q_PER_KV) for g in range(KV_HEADS)]


def _pack_w_in(w_in):
    d_model = w_in.shape[0]
    pm_width = d_model - ATTN_WIDTH
    s0 = ATTN_WIDTH
    s1 = s0 + 2 * KV_WIDTH
    s2 = s1 + 2 * KV_WIDTH
    s3 = s2 + 2 * KV_WIDTH
    s4 = s3 + 3 * N_HEADS
    wq = w_in[:, :s0].reshape(d_model, N_HEADS, HEAD_DIM)[:, jnp.array(_head_perm())].reshape(d_model, ATTN_WIDTH)
    wg = w_in[:, s3:s4].reshape(d_model, N_HEADS, 3).transpose(0, 2, 1).reshape(d_model, 3 * N_HEADS)
    wg = jnp.pad(wg, ((0, 0), (0, GATE_PAD - 3 * N_HEADS)))
    return jnp.concatenate([wq, w_in[:, s0:s3], w_in[:, s4:s4 + pm_width], wg], axis=1).astype(BF16)


def _cmp_weights(w_cmp_k, b_cmp_k, w_cmp_v, b_cmp_v):
    def half(o):
        wk = jnp.tile(w_cmp_k[o * CMP_STRIDE:(o + 1) * CMP_STRIDE], (1, KV_HEADS))
        wv = jnp.tile(w_cmp_v[o * CMP_STRIDE:(o + 1) * CMP_STRIDE], (1, KV_HEADS))
        return jnp.concatenate([wk, wv], axis=1)
    bias = jnp.concatenate([jnp.tile(b_cmp_k, KV_HEADS), jnp.tile(b_cmp_v, KV_HEADS)])[None, :]
    return half(0), half(1), bias


def _cmp_to_slc_t(n_cmp_pad, n_cmp, n_slc, n_slc_pad):
    i0 = jnp.arange(n_cmp_pad)[None, :] * CMP_STRIDE
    j0 = jnp.arange(n_slc_pad)[:, None] * SLC_BLOCK
    shared = jnp.minimum(i0 + CMP_LEN, j0 + SLC_BLOCK) - jnp.maximum(i0, j0)
    frac = jnp.clip(shared, 0, None).astype(F32) / CMP_LEN
    ok = (jnp.arange(n_cmp_pad)[None, :] < n_cmp) & (jnp.arange(n_slc_pad)[:, None] < n_slc)
    return jnp.where(ok, frac, 0.0).astype(BF16)


def _block_expand(n_blk_pad, n_keys):
    return (jnp.arange(n_blk_pad)[:, None] == (jnp.arange(n_keys)[None, :] // SLC_BLOCK)).astype(BF16)


def _prompt_mixer(x_prompt, p):
    batch, seq, d_model = x_prompt.shape
    n = batch * seq
    q, kvc, kvs, kvw, gate, u, dpool, ks_b, vs_t, kw_b, vw_t = _inproj(
        x_prompt.reshape(n, d_model), p["g1"], p["w_in"], p["qg"], p["ksg"], p["kwg"], tm=256, seq_len=seq)
    npg = seq // PAGE_SIZE
    pt = (jnp.arange(batch, dtype=I32)[:, None] * npg + jnp.arange(npg, dtype=I32)[None, :])
    kc, vc = _compress(pt, kvc.reshape(batch * npg, PAGE_SIZE, 2 * KV_WIDTH), p["cw0"], p["cw1"], p["cb"], p["kcg"],
                       pages_per_step=8)
    n_cmp = seq // CMP_STRIDE - 1
    n_slc = -(-seq // SLC_BLOCK)
    ct = _cmp_to_slc_t(kc.shape[1], n_cmp, n_slc, LANES)
    ext = _block_expand(LANES, seq).T

    o_attn = _prompt_attn(q, gate, kc, jnp.swapaxes(vc, 1, 2), ks_b, vs_t, kw_b, vw_t, ct, ext, batch=batch, seq=seq)
    return o_attn, dpool, kvc, kvs, kvw, u


def _sample_mixer(x_sample, cache_cmp, cache_slc, state_win, state_pool, page_table, p):
    nb, t, d_model = x_sample.shape
    assert t == 1, "decode path handles one new row per sequence"
    npg = page_table.shape[1]
    past = npg * PAGE_SIZE
    qpos = past
    q, kvc, kvs, kvw, gate, u, dpool = _inproj(
        x_sample.reshape(nb, d_model), p["g1"], p["w_in"], p["qg"], p["ksg"], p["kwg"], tm=nb,
        pool_state=state_pool, hist_pos=qpos)
    def rows_on_lanes(a):
        return jnp.transpose(a, (0, 2, 3, 4, 1)).reshape(a.shape[0], 2 * KV_WIDTH, a.shape[1])

    cw0_t = jnp.tile(p["cw0"].T, (1, PAGE_SIZE // CMP_STRIDE))
    cw1_t = jnp.tile(p["cw1"].T, (1, PAGE_SIZE // CMP_STRIDE))
    a0, a1 = _compress_t(page_table, rows_on_lanes(cache_cmp), cw0_t, cw1_t, pages_per_step=LANES // CMP_SUB_PER_PAGE)
    n_cmp = (past + t) // CMP_STRIDE - 1
    n_slc = -(-(past + t) // SLC_BLOCK)
    n_pad = -(-n_slc // LANES) * LANES
    cmat = _cmp_to_slc_t(a0.shape[2], n_cmp, n_slc, n_pad).T
    slopes = jnp.array([SLOPES[g * Q_PER_KV + r] * LOG2E for r in range(Q_PER_KV) for g in range(KV_HEADS)], F32)[:, None]
    q3 = q.reshape(nb, 1, ATTN_WIDTH)
    oc, member = _decode_cmp(q3, slopes, a0, a1, p["cb"].T, p["kcg"][:, :HEAD_DIM].T, cmat, qpos=qpos, n_slc=n_slc)
    rows = jnp.arange(3 * N_HEADS)
    k_i, g_i, r_i = rows // N_HEADS, (rows % N_HEADS) // Q_PER_KV, rows % Q_PER_KV
    col_head = k_i * N_HEADS + r_i * KV_HEADS + g_i
    gexp = (jnp.arange(3 * ATTN_WIDTH)[None, :] // HEAD_DIM == col_head[:, None])
    gexp = jnp.pad(gexp, ((0, GATE_PAD - 3 * N_HEADS), (0, 0))).astype(BF16)
    o = _decode_sw(page_table, q3, slopes, oc, member, gate.reshape(nb, 1, GATE_PAD), gexp,
                   rows_on_lanes(state_win), kvs.reshape(nb, 1, 2 * KV_WIDTH), kvw.reshape(nb, 1, 2 * KV_WIDTH),
                   rows_on_lanes(cache_slc), pages_per_step=16, qpos=qpos, past=past)
    return o.reshape(nb, ATTN_WIDTH), dpool, kvc, kvs, kvw, u


def _prep_params(norm1_g, w_in, q_gain, k_cmp_gain, k_slc_gain, k_win_gain, w_cmp_k, b_cmp_k, w_cmp_v, b_cmp_v):
    cw0, cw1, cb = _cmp_weights(w_cmp_k, b_cmp_k, w_cmp_v, b_cmp_v)
    return {
        "g1": norm1_g[None, :],
        "w_in": _pack_w_in(w_in),
        "qg": jnp.tile(q_gain, N_HEADS)[None, :],
        "ksg": jnp.tile(k_slc_gain, KV_HEADS)[None, :],
        "kwg": jnp.tile(k_win_gain, KV_HEADS)[None, :],
        "kcg": jnp.tile(k_cmp_gain, KV_HEADS)[None, :],
        "cw0": cw0, "cw1": cw1, "cb": cb,
    }


def _prep_out_params(w_pm, s_pm, w_out, norm2_g, w_router, b_router):
    d_model = w_out.shape[0]
    woa = w_out[:ATTN_WIDTH].reshape(N_HEADS, HEAD_DIM, d_model)[jnp.array(_head_perm())].reshape(ATTN_WIDTH, d_model)
    wr = jnp.pad(w_router, ((0, 0), (0, LANES - w_router.shape[1])))
    wrh, wrl = _split_bf16(wr)
    return (w_pm.astype(BF16), s_pm.reshape(1, -1), woa.astype(BF16), w_out[ATTN_WIDTH:].astype(BF16),
            norm2_g[None, :], wrh, wrl, jnp.pad(b_router, (0, LANES - b_router.shape[0]))[None, :])


def _layer(x_prompt, x_sample, cache_cmp, cache_slc, state_win, state_pool, page_table,
           norm1_g, w_in, q_gain, k_cmp_gain, k_slc_gain, k_win_gain, w_cmp_k, b_cmp_k, w_cmp_v, b_cmp_v,
           w_pm, s_pm, w_out, norm2_g, w_router, b_router, w_gu, b_gu, w_down, b_down):
    batch, seq, d_model = x_prompt.shape
    nb, t = x_sample.shape[:2]
    p = _prep_params(norm1_g, w_in, q_gain, k_cmp_gain, k_slc_gain, k_win_gain, w_cmp_k, b_cmp_k, w_cmp_v, b_cmp_v)
    oa_p, d_p, kvc_p, kvs_p, kvw_p, u_p = _prompt_mixer(x_prompt, p)
    oa_s, d_s, kvc_s, kvs_s, kvw_s, u_s = _sample_mixer(x_sample, cache_cmp, cache_slc, state_win, state_pool, page_table, p)
    op = _prep_out_params(w_pm, s_pm, w_out, norm2_g, w_router, b_router)
    x1_p, h2_p, te_p, gt_p = _outproj(x_prompt.reshape(batch * seq, d_model), oa_p, d_p, *op, tm=256)
    x1_s, h2_s, te_s, gt_s = _outproj(x_sample.reshape(nb * t, d_model), oa_s, d_s, *op, tm=nb * t)
    y_p, y_s = _moe(h2_p, h2_s, te_p, te_s, gt_p, gt_s, x1_p, x1_s, w_gu, b_gu, w_down, b_down)
    kv_shape = (2, KV_HEADS, HEAD_DIM)
    n_win = min(WINDOW, seq)
    st_p = (kvc_p.reshape(batch, seq, *kv_shape), kvs_p.reshape(batch, seq, *kv_shape),
            kvw_p.reshape(batch, seq, *kv_shape)[:, seq - n_win:], u_p.reshape(batch, seq, -1)[:, seq - POOL_HIST:])
    kvw_s5 = kvw_s.reshape(nb, t, *kv_shape)
    u_s3 = u_s.reshape(nb, t, -1)
    st_s = (kvc_s.reshape(nb, t, *kv_shape), kvs_s.reshape(nb, t, *kv_shape),
            jnp.concatenate([state_win, kvw_s5], axis=1)[:, t:], jnp.concatenate([state_pool, u_s3], axis=1)[:, t:])
    return y_p.reshape(batch, seq, d_model), y_s.reshape(nb, t, d_model), st_p, st_s


def kernel(x_prompt, x_sample, cache_cmp_kv, cache_slc_kv, state_win_kv, state_pool, page_table, norm1_g, w_in, q_gain, k_cmp_gain, k_slc_gain, k_win_gain, w_cmp_k, b_cmp_k, w_cmp_v, b_cmp_v, w_pm, s_pm, w_out, norm2_g, w_router, b_router, w_gu, b_gu, w_down, b_down):
    layer_params = (norm1_g, w_in, q_gain, k_cmp_gain, k_slc_gain, k_win_gain, w_cmp_k, b_cmp_k, w_cmp_v, b_cmp_v,
                    w_pm, s_pm, w_out, norm2_g, w_router, b_router, w_gu, b_gu, w_down, b_down)
    y_p, y_s = x_prompt, x_sample
    p_states, s_states = [], []
    for layer in range(norm1_g.shape[0]):
        lw = [w[layer] for w in layer_params]
        y_p, y_s, st_p, st_s = _layer(y_p, y_s, cache_cmp_kv[layer], cache_slc_kv[layer], state_win_kv[layer],
                                      state_pool[layer], page_table, *lw)
        p_states.append(st_p)
        s_states.append(st_s)
    stack = lambda states, i: jnp.stack([s[i] for s in states], axis=0)
    return (y_p, y_s,
            stack(p_states, 0), stack(p_states, 1), stack(p_states, 2), stack(p_states, 3),
            stack(s_states, 0), stack(s_states, 1), stack(s_states, 2), stack(s_states, 3))
```

```python
import functools
import math

import jax
import jax.numpy as jnp
from jax import lax
from jax.experimental import pallas as pl
from jax.experimental.pallas import tpu as pltpu

F32 = jnp.float32
BF16 = jnp.bfloat16
I32 = jnp.int32

N_HEADS = 16
HEAD_DIM = 64
KV_HEADS = 4
Q_PER_KV = N_HEADS // KV_HEADS
ATTN_WIDTH = N_HEADS * HEAD_DIM
KV_WIDTH = KV_HEADS * HEAD_DIM
CMP_LEN = 32
CMP_STRIDE = 16
SLC_BLOCK = 64
N_SELECT = 16
WINDOW = 512
FORCE_SCORE = 1e4
NEG_INF = -1e30
POOL_WINDOWS = (2, 4, 8, 16)
MAX_POOL_W = max(POOL_WINDOWS)
POOL_HIST = MAX_POOL_W - 1
N_EXPERTS = 32
TOP_K = 4
SWIGLU_LIMIT = 7.0
SWIGLU_ALPHA = 1.702
EPS = 1e-6
PAGE_SIZE = 128

LANES = 128
VMEM_LIMIT = 56 * 1024 * 1024

GATE_PAD = LANES
SLOPES = [2.0 ** (-8.0 * (h + 1) / N_HEADS) for h in range(N_HEADS)]
LOG2E = math.log2(math.e)


def _cparams(sem):
    return pltpu.CompilerParams(dimension_semantics=sem, vmem_limit_bytes=VMEM_LIMIT)


def _iota(shape, dim):
    return lax.broadcasted_iota(I32, shape, dim)


def _split_bf16(x):
    hi = x.astype(BF16)
    lo = (x - hi.astype(F32)).astype(BF16)
    return hi, lo


def _dot(a, b):
    return jnp.dot(a, b, preferred_element_type=F32)


def _dot_t(a, b):
    return lax.dot_general(a, b, (((1,), (1,)), ((), ())), preferred_element_type=F32)


def _head_mean_sq(z):
    m, w = z.shape
    ones_bd = jnp.where(_iota((256, 256), 0) // HEAD_DIM == _iota((256, 256), 1) // HEAD_DIM, 1.0, 0.0).astype(BF16)
    zz = z * z
    hi, lo = _split_bf16(zz)
    parts = []
    for c in range(w // 256):
        sl = slice(c * 256, (c + 1) * 256)
        parts.append(_dot(hi[:, sl], ones_bd) + _dot(lo[:, sl], ones_bd))
    ss = parts[0] if len(parts) == 1 else jnp.concatenate(parts, axis=1)
    return ss * (1.0 / HEAD_DIM)


C_Q = 0
C_KVC = ATTN_WIDTH
C_KVS = C_KVC + 2 * KV_WIDTH
C_KVW = C_KVS + 2 * KV_WIDTH
C_U = C_KVW + 2 * KV_WIDTH


def _inproj_kernel(x_ref, g1_ref, w_ref, qg_ref, ksg_ref, kwg_ref, *refs, tm, pm_width, seq_tiles, hist_pos):
    if seq_tiles is None:
        sp_ref, refs = refs[0], refs[1:]
    q_out, kvc_out, kvs_out, kvw_out, gate_out, u_out, d_out = refs[:7]
    x = x_ref[...]
    ms = jnp.mean(x * x, axis=-1, keepdims=True)
    h = (x * lax.rsqrt(ms + EPS)) * g1_ref[...]
    hb = h.astype(BF16)
    c_gate = C_U + pm_width

    zq = _dot(hb, w_ref[:, C_Q:C_KVC])
    qn = (zq * lax.rsqrt(_head_mean_sq(zq) + EPS)) * qg_ref[...]
    q_out[...] = (qn * (HEAD_DIM ** -0.5 * LOG2E)).astype(BF16)

    kvc_out[...] = _dot(hb, w_ref[:, C_KVC:C_KVS])

    zs = _dot(hb, w_ref[:, C_KVS:C_KVW])
    ks = zs[:, :KV_WIDTH]
    ks = (ks * lax.rsqrt(_head_mean_sq(ks) + EPS)) * ksg_ref[...]
    kvs = jnp.concatenate([ks, zs[:, KV_WIDTH:]], axis=1)
    kvs_out[...] = kvs

    zw = _dot(hb, w_ref[:, C_KVW:C_U])
    kw = zw[:, :KV_WIDTH]
    kw = (kw * lax.rsqrt(_head_mean_sq(kw) + EPS)) * kwg_ref[...]
    kvw = jnp.concatenate([kw, zw[:, KV_WIDTH:]], axis=1)
    kvw_out[...] = kvw
    if seq_tiles is not None:
        ksb_out, vst_out, kwb_out, vwt_out = refs[7:11]
        ksb_out[...] = ks.astype(BF16)
        vst_out[...] = zs[:, KV_WIDTH:].T.astype(BF16)
        kwb_out[...] = kw.astype(BF16)
        vw_t = zw[:, KV_WIDTH:].T.astype(BF16)
        for j in range(tm // TKW):
            vwt_out[j] = vw_t[:, j * TKW:(j + 1) * TKW]

    gate_out[...] = jax.nn.sigmoid(_dot(hb, w_ref[:, c_gate:c_gate + GATE_PAD]))

    u = _dot(hb, w_ref[:, C_U:c_gate])
    u_out[...] = u

    pm_group = pm_width // len(POOL_WINDOWS)
    if seq_tiles is None:
        tpos = float(hist_pos + 1)
        for gi, w in enumerate(POOL_WINDOWS):
            cs = slice(gi * pm_group, (gi + 1) * pm_group)
            s = u[:, cs]
            for k in range(1, w):
                s = s + sp_ref[:, POOL_HIST - k, cs]
            d_out[:, cs] = (s / min(float(w), tpos) - u[:, cs]).astype(BF16)
    else:
        ext_ref = refs[11]
        j = pl.program_id(0) % seq_tiles

        @pl.when(j == 0)
        def _():
            ext_ref[0:MAX_POOL_W, :] = jnp.zeros((MAX_POOL_W, pm_width), F32)

        ext_ref[MAX_POOL_W:MAX_POOL_W + tm, :] = u
        tpos = (j * tm + _iota((tm, 1), 0) + 1).astype(F32)
        for gi, w in enumerate(POOL_WINDOWS):
            cs = slice(gi * pm_group, (gi + 1) * pm_group)
            s = ext_ref[MAX_POOL_W:MAX_POOL_W + tm, cs]
            for k in range(1, w):
                s = s + ext_ref[MAX_POOL_W - k:MAX_POOL_W - k + tm, cs]
            cnt = jnp.minimum(float(w), tpos)
            d_out[:, cs] = (s / cnt - u[:, cs]).astype(BF16)
        ext_ref[0:MAX_POOL_W, :] = ext_ref[tm:tm + MAX_POOL_W, :]


def _inproj(x2d, g1, w_packed, qg, ksg, kwg, *, tm, seq_len=None, pool_state=None, hist_pos=None):
    n, d_model = x2d.shape
    pm_width = d_model - ATTN_WIDTH
    seq_tiles = None if seq_len is None else seq_len // tm
    row = lambda w: pl.BlockSpec((tm, w), lambda i: (i, 0))
    full = lambda a: pl.BlockSpec(a.shape, lambda i: (0,) * a.ndim, pipeline_mode=pl.Buffered(1))
    out_shape = [
        jax.ShapeDtypeStruct((n, ATTN_WIDTH), BF16),
        jax.ShapeDtypeStruct((n, 2 * KV_WIDTH), F32),
        jax.ShapeDtypeStruct((n, 2 * KV_WIDTH), F32),
        jax.ShapeDtypeStruct((n, 2 * KV_WIDTH), F32),
        jax.ShapeDtypeStruct((n, GATE_PAD), F32),
        jax.ShapeDtypeStruct((n, pm_width), F32),
        jax.ShapeDtypeStruct((n, pm_width), BF16),
    ]
    out_specs = [row(ATTN_WIDTH), row(2 * KV_WIDTH), row(2 * KV_WIDTH), row(2 * KV_WIDTH),
                 row(GATE_PAD), row(pm_width), row(pm_width)]
    in_specs = [row(d_model), full(g1), full(w_packed), full(qg), full(ksg), full(kwg)]
    args = [x2d, g1, w_packed, qg, ksg, kwg]
    scratch = []
    if seq_tiles is None:
        in_specs.append(pl.BlockSpec((tm, POOL_HIST, pm_width), lambda i: (i, 0, 0)))
        args.append(pool_state)
    else:
        per_tile = TK // tm
        out_shape += [jax.ShapeDtypeStruct((n, KV_WIDTH), BF16), jax.ShapeDtypeStruct((n // TK, KV_WIDTH, TK), BF16),
                      jax.ShapeDtypeStruct((n, KV_WIDTH), BF16), jax.ShapeDtypeStruct((n // TKW, KV_WIDTH, TKW), BF16)]
        out_specs += [row(KV_WIDTH), pl.BlockSpec((None, KV_WIDTH, tm), lambda i: (i // per_tile, 0, i % per_tile)),
                      row(KV_WIDTH), pl.BlockSpec((tm // TKW, KV_WIDTH, TKW), lambda i: (i, 0, 0))]
        scratch.append(pltpu.VMEM((tm + MAX_POOL_W, pm_width), F32))
    return pl.pallas_call(
        functools.partial(_inproj_kernel, tm=tm, pm_width=pm_width, seq_tiles=seq_tiles, hist_pos=hist_pos),
        grid=(n // tm,),
        in_specs=in_specs,
        out_specs=out_specs,
        out_shape=out_shape,
        scratch_shapes=scratch,
        compiler_params=_cparams(("arbitrary",)),
        name="inproj",
    )(*args)


CMP_SUB_PER_PAGE = PAGE_SIZE // CMP_STRIDE


def _compress_kernel(pt_ref, *refs, pages_per_step):
    del pt_ref
    p = pages_per_step
    page_refs = refs[:p]
    halo_ref, w0_ref, w1_ref, b_ref, kg_ref, kc_out, vc_out, xs_ref = refs[p:]
    n_lt = 2 * KV_WIDTH // LANES
    nb = p * CMP_SUB_PER_PAGE
    parts = []
    for c in range(n_lt):
        cs = slice(c * LANES, (c + 1) * LANES)
        for k in range(p):
            xs_ref[c, k * PAGE_SIZE:(k + 1) * PAGE_SIZE, :] = page_refs[k][:, cs]
        xs_ref[c, p * PAGE_SIZE:p * PAGE_SIZE + CMP_STRIDE, :] = halo_ref[:, cs]
        acc = jnp.zeros((nb, LANES), F32) + b_ref[:, cs]
        for j in range(CMP_STRIDE):
            acc = acc + xs_ref[c, pl.ds(j, nb, stride=CMP_STRIDE), :] * w0_ref[j:j + 1, cs]
            acc = acc + xs_ref[c, pl.ds(CMP_STRIDE + j, nb, stride=CMP_STRIDE), :] * w1_ref[j:j + 1, cs]
        parts.append(acc)
    acc = jnp.concatenate(parts, axis=1)
    kc = acc[:, :KV_WIDTH]
    kc = (kc * lax.rsqrt(_head_mean_sq(kc) + EPS)) * kg_ref[...]
    kc_out[...] = kc.astype(BF16)
    vc_out[...] = acc[:, KV_WIDTH:].astype(BF16)


def _compress(page_table, pages, w0t, w1t, bias, kgain, *, pages_per_step):
    nb, npg = page_table.shape
    p = pages_per_step
    steps = npg // p

    def page_spec(k):
        return pl.BlockSpec((None, PAGE_SIZE, 2 * KV_WIDTH), lambda b, i, pt: (pt[b, i * p + k], 0, 0))

    halo_spec = pl.BlockSpec((None, CMP_STRIDE, 2 * KV_WIDTH),
                             lambda b, i, pt: (pt[b, jnp.minimum(i * p + p, npg - 1)], 0, 0))
    full = lambda a: pl.BlockSpec(a.shape, lambda b, i, pt: (0,) * a.ndim)
    out_spec = pl.BlockSpec((None, p * CMP_SUB_PER_PAGE, KV_WIDTH), lambda b, i, pt: (b, i, 0))
    grid_spec = pltpu.PrefetchScalarGridSpec(
        num_scalar_prefetch=1,
        grid=(nb, steps),
        in_specs=[page_spec(k) for k in range(p)] + [halo_spec, full(w0t), full(w1t), full(bias), full(kgain)],
        out_specs=[out_spec, out_spec],
        scratch_shapes=[pltpu.VMEM((2 * KV_WIDTH // LANES, p * PAGE_SIZE + CMP_STRIDE, LANES), F32)],
    )
    return pl.pallas_call(
        functools.partial(_compress_kernel, pages_per_step=p),
        grid_spec=grid_spec,
        out_shape=[jax.ShapeDtypeStruct((nb, npg * CMP_SUB_PER_PAGE, KV_WIDTH), BF16)] * 2,
        compiler_params=_cparams(("arbitrary", "arbitrary")),
        name="compress",
    )(page_table, *([pages] * p), pages, w0t, w1t, bias, kgain)


TQ = 128
TK = 512
TKW = 128
QROWS = Q_PER_KV * TQ


def _select_members_t(score_t, n_cand):
    rows = _iota((n_cand, 1), 0)
    rank = jnp.zeros(score_t.shape, F32)
    for i in range(n_cand):
        si = score_t[i:i + 1, :]
        ahead = jnp.where(si > score_t, 1.0, jnp.where(si == score_t, jnp.where(rows > i, 1.0, 0.0), 0.0))
        rank = rank + ahead
    return jnp.where(rank < float(N_SELECT), 1.0, 0.0)


def _prompt_attn_kernel(q_ref, gate_ref, kc_ref, vct_ref, ks_ref, vst_ref, kw_ref, vwt_ref, ct_ref, ext_ref,
                        o_ref, qpad_ref, memb_ref, tot_ref, *stat_refs, n_slc):
    qi = pl.program_id(1)
    q0 = qi * TQ
    lane = _iota((1, QROWS), 1)
    r_lane = lane // TQ
    qidx = q0 + lane % TQ
    qpos_l = q0 + _iota((1, TQ), 1)
    lane_g = _iota((1, KV_WIDTH), 1) // HEAD_DIM
    n_cmp_pad = kc_ref.shape[0]
    n_kt = memb_ref.shape[1]
    gates_t = gate_ref[...].T

    def slope_row(g):
        row = jnp.zeros((1, QROWS), F32)
        for r in range(Q_PER_KV):
            row = jnp.where(r_lane == r, SLOPES[g * Q_PER_KV + r] * LOG2E, row)
        return row

    def gate_row(g, k):
        c = k * N_HEADS + g * Q_PER_KV
        return jnp.concatenate([gates_t[c + r:c + r + 1, :] for r in range(Q_PER_KV)], axis=1)

    c_start = _iota((n_cmp_pad, QROWS), 0) * CMP_STRIDE
    c_mid = c_start.astype(F32) + 0.5 * (CMP_LEN - 1)
    cmask = (c_start + (CMP_LEN - 1)) <= qidx
    kidx = _iota((TK, TQ), 0)

    for g in range(KV_HEADS):
        qpad = jnp.concatenate(
            [jnp.where(lane_g == g, q_ref[:, r * KV_WIDTH:(r + 1) * KV_WIDTH], jnp.zeros((), BF16))
             for r in range(Q_PER_KV)], axis=0)
        qpad_ref[g] = qpad
        s = _dot_t(kc_ref[...], qpad) + slope_row(g) * c_mid
        s = jnp.where(cmask, s, NEG_INF)
        e = jnp.where(cmask, jnp.exp2(s - jnp.max(s, axis=0, keepdims=True)), 0.0)
        l = jnp.sum(e, axis=0, keepdims=True)
        pc = e * jnp.where(l > 0.0, 1.0 / l, 0.0)
        oc = _dot(vct_ref[g * HEAD_DIM:(g + 1) * HEAD_DIM, :], pc.astype(BF16))
        tot_ref[g] = oc * gate_row(g, 0)
        p_sum = pc[:, 0:TQ]
        for r in range(1, Q_PER_KV):
            p_sum = p_sum + pc[:, r * TQ:(r + 1) * TQ]
        hi, lo = _split_bf16(p_sum)
        imp_t = (_dot(ct_ref[...], hi) + _dot(ct_ref[...], lo))[0:n_slc]
        blk = _iota((n_slc, 1), 0)
        qblk = qpos_l // SLC_BLOCK
        forced = (blk == 0) | (blk == qblk) | (blk == qblk - 1)
        score_t = jnp.where(forced, FORCE_SCORE, jnp.where(blk * SLC_BLOCK <= qpos_l, imp_t, NEG_INF))
        member_t = _select_members_t(score_t, n_slc)
        member_t = jnp.concatenate([member_t, jnp.zeros((LANES - n_slc, TQ), F32)], axis=0).astype(BF16)
        memb_keys = _dot(ext_ref[...], member_t)
        for j in range(n_kt):
            ok = jnp.where(kidx + j * TK <= qpos_l, memb_keys[j * TK:(j + 1) * TK, :], 0.0)
            memb_ref[g, j] = jnp.where(ok > 0.5, 0.0, NEG_INF)

    m_refs, l_refs, acc_refs = (stat_refs[i * KV_HEADS:(i + 1) * KV_HEADS] for i in range(3))

    def sweep(k_ref, vt_ref, lo_t, hi_t, shared_fn, mask_fn, gate_k):
        for g in range(KV_HEADS):
            m_refs[g][...] = jnp.full(m_refs[g].shape, NEG_INF, F32)
            l_refs[g][...] = jnp.zeros(l_refs[g].shape, F32)
            acc_refs[g][...] = jnp.zeros(acc_refs[g].shape, F32)

        def body(kj, carry):
            k0 = pl.multiple_of(kj * TK, TK)
            kt = k_ref[pl.ds(k0, TK), :]
            vt = vt_ref[kj]
            kpos = (kidx + k0).astype(F32)
            shared = shared_fn(k0)
            for g in range(KV_HEADS):
                s_all = _dot_t(kt, qpad_ref[g])
                mask = mask_fn(g, kj, shared)
                vg = vt[g * HEAD_DIM:(g + 1) * HEAD_DIM, :]
                for r in range(Q_PER_KV):
                    cs = slice(r * TQ, (r + 1) * TQ)
                    s = s_all[:, cs] + (SLOPES[g * Q_PER_KV + r] * LOG2E) * kpos + mask
                    m_old = m_refs[g][:, cs]
                    m_new = jnp.maximum(m_old, jnp.max(s, axis=0, keepdims=True))
                    alpha = jnp.exp2(m_old - m_new)
                    p = jnp.exp2(s - m_new)
                    l_refs[g][:, cs] = alpha * l_refs[g][:, cs] + jnp.sum(p, axis=0, keepdims=True)
                    acc_refs[g][:, cs] = alpha * acc_refs[g][:, cs] + _dot(vg, p.astype(BF16))
                    m_refs[g][:, cs] = m_new
            return carry

        lax.fori_loop(lo_t, hi_t, body, 0)
        for g in range(KV_HEADS):
            l = l_refs[g][...]
            tot_ref[g] = tot_ref[g] + acc_refs[g][...] * (jnp.where(l > 0.0, 1.0 / l, 0.0) * gate_row(g, gate_k))

    n_hi = (q0 + TQ + TK - 1) // TK
    sweep(ks_ref, vst_ref, 0, n_hi, lambda k0: None, lambda g, kj, shared: memb_ref[g, kj], 1)

    n_wt = (WINDOW + TQ) // TKW
    nwk = n_wt * TKW
    j0 = jnp.maximum(q0 - WINDOW, 0) // TKW
    w0 = pl.multiple_of(j0 * TKW, TKW)
    kt = kw_ref[pl.ds(w0, nwk), :]
    vt = jnp.concatenate([vwt_ref[j0 + i] for i in range(n_wt)], axis=1)
    d = _iota((nwk, TQ), 0) - _iota((nwk, TQ), 1) + (w0 - q0)
    wmask = jnp.where(d <= 0, jnp.where(d > -WINDOW, 0.0, NEG_INF), NEG_INF)
    wpos = (_iota((nwk, TQ), 0) + w0).astype(F32)
    for g in range(KV_HEADS):
        s_all = _dot_t(kt, qpad_ref[g])
        vg = vt[g * HEAD_DIM:(g + 1) * HEAD_DIM, :]
        for r in range(Q_PER_KV):
            cs = slice(r * TQ, (r + 1) * TQ)
            s = s_all[:, cs] + (SLOPES[g * Q_PER_KV + r] * LOG2E) * wpos + wmask
            e = jnp.exp2(s - jnp.max(s, axis=0, keepdims=True))
            scale = gate_row(g, 2)[:, cs] / jnp.sum(e, axis=0, keepdims=True)
            tot_ref[g, :, cs] = tot_ref[g, :, cs] + _dot(vg, e.astype(BF16)) * scale

    total = jnp.concatenate([tot_ref[g] for g in range(KV_HEADS)], axis=0)
    for r in range(Q_PER_KV):
        o_ref[:, r * KV_WIDTH:(r + 1) * KV_WIDTH] = total[:, r * TQ:(r + 1) * TQ].T.astype(BF16)


def _prompt_attn(q, gate, kc, vct, kvs_b, vst, kvw_b, vwt, ct, ext, *, batch, seq):
    n_slc = -(-seq // SLC_BLOCK)
    n_cmp_pad = kc.shape[1]
    n_qt = seq // TQ
    n_kt = seq // TK
    rowblk = lambda w: pl.BlockSpec((TQ, w), lambda b, i: (b * n_qt + i, 0))
    kblk = pl.BlockSpec((seq, KV_WIDTH), lambda b, i: (b, 0))
    assert seq >= WINDOW + TQ
    vtblk = pl.BlockSpec((n_kt, KV_WIDTH, TK), lambda b, i: (b, 0, 0))
    vwblk = pl.BlockSpec((seq // TKW, KV_WIDTH, TKW), lambda b, i: (b, 0, 0))
    full = lambda a: pl.BlockSpec(a.shape, lambda b, i: (0,) * a.ndim)
    return pl.pallas_call(
        functools.partial(_prompt_attn_kernel, n_slc=n_slc),
        grid=(batch, n_qt),
        in_specs=[rowblk(ATTN_WIDTH), rowblk(GATE_PAD),
                  pl.BlockSpec((None, n_cmp_pad, KV_WIDTH), lambda b, i: (b, 0, 0)),
                  pl.BlockSpec((None, KV_WIDTH, n_cmp_pad), lambda b, i: (b, 0, 0)),
                  kblk, vtblk, kblk, vwblk, full(ct), full(ext)],
        out_specs=rowblk(ATTN_WIDTH),
        out_shape=jax.ShapeDtypeStruct((batch * seq, ATTN_WIDTH), BF16),
        scratch_shapes=[pltpu.VMEM((KV_HEADS, QROWS, KV_WIDTH), BF16),
                        pltpu.VMEM((KV_HEADS, n_kt, TK, TQ), F32),
                        pltpu.VMEM((KV_HEADS, HEAD_DIM, QROWS), F32)]
                       + [pltpu.VMEM((1, QROWS), F32)] * (2 * KV_HEADS) + [pltpu.VMEM((HEAD_DIM, QROWS), F32)] * KV_HEADS,
        compiler_params=_cparams(("arbitrary", "arbitrary")),
        name="prompt_attn",
    )(q, gate, kc, vct, kvs_b, vst, kvw_b, vwt, ct, ext)


DROWS = N_HEADS
NEVER = -3e38


def _decode_qpad(q_ref):
    lane_g = _iota((KV_HEADS, KV_WIDTH), 1) // HEAD_DIM
    row_g = _iota((KV_HEADS, KV_WIDTH), 0)
    parts = []
    for r in range(Q_PER_KV):
        qr = jnp.broadcast_to(q_ref[:, r * KV_WIDTH:(r + 1) * KV_WIDTH].astype(F32), (KV_HEADS, KV_WIDTH))
        parts.append(jnp.where(lane_g == row_g, qr, 0.0))
    return jnp.concatenate(parts, axis=0).astype(BF16)


def _compress_t_kernel(pt_ref, *refs, pages_per_step):
    del pt_ref
    p = pages_per_step
    page_refs = refs[:p]
    w0_ref, w1_ref, a0_out, a1_out = refs[p:]
    prow = _iota((PAGE_SIZE, LANES), 0)
    ocol = _iota((PAGE_SIZE, LANES), 1)
    a0 = jnp.zeros(a0_out.shape, F32)
    a1 = jnp.zeros(a1_out.shape, F32)
    for k in range(p):
        sel = jnp.where(ocol == k * CMP_SUB_PER_PAGE + prow // CMP_STRIDE, 1.0, 0.0).astype(BF16)
        x = page_refs[k][...]
        a0 = a0 + _dot((x * w0_ref[...]).astype(BF16), sel)
        a1 = a1 + _dot((x * w1_ref[...]).astype(BF16), sel)
    a0_out[...] = a0
    a1_out[...] = a1


def _compress_t(page_table, pages_t, w0t, w1t, *, pages_per_step):
    nb, npg = page_table.shape
    p = pages_per_step
    assert p * CMP_SUB_PER_PAGE == LANES

    def page_spec(k):
        return pl.BlockSpec((None, 2 * KV_WIDTH, PAGE_SIZE), lambda b, i, pt: (pt[b, i * p + k], 0, 0))

    full = lambda a: pl.BlockSpec(a.shape, lambda b, i, pt: (0,) * a.ndim)
    out_spec = pl.BlockSpec((None, 2 * KV_WIDTH, LANES), lambda b, i, pt: (b, 0, i))
    grid_spec = pltpu.PrefetchScalarGridSpec(
        num_scalar_prefetch=1,
        grid=(nb, npg // p),
        in_specs=[page_spec(k) for k in range(p)] + [full(w0t), full(w1t)],
        out_specs=[out_spec, out_spec],
    )
    return pl.pallas_call(
        functools.partial(_compress_t_kernel, pages_per_step=p),
        grid_spec=grid_spec,
        out_shape=[jax.ShapeDtypeStruct((nb, 2 * KV_WIDTH, npg * CMP_SUB_PER_PAGE), F32)] * 2,
        compiler_params=_cparams(("arbitrary", "arbitrary")),
        name="compress_t",
    )(page_table, *([pages_t] * p), w0t, w1t)


def _decode_cmp_kernel(q_ref, slope_ref, a0_ref, a1_ref, cb_ref, kg_ref, c_ref, oc_out, member_out, *, qpos, n_slc):
    qpad = _decode_qpad(q_ref)
    slope = slope_ref[...]
    n_cmp_pad = a0_ref.shape[1]
    acc = a0_ref[...] + pltpu.roll(a1_ref[...], n_cmp_pad - 1, 1) + cb_ref[...]
    kparts = []
    for g in range(KV_HEADS):
        kg = acc[g * HEAD_DIM:(g + 1) * HEAD_DIM, :]
        ms = jnp.mean(kg * kg, axis=0, keepdims=True)
        kparts.append((kg * lax.rsqrt(ms + EPS)) * kg_ref[...])
    kc_t = jnp.concatenate(kparts, axis=0).astype(BF16)
    vc_t = acc[KV_WIDTH:, :].astype(BF16)
    c_start = _iota((1, n_cmp_pad), 1) * CMP_STRIDE
    c_mid = c_start.astype(F32) + 0.5 * (CMP_LEN - 1)
    cmask = (c_start + (CMP_LEN - 1)) <= qpos
    s = _dot(qpad, kc_t) + slope * c_mid
    s = jnp.where(cmask, s, NEG_INF)
    e = jnp.where(cmask, jnp.exp2(s - jnp.max(s, axis=-1, keepdims=True)), 0.0)
    l = jnp.sum(e, axis=-1, keepdims=True)
    pc = e * jnp.where(l > 0.0, 1.0 / l, 0.0)
    oc_out[...] = _dot_t(pc.astype(BF16), vc_t)
    p_sum = pc[0:KV_HEADS]
    for r in range(1, Q_PER_KV):
        p_sum = p_sum + pc[r * KV_HEADS:(r + 1) * KV_HEADS]
    p_sum = jnp.concatenate([p_sum, jnp.zeros((DROWS - KV_HEADS, n_cmp_pad), F32)], axis=0)
    hi, lo = _split_bf16(p_sum)
    imp = _dot(hi, c_ref[...]) + _dot(lo, c_ref[...])
    n_pad = imp.shape[1]
    blk = _iota((1, n_pad), 1)
    qblk = qpos // SLC_BLOCK
    forced = (blk == 0) | (blk == qblk) | (blk == qblk - 1)
    score = jnp.where(forced, FORCE_SCORE, jnp.where(blk * SLC_BLOCK <= qpos, imp, NEG_INF))
    score = jnp.where(blk < n_slc, score, NEVER)
    rank = jnp.zeros(score.shape, F32)
    for i in range(n_slc):
        si = score[:, i:i + 1]
        rank = rank + jnp.where(si > score, 1.0, jnp.where(si == score, jnp.where(blk > i, 1.0, 0.0), 0.0))
    member_out[...] = jnp.where(rank < float(min(N_SELECT, n_slc)), 1.0, 0.0)


def _decode_cmp(q3, slopes, a0, a1, cb_col, kg_col, cmat, *, qpos, n_slc):
    nb = q3.shape[0]
    n_cmp_pad = a0.shape[2]
    n_pad = cmat.shape[1]
    per_b = lambda r, c: pl.BlockSpec((None, r, c), lambda b: (b, 0, 0))
    full = lambda a: pl.BlockSpec(a.shape, lambda b: (0,) * a.ndim)
    return pl.pallas_call(
        functools.partial(_decode_cmp_kernel, qpos=qpos, n_slc=n_slc),
        grid=(nb,),
        in_specs=[per_b(1, ATTN_WIDTH), full(slopes), per_b(2 * KV_WIDTH, n_cmp_pad), per_b(2 * KV_WIDTH, n_cmp_pad),
                  full(cb_col), full(kg_col), full(cmat)],
        out_specs=[per_b(DROWS, KV_WIDTH), per_b(DROWS, n_pad)],
        out_shape=[jax.ShapeDtypeStruct((nb, DROWS, KV_WIDTH), F32), jax.ShapeDtypeStruct((nb, DROWS, n_pad), F32)],
        compiler_params=_cparams(("arbitrary",)),
        name="decode_cmp",
    )(q3, slopes, a0, a1, cb_col, kg_col, cmat)


def _decode_sw_kernel(pt_ref, q_ref, slope_ref, oc_ref, member_ref, gate_ref, gexp_ref, win_ref, news_ref, neww_ref,
                      *refs, pages_per_step, qpos, past, n_buf):
    del pt_ref
    p = pages_per_step
    page_refs = refs[:p]
    o_ref, m_ref, l_ref, acc_ref, ow_ref = refs[p:]
    c = pl.program_id(1)
    qpad = _decode_qpad(q_ref)
    qf = qpad.astype(F32)
    slope = slope_ref[...]

    def new_row(row_ref):
        kn = row_ref[:, :KV_WIDTH].astype(BF16).astype(F32)
        vn = row_ref[:, KV_WIDTH:].astype(BF16).astype(F32)
        return jnp.sum(qf * kn, axis=-1, keepdims=True) + slope * float(qpos), vn

    @pl.when(c == 0)
    def _():
        m_ref[...] = jnp.full((DROWS, 1), NEG_INF, F32)
        l_ref[...] = jnp.zeros((DROWS, 1), F32)
        acc_ref[...] = jnp.zeros((DROWS, KV_WIDTH), F32)
        kw_t = win_ref[:KV_WIDTH, :].astype(BF16)
        vw_t = win_ref[KV_WIDTH:, :].astype(BF16)
        wpos = past - n_buf + _iota((1, n_buf), 1)
        ok = (wpos <= qpos) & (wpos > qpos - WINDOW) & (wpos >= 0)
        s = jnp.where(ok, _dot(qpad, kw_t) + slope * wpos.astype(F32), NEG_INF)
        s_new, v_new = new_row(neww_ref)
        m = jnp.maximum(jnp.max(s, axis=-1, keepdims=True), s_new)
        e = jnp.where(ok, jnp.exp2(s - m), 0.0)
        e_new = jnp.exp2(s_new - m)
        l = jnp.sum(e, axis=-1, keepdims=True) + e_new
        ow_ref[...] = (_dot_t(e.astype(BF16), vw_t) + e_new * v_new) / l

    nk = p * PAGE_SIZE
    kt_t = jnp.concatenate([page_refs[k][:KV_WIDTH, :].astype(BF16) for k in range(p)], axis=1)
    vt_t = jnp.concatenate([page_refs[k][KV_WIDTH:, :].astype(BF16) for k in range(p)], axis=1)
    member = member_ref[0:KV_HEADS, :].astype(BF16)
    member = jnp.concatenate([member] * Q_PER_KV, axis=0)
    n_pad = member.shape[1]
    kidx = c * nk + _iota((n_pad, nk), 1)
    expand = jnp.where(kidx // SLC_BLOCK == _iota((n_pad, nk), 0), 1.0, 0.0).astype(BF16)
    mk = _dot(member, expand)
    kpos = (c * nk + _iota((1, nk), 1)).astype(F32)
    s = jnp.where(mk > 0.5, _dot(qpad, kt_t) + slope * kpos, NEG_INF)
    m_old = m_ref[...]
    m_new = jnp.maximum(m_old, jnp.max(s, axis=-1, keepdims=True))
    alpha = jnp.exp2(m_old - m_new)
    pr = jnp.exp2(s - m_new)
    l_ref[...] = alpha * l_ref[...] + jnp.sum(pr, axis=-1, keepdims=True)
    acc_ref[...] = alpha * acc_ref[...] + _dot_t(pr.astype(BF16), vt_t)
    m_ref[...] = m_new

    @pl.when(c == pl.num_programs(1) - 1)
    def _():
        new_blk = qpos // SLC_BLOCK
        is_member = jnp.concatenate([member_ref[0:KV_HEADS, new_blk:new_blk + 1]] * Q_PER_KV, axis=0) > 0.5
        s_new, v_new = new_row(news_ref)
        s_new = jnp.where(is_member, s_new, NEG_INF)
        m_old = m_ref[...]
        m_new = jnp.maximum(m_old, s_new)
        alpha = jnp.exp2(m_old - m_new)
        e_new = jnp.where(is_member, jnp.exp2(s_new - m_new), 0.0)
        l = alpha * l_ref[...] + e_new
        os = (alpha * acc_ref[...] + e_new * v_new) / l

        lane_g = _iota((1, KV_WIDTH), 1) // HEAD_DIM

        def flat(o):
            segs = []
            for r in range(Q_PER_KV):
                seg = jnp.zeros((1, KV_WIDTH), F32)
                for g in range(KV_HEADS):
                    i = r * KV_HEADS + g
                    seg = seg + jnp.where(lane_g == g, o[i:i + 1, :], 0.0)
                segs.append(seg)
            return jnp.concatenate(segs, axis=1)

        ghi, glo = _split_bf16(jnp.broadcast_to(gate_ref[...], (DROWS, GATE_PAD)))
        gx = (_dot(ghi, gexp_ref[...]) + _dot(glo, gexp_ref[...]))[0:1]
        o = (gx[:, 0:ATTN_WIDTH] * flat(oc_ref[...])
             + gx[:, ATTN_WIDTH:2 * ATTN_WIDTH] * flat(os)
             + gx[:, 2 * ATTN_WIDTH:] * flat(ow_ref[...]))
        o_ref[...] = o.astype(BF16)


def _decode_sw(page_table, q3, slopes, oc, member, gate3, gexp, win_state_t, new_s, new_w, pages_t,
               *, pages_per_step, qpos, past):
    nb, npg = page_table.shape
    p = pages_per_step
    n_buf = win_state_t.shape[2]
    n_pad = member.shape[2]
    per_b = lambda r, c: pl.BlockSpec((None, r, c), lambda b, i, pt: (b, 0, 0))
    full = lambda a: pl.BlockSpec(a.shape, lambda b, i, pt: (0,) * a.ndim)

    def page_spec(k):
        return pl.BlockSpec((None, 2 * KV_WIDTH, PAGE_SIZE), lambda b, i, pt: (pt[b, i * p + k], 0, 0))

    grid_spec = pltpu.PrefetchScalarGridSpec(
        num_scalar_prefetch=1,
        grid=(nb, npg // p),
        in_specs=[per_b(1, ATTN_WIDTH), full(slopes), per_b(DROWS, KV_WIDTH), per_b(DROWS, n_pad), per_b(1, GATE_PAD),
                  full(gexp), per_b(2 * KV_WIDTH, n_buf), per_b(1, 2 * KV_WIDTH), per_b(1, 2 * KV_WIDTH)]
                 + [page_spec(k) for k in range(p)],
        out_specs=per_b(1, ATTN_WIDTH),
        scratch_shapes=[pltpu.VMEM((DROWS, 1), F32), pltpu.VMEM((DROWS, 1), F32),
                        pltpu.VMEM((DROWS, KV_WIDTH), F32), pltpu.VMEM((DROWS, KV_WIDTH), F32)],
    )
    return pl.pallas_call(
        functools.partial(_decode_sw_kernel, pages_per_step=p, qpos=qpos, past=past, n_buf=n_buf),
        grid_spec=grid_spec,
        out_shape=jax.ShapeDtypeStruct((nb, 1, ATTN_WIDTH), BF16),
        compiler_params=_cparams(("arbitrary", "arbitrary")),
        name="decode_slc_win",
    )(page_table, q3, slopes, oc, member, gate3, gexp, win_state_t, new_s, new_w, *([pages_t] * p))


def _outproj_kernel(x_ref, oa_ref, d_ref, wpm_ref, spm_ref, woa_ref, wop_ref, g2_ref, wrh_ref, wrl_ref, br_ref,
                    x1_out, h2_out, tope_out, gate_out):
    n_grp, pm_group = wpm_ref.shape[0], wpm_ref.shape[1]
    pm = jnp.concatenate([_dot(d_ref[:, gi * pm_group:(gi + 1) * pm_group], wpm_ref[gi]) for gi in range(n_grp)], axis=1)
    pm = pm * spm_ref[...]
    x1 = x_ref[...] + _dot(oa_ref[...], woa_ref[...]) + _dot(pm.astype(BF16), wop_ref[...])
    x1_out[...] = x1
    ms = jnp.mean(x1 * x1, axis=-1, keepdims=True)
    h2 = (x1 * lax.rsqrt(ms + EPS)) * g2_ref[...]
    half = h2.shape[1] // 2
    bits = lax.bitcast_convert_type(h2.astype(BF16).astype(F32), jnp.uint32)
    h2_out[...] = (bits[:, half:] & jnp.uint32(0xFFFF0000)) | (bits[:, :half] >> 16)
    hi, lo = _split_bf16(h2)
    logits = _dot(hi, wrh_ref[...]) + _dot(lo, wrh_ref[...]) + _dot(hi, wrl_ref[...]) + br_ref[...]
    tm = logits.shape[0]
    lane = _iota((tm, LANES), 1)
    logits = jnp.where(lane < N_EXPERTS, logits, NEVER)
    vals, idxs = [], []
    for _ in range(TOP_K):
        m = jnp.max(logits, axis=-1, keepdims=True)
        idx = jnp.min(jnp.where(logits == m, lane, LANES), axis=-1, keepdims=True)
        vals.append(m)
        idxs.append(idx)
        logits = jnp.where(lane == idx, NEVER, logits)
    es = [jnp.exp(v - vals[0]) for v in vals]
    den = es[0]
    for e in es[1:]:
        den = den + e
    tope = jnp.full((tm, LANES), -1, I32)
    gts = jnp.zeros((tm, LANES), F32)
    for k in range(TOP_K):
        tope = jnp.where(lane == k, idxs[k], tope)
        gts = jnp.where(lane == k, es[k] / den, gts)
    tope_out[...] = tope
    gate_out[...] = gts


def _outproj(x2d, o_attn, dpool, wpm, spm, woa, wop, g2, wrh, wrl, br, *, tm):
    n, d_model = x2d.shape
    row = lambda w: pl.BlockSpec((tm, w), lambda i: (i, 0))
    full = lambda a: pl.BlockSpec(a.shape, lambda i: (0,) * a.ndim, pipeline_mode=pl.Buffered(1))
    consts = [wpm, spm, woa, wop, g2, wrh, wrl, br]
    return pl.pallas_call(
        _outproj_kernel,
        grid=(n // tm,),
        in_specs=[row(d_model), row(o_attn.shape[1]), row(dpool.shape[1])] + [full(a) for a in consts],
        out_specs=[row(d_model), row(d_model // 2), row(LANES), row(LANES)],
        out_shape=[jax.ShapeDtypeStruct((n, d_model), F32), jax.ShapeDtypeStruct((n, d_model // 2), jnp.uint32),
                   jax.ShapeDtypeStruct((n, LANES), I32), jax.ShapeDtypeStruct((n, LANES), F32)],
        compiler_params=_cparams(("arbitrary",)),
        name="outproj",
    )(x2d, o_attn, dpool, *consts)


MOE_SUB = 256
MOE_ROWS = 5 * MOE_SUB
MOE_FC = 256
ROUTE_TILE = 512


def _route_kernel(e_ref, rank_out, cnt_out, carry_ref):
    @pl.when(pl.program_id(0) == 0)
    def _():
        carry_ref[...] = jnp.zeros(carry_ref.shape, F32)

    tr = e_ref.shape[0]
    lane = _iota((tr, LANES), 1)
    e = e_ref[...]
    ohs = [jnp.where(e[:, k:k + 1] == lane, 1.0, 0.0) for k in range(TOP_K)]
    tot = ohs[0]
    for oh in ohs[1:]:
        tot = tot + oh
    lower = jnp.where(_iota((tr, tr), 1) < _iota((tr, tr), 0), 1.0, 0.0).astype(BF16)
    before = _dot(lower, tot.astype(BF16)) + carry_ref[...]
    rank = jnp.zeros((tr, LANES), I32)
    for k in range(TOP_K):
        rk = jnp.sum(ohs[k] * before, axis=-1, keepdims=True).astype(I32)
        rank = jnp.where(lane == k, rk, rank)
    rank_out[...] = rank
    carry_ref[...] = carry_ref[...] + jnp.sum(tot, axis=0, keepdims=True)
    cnt_out[...] = carry_ref[...]


def _route(tope):
    n = tope.shape[0]
    return pl.pallas_call(
        _route_kernel,
        grid=(n // ROUTE_TILE,),
        in_specs=[pl.BlockSpec((ROUTE_TILE, LANES), lambda i: (i, 0))],
        out_specs=[pl.BlockSpec((ROUTE_TILE, LANES), lambda i: (i, 0)), pl.BlockSpec((1, LANES), lambda i: (0, 0))],
        out_shape=[jax.ShapeDtypeStruct((n, LANES), I32), jax.ShapeDtypeStruct((1, LANES), F32)],
        scratch_shapes=[pltpu.VMEM((1, LANES), F32)],
        compiler_params=_cparams(("arbitrary",)),
        name="moe_route",
    )(tope)


def _slots_kernel(e_ref, rank_ref, per_ref, start_ref, slot_out):
    tr = e_ref.shape[0]
    lane = _iota((tr, LANES), 1)
    e = e_ref[...]
    rank = rank_ref[...]
    slot = jnp.zeros((tr, LANES), I32)
    for k in range(TOP_K):
        oh = jnp.where(e[:, k:k + 1] == lane, 1.0, 0.0)
        per_e = jnp.maximum(jnp.sum(oh * per_ref[...], axis=-1, keepdims=True), 1.0)
        start_e = jnp.sum(oh * start_ref[...], axis=-1, keepdims=True)
        rk = rank[:, k:k + 1].astype(F32)
        item = jnp.floor((rk + 0.5) / per_e)
        s = (start_e + item) * float(MOE_ROWS) + (rk - item * per_e)
        slot = jnp.where(lane == k, s.astype(I32), slot)
    slot_out[...] = slot


def _slots(tope, rank, per_row, start_row):
    n = tope.shape[0]
    blk = pl.BlockSpec((ROUTE_TILE, LANES), lambda i: (i, 0))
    row = pl.BlockSpec((1, LANES), lambda i: (0, 0))
    return pl.pallas_call(
        _slots_kernel,
        grid=(n // ROUTE_TILE,),
        in_specs=[blk, blk, row, row],
        out_specs=blk,
        out_shape=jax.ShapeDtypeStruct((n, LANES), I32),
        compiler_params=_cparams(("arbitrary",)),
        name="moe_slots",
    )(tope, rank, per_row, start_row)


DMA_UNROLL = 4


def _dispatch_kernel(slot_ref, h_ref, *refs):
    xbuf, sem = refs[-2], refs[-1]
    tr = h_ref.shape[0]

    def row_copy(i, s):
        return pltpu.make_async_copy(h_ref.at[pl.ds(i, 1)], xbuf.at[pl.ds(s, 1)], sem)

    def issue(i, carry):
        for k in range(TOP_K):
            row_copy(i, slot_ref[i * TOP_K + k]).start(priority=k % 2)
        return carry

    lax.fori_loop(0, tr, issue, 0, unroll=DMA_UNROLL)
    n_rows = tr * TOP_K
    pltpu.make_async_copy(xbuf.at[pl.ds(0, n_rows)], xbuf.at[pl.ds(0, n_rows)], sem).wait()


def _dispatch(slots_flat, h2, xbuf, n_rows, *, tr):
    n, d = h2.shape
    in_specs = [pl.BlockSpec((tr * TOP_K,), lambda i: (i,), memory_space=pltpu.SMEM),
                pl.BlockSpec((tr, d), lambda i: (i, 0))]
    args = [slots_flat, h2]
    aliases = {}
    if xbuf is not None:
        in_specs.append(pl.BlockSpec(memory_space=pl.ANY))
        args.append(xbuf)
        aliases = {2: 0}
    return pl.pallas_call(
        _dispatch_kernel,
        grid=(n // tr,),
        in_specs=in_specs,
        out_specs=pl.BlockSpec(memory_space=pl.ANY),
        out_shape=jax.ShapeDtypeStruct((n_rows, d), h2.dtype),
        scratch_shapes=[pltpu.SemaphoreType.DMA(())],
        input_output_aliases=aliases,
        compiler_params=pltpu.CompilerParams(dimension_semantics=("arbitrary",), vmem_limit_bytes=VMEM_LIMIT,
                                             has_side_effects=True),
        name="moe_dispatch",
    )(*args)


def _experts_kernel(we_ref, wr_ref, wb_ref, x_ref, wg_ref, wu_ref, bg_ref, bu_ref, wd_ref, bd_ref, o_ref, xb_ref):
    del we_ref, wb_ref
    w = pl.program_id(0)
    c = pl.program_id(1)
    rows = wr_ref[w]

    half = x_ref.shape[1]

    @pl.when(rows > 0)
    def _():
        @pl.when(c == 0)
        def _():
            valid = _iota((MOE_ROWS, 1), 0) < rows
            x = x_ref[...]
            lo = lax.bitcast_convert_type(x << 16, F32)
            hi = lax.bitcast_convert_type(x & jnp.uint32(0xFFFF0000), F32)
            xb_ref[:, :half] = jnp.where(valid, lo, 0.0).astype(BF16)
            xb_ref[:, half:] = jnp.where(valid, hi, 0.0).astype(BF16)
            o_ref[...] = jnp.broadcast_to(bd_ref[...], o_ref.shape)

        def sub_tile(start, size, wg, wu, wd):
            rs = slice(start * MOE_SUB, (start + size) * MOE_SUB)
            xs = xb_ref[rs, :]
            g = _dot(xs, wg) + bg_ref[...]
            u = _dot(xs, wu) + bu_ref[...]
            gh = jnp.minimum(g, SWIGLU_LIMIT)
            up = jnp.clip(u, -SWIGLU_LIMIT, SWIGLU_LIMIT)
            act = (up + 1.0) * gh * jax.nn.sigmoid(SWIGLU_ALPHA * gh)
            o_ref[rs, :] = o_ref[rs, :] + _dot(act.astype(BF16), wd)

        n_sub = (rows + MOE_SUB - 1) // MOE_SUB
        for n in range(1, MOE_ROWS // MOE_SUB + 1):
            @pl.when(n_sub == n)
            def _():
                ws = (wg_ref[...].astype(BF16), wu_ref[...].astype(BF16), wd_ref[...].astype(BF16))
                for i in range(0, n - 1, 2):
                    sub_tile(i, 2, *ws)
                if n % 2:
                    sub_tile(n - 1, 1, *ws)


def _experts(work_e, work_rows, work_blk, xbuf, w_gu, b_gu3, w_down, b_down3):
    n_work = work_e.shape[0]
    n_exp, d_model, two_ff = w_gu.shape
    d_ff = two_ff // 2
    nc = d_ff // MOE_FC

    def cidx(w, c, wr):
        return jnp.where(wr[w] > 0, c, nc - 1)

    grid_spec = pltpu.PrefetchScalarGridSpec(
        num_scalar_prefetch=3,
        grid=(n_work, nc),
        in_specs=[
            pl.BlockSpec((MOE_ROWS, d_model // 2), lambda w, c, we, wr, wb: (wb[w], 0)),
            pl.BlockSpec((None, d_model, MOE_FC), lambda w, c, we, wr, wb: (we[w], 0, cidx(w, c, wr))),
            pl.BlockSpec((None, d_model, MOE_FC), lambda w, c, we, wr, wb: (we[w], 0, nc + cidx(w, c, wr))),
            pl.BlockSpec((None, 1, MOE_FC), lambda w, c, we, wr, wb: (we[w], 0, cidx(w, c, wr))),
            pl.BlockSpec((None, 1, MOE_FC), lambda w, c, we, wr, wb: (we[w], 0, nc + cidx(w, c, wr))),
            pl.BlockSpec((None, MOE_FC, d_model), lambda w, c, we, wr, wb: (we[w], cidx(w, c, wr), 0)),
            pl.BlockSpec((None, 1, d_model), lambda w, c, we, wr, wb: (we[w], 0, 0)),
        ],
        out_specs=pl.BlockSpec((MOE_ROWS, d_model), lambda w, c, we, wr, wb: (wb[w], 0)),
        scratch_shapes=[pltpu.VMEM((MOE_ROWS, d_model), BF16)],
    )
    return pl.pallas_call(
        _experts_kernel,
        grid_spec=grid_spec,
        out_shape=jax.ShapeDtypeStruct((xbuf.shape[0], d_model), F32),
        compiler_params=_cparams(("arbitrary", "arbitrary")),
        name="moe_experts",
    )(work_e, work_rows, work_blk, xbuf, w_gu, w_gu, b_gu3, b_gu3, w_down, b_down3)


def _combine_kernel(slot_ref, next_slot_ref, x1_ref, gate_ref, ybuf, o_ref, rows_ref, sems):
    tc = x1_ref.shape[0]
    step = pl.program_id(0)
    cur = step % 2

    def fetch(slots, buf):
        def issue(i, carry):
            for k in range(TOP_K):
                pltpu.make_async_copy(ybuf.at[pl.ds(slots[i * TOP_K + k], 1)], rows_ref.at[buf, k, pl.ds(i, 1)],
                                      sems.at[buf]).start(priority=k % 2)
            return carry

        lax.fori_loop(0, tc, issue, 0, unroll=DMA_UNROLL)

    @pl.when(step == 0)
    def _():
        fetch(slot_ref, 0)

    @pl.when(step + 1 < pl.num_programs(0))
    def _():
        fetch(next_slot_ref, 1 - cur)

    pltpu.make_async_copy(rows_ref.at[cur], rows_ref.at[cur], sems.at[cur]).wait()
    gates = gate_ref[...]
    y = x1_ref[...]
    for k in range(TOP_K):
        y = y + gates[:, k:k + 1] * rows_ref[cur, k]
    o_ref[...] = y


def _combine(slots_flat, x1, gates, ybuf, *, tc):
    n, d = x1.shape
    steps = n // tc
    return pl.pallas_call(
        _combine_kernel,
        grid=(steps,),
        in_specs=[pl.BlockSpec((tc * TOP_K,), lambda i: (i,), memory_space=pltpu.SMEM),
                  pl.BlockSpec((tc * TOP_K,), lambda i: (jnp.minimum(i + 1, steps - 1),), memory_space=pltpu.SMEM),
                  pl.BlockSpec((tc, d), lambda i: (i, 0)),
                  pl.BlockSpec((tc, LANES), lambda i: (i, 0)),
                  pl.BlockSpec(memory_space=pl.ANY)],
        out_specs=pl.BlockSpec((tc, d), lambda i: (i, 0)),
        out_shape=jax.ShapeDtypeStruct((n, d), F32),
        scratch_shapes=[pltpu.VMEM((2, TOP_K, tc, d), F32), pltpu.SemaphoreType.DMA((2,))],
        compiler_params=_cparams(("arbitrary",)),
        name="moe_combine",
    )(slots_flat, slots_flat, x1, gates, ybuf)


def _moe(h2_p, h2_s, tope_p, tope_s, gate_p, gate_s, x1_p, x1_s, w_gu, b_gu, w_down, b_down):
    n_p, n_s = h2_p.shape[0], h2_s.shape[0]
    n_exp = w_gu.shape[0]
    pad = (-(n_p + n_s)) % ROUTE_TILE
    tope_all = jnp.concatenate([tope_p, tope_s, jnp.full((pad, LANES), -1, I32)], axis=0)
    rank, counts = _route(tope_all)
    counts = counts[0, :n_exp].astype(I32)
    n_assign = (n_p + n_s) * TOP_K
    n_work = n_assign // MOE_ROWS + n_exp
    items = (counts + MOE_ROWS - 1) // MOE_ROWS
    per = -(-counts // jnp.maximum(items, 1))
    per = jnp.maximum(-(-per // MOE_SUB) * MOE_SUB, MOE_SUB)
    item_end = jnp.cumsum(items)
    item_start = item_end - items
    n_used = item_end[-1]
    w_ids = jnp.arange(n_work, dtype=I32)
    used = w_ids < n_used
    w_eff = jnp.where(used, w_ids, n_used - 1)
    work_e = jnp.minimum(jnp.searchsorted(item_end, w_eff, side="right"), n_exp - 1).astype(I32)
    work_rows = jnp.clip(counts[work_e] - (w_eff - item_start[work_e]) * per[work_e], 0, per[work_e])
    work_rows = jnp.where(used, work_rows, 0).astype(I32)
    lane_pad = lambda v: jnp.pad(v.astype(F32), (0, LANES - n_exp))[None, :]
    slots = _slots(tope_all, rank, lane_pad(per), lane_pad(item_start))[:n_p + n_s, :TOP_K].reshape(-1)
    slots_p, slots_s = slots[:n_p * TOP_K], slots[n_p * TOP_K:]
    n_rows = n_work * MOE_ROWS
    xbuf = _dispatch(slots_p, h2_p, None, n_rows, tr=512)
    xbuf = _dispatch(slots_s, h2_s, xbuf, n_rows, tr=n_s)
    ybuf = _experts(work_e, work_rows, w_eff.astype(I32), xbuf, w_gu, b_gu[:, None, :], w_down, b_down[:, None, :])
    y_p = _combine(slots_p, x1_p, gate_p, ybuf, tc=128)
    y_s = _combine(slots_s, x1_s, gate_s, ybuf, tc=n_s)
    return y_p, y_s


def _head_perm():
    return [g * Q_PER_KV + r for r in range(Q_PER_KV) for g in range(KV_HEADS)]


def _pack_w_in(w_in):
    d_model = w_in.shape[0]
    pm_width = d_model - ATTN_WIDTH
    s0 = ATTN_WIDTH
    s1 = s0 + 2 * KV_WIDTH
    s2 = s1 + 2 * KV_WIDTH
    s3 = s2 + 2 * KV_WIDTH
    s4 = s3 + 3 * N_HEADS
    wq = w_in[:, :s0].reshape(d_model, N_HEADS, HEAD_DIM)[:, jnp.array(_head_perm())].reshape(d_model, ATTN_WIDTH)
    wg = w_in[:, s3:s4].reshape(d_model, N_HEADS, 3).transpose(0, 2, 1).reshape(d_model, 3 * N_HEADS)
    wg = jnp.pad(wg, ((0, 0), (0, GATE_PAD - 3 * N_HEADS)))
    return jnp.concatenate([wq, w_in[:, s0:s3], w_in[:, s4:s4 + pm_width], wg], axis=1).astype(BF16)


def _cmp_weights(w_cmp_k, b_cmp_k, w_cmp_v, b_cmp_v):
    def half(o):
        wk = jnp.tile(w_cmp_k[o * CMP_STRIDE:(o + 1) * CMP_STRIDE], (1, KV_HEADS))
        wv = jnp.tile(w_cmp_v[o * CMP_STRIDE:(o + 1) * CMP_STRIDE], (1, KV_HEADS))
        return jnp.concatenate([wk, wv], axis=1)
    bias = jnp.concatenate([jnp.tile(b_cmp_k, KV_HEADS), jnp.tile(b_cmp_v, KV_HEADS)])[None, :]
    return half(0), half(1), bias


def _cmp_to_slc_t(n_cmp_pad, n_cmp, n_slc, n_slc_pad):
    i0 = jnp.arange(n_cmp_pad)[None, :] * CMP_STRIDE
    j0 = jnp.arange(n_slc_pad)[:, None] * SLC_BLOCK
    shared = jnp.minimum(i0 + CMP_LEN, j0 + SLC_BLOCK) - jnp.maximum(i0, j0)
    frac = jnp.clip(shared, 0, None).astype(F32) / CMP_LEN
    ok = (jnp.arange(n_cmp_pad)[None, :] < n_cmp) & (jnp.arange(n_slc_pad)[:, None] < n_slc)
    return jnp.where(ok, frac, 0.0).astype(BF16)


def _block_expand(n_blk_pad, n_keys):
    return (jnp.arange(n_blk_pad)[:, None] == (jnp.arange(n_keys)[None, :] // SLC_BLOCK)).astype(BF16)


def _prompt_mixer(x_prompt, p):
    batch, seq, d_model = x_prompt.shape
    n = batch * seq
    q, kvc, kvs, kvw, gate, u, dpool, ks_b, vs_t, kw_b, vw_t = _inproj(
        x_prompt.reshape(n, d_model), p["g1"], p["w_in"], p["qg"], p["ksg"], p["kwg"], tm=512, seq_len=seq)
    npg = seq // PAGE_SIZE
    pt = (jnp.arange(batch, dtype=I32)[:, None] * npg + jnp.arange(npg, dtype=I32)[None, :])
    kc, vc = _compress(pt, kvc.reshape(batch * npg, PAGE_SIZE, 2 * KV_WIDTH), p["cw0"], p["cw1"], p["cb"], p["kcg"],
                       pages_per_step=8)
    n_cmp = seq // CMP_STRIDE - 1
    n_slc = -(-seq // SLC_BLOCK)
    ct = _cmp_to_slc_t(kc.shape[1], n_cmp, n_slc, LANES)
    ext = _block_expand(LANES, seq).T

    o_attn = _prompt_attn(q, gate, kc, jnp.swapaxes(vc, 1, 2), ks_b, vs_t, kw_b, vw_t, ct, ext, batch=batch, seq=seq)
    return o_attn, dpool, kvc, kvs, kvw, u


def _sample_mixer(x_sample, cache_cmp, cache_slc, state_win, state_pool, page_table, p):
    nb, t, d_model = x_sample.shape
    assert t == 1, "decode path handles one new row per sequence"
    npg = page_table.shape[1]
    past = npg * PAGE_SIZE
    qpos = past
    q, kvc, kvs, kvw, gate, u, dpool = _inproj(
        x_sample.reshape(nb, d_model), p["g1"], p["w_in"], p["qg"], p["ksg"], p["kwg"], tm=nb,
        pool_state=state_pool, hist_pos=qpos)
    def rows_on_lanes(a):
        return jnp.transpose(a, (0, 2, 3, 4, 1)).reshape(a.shape[0], 2 * KV_WIDTH, a.shape[1])

    cw0_t = jnp.tile(p["cw0"].T, (1, PAGE_SIZE // CMP_STRIDE))
    cw1_t = jnp.tile(p["cw1"].T, (1, PAGE_SIZE // CMP_STRIDE))
    a0, a1 = _compress_t(page_table, rows_on_lanes(cache_cmp), cw0_t, cw1_t, pages_per_step=LANES // CMP_SUB_PER_PAGE)
    n_cmp = (past + t) // CMP_STRIDE - 1
    n_slc = -(-(past + t) // SLC_BLOCK)
    n_pad = -(-n_slc // LANES) * LANES
    cmat = _cmp_to_slc_t(a0.shape[2], n_cmp, n_slc, n_pad).T
    slopes = jnp.array([SLOPES[g * Q_PER_KV + r] * LOG2E for r in range(Q_PER_KV) for g in range(KV_HEADS)], F32)[:, None]
    q3 = q.reshape(nb, 1, ATTN_WIDTH)
    oc, member = _decode_cmp(q3, slopes, a0, a1, p["cb"].T, p["kcg"][:, :HEAD_DIM].T, cmat, qpos=qpos, n_slc=n_slc)
    rows = jnp.arange(3 * N_HEADS)
    k_i, g_i, r_i = rows // N_HEADS, (rows % N_HEADS) // Q_PER_KV, rows % Q_PER_KV
    col_head = k_i * N_HEADS + r_i * KV_HEADS + g_i
    gexp = (jnp.arange(3 * ATTN_WIDTH)[None, :] // HEAD_DIM == col_head[:, None])
    gexp = jnp.pad(gexp, ((0, GATE_PAD - 3 * N_HEADS), (0, 0))).astype(BF16)
    o = _decode_sw(page_table, q3, slopes, oc, member, gate.reshape(nb, 1, GATE_PAD), gexp,
                   rows_on_lanes(state_win), kvs.reshape(nb, 1, 2 * KV_WIDTH), kvw.reshape(nb, 1, 2 * KV_WIDTH),
                   rows_on_lanes(cache_slc), pages_per_step=16, qpos=qpos, past=past)
    return o.reshape(nb, ATTN_WIDTH), dpool, kvc, kvs, kvw, u


def _prep_params(norm1_g, w_in, q_gain, k_cmp_gain, k_slc_gain, k_win_gain, w_cmp_k, b_cmp_k, w_cmp_v, b_cmp_v):
    cw0, cw1, cb = _cmp_weights(w_cmp_k, b_cmp_k, w_cmp_v, b_cmp_v)
    return {
        "g1": norm1_g[None, :],
        "w_in": _pack_w_in(w_in),
        "qg": jnp.tile(q_gain, N_HEADS)[None, :],
        "ksg": jnp.tile(k_slc_gain, KV_HEADS)[None, :],
        "kwg": jnp.tile(k_win_gain, KV_HEADS)[None, :],
        "kcg": jnp.tile(k_cmp_gain, KV_HEADS)[None, :],
        "cw0": cw0, "cw1": cw1, "cb": cb,
    }


def _prep_out_params(w_pm, s_pm, w_out, norm2_g, w_router, b_router):
    d_model = w_out.shape[0]
    woa = w_out[:ATTN_WIDTH].reshape(N_HEADS, HEAD_DIM, d_model)[jnp.array(_head_perm())].reshape(ATTN_WIDTH, d_model)
    wr = jnp.pad(w_router, ((0, 0), (0, LANES - w_router.shape[1])))
    wrh, wrl = _split_bf16(wr)
    return (w_pm.astype(BF16), s_pm.reshape(1, -1), woa.astype(BF16), w_out[ATTN_WIDTH:].astype(BF16),
            norm2_g[None, :], wrh, wrl, jnp.pad(b_router, (0, LANES - b_router.shape[0]))[None, :])


def _layer(x_prompt, x_sample, cache_cmp, cache_slc, state_win, state_pool, page_table,
           norm1_g, w_in, q_gain, k_cmp_gain, k_slc_gain, k_win_gain, w_cmp_k, b_cmp_k, w_cmp_v, b_cmp_v,
           w_pm, s_pm, w_out, norm2_g, w_router, b_router, w_gu, b_gu, w_down, b_down):
    batch, seq, d_model = x_prompt.shape
    nb, t = x_sample.shape[:2]
    p = _prep_params(norm1_g, w_in, q_gain, k_cmp_gain, k_slc_gain, k_win_gain, w_cmp_k, b_cmp_k, w_cmp_v, b_cmp_v)
    oa_p, d_p, kvc_p, kvs_p, kvw_p, u_p = _prompt_mixer(x_prompt, p)
    oa_s, d_s, kvc_s, kvs_s, kvw_s, u_s = _sample_mixer(x_sample, cache_cmp, cache_slc, state_win, state_pool, page_table, p)
    op = _prep_out_params(w_pm, s_pm, w_out, norm2_g, w_router, b_router)
    x1_p, h2_p, te_p, gt_p = _outproj(x_prompt.reshape(batch * seq, d_model), oa_p, d_p, *op, tm=512)
    x1_s, h2_s, te_s, gt_s = _outproj(x_sample.reshape(nb * t, d_model), oa_s, d_s, *op, tm=nb * t)
    y_p, y_s = _moe(h2_p, h2_s, te_p, te_s, gt_p, gt_s, x1_p, x1_s, w_gu, b_gu, w_down, b_down)
    kv_shape = (2, KV_HEADS, HEAD_DIM)
    n_win = min(WINDOW, seq)
    st_p = (kvc_p.reshape(batch, seq, *kv_shape), kvs_p.reshape(batch, seq, *kv_shape),
            kvw_p.reshape(batch, seq, *kv_shape)[:, seq - n_win:], u_p.reshape(batch, seq, -1)[:, seq - POOL_HIST:])
    kvw_s5 = kvw_s.reshape(nb, t, *kv_shape)
    u_s3 = u_s.reshape(nb, t, -1)
    st_s = (kvc_s.reshape(nb, t, *kv_shape), kvs_s.reshape(nb, t, *kv_shape),
            jnp.concatenate([state_win, kvw_s5], axis=1)[:, t:], jnp.concatenate([state_pool, u_s3], axis=1)[:, t:])
    return y_p.reshape(batch, seq, d_model), y_s.reshape(nb, t, d_model), st_p, st_s


def kernel(x_prompt, x_sample, cache_cmp_kv, cache_slc_kv, state_win_kv, state_pool, page_table, norm1_g, w_in, q_gain, k_cmp_gain, k_slc_gain, k_win_gain, w_cmp_k, b_cmp_k, w_cmp_v, b_cmp_v, w_pm, s_pm, w_out, norm2_g, w_router, b_router, w_gu, b_gu, w_down, b_down):
    layer_params = (norm1_g, w_in, q_gain, k_cmp_gain, k_slc_gain, k_win_gain, w_cmp_k, b_cmp_k, w_cmp_v, b_cmp_v,
                    w_pm, s_pm, w_out, norm2_g, w_router, b_router, w_gu, b_gu, w_down, b_down)
    y_p, y_s = x_prompt, x_sample
    p_states, s_states = [], []
    for layer in range(norm1_g.shape[0]):
        lw = [w[layer] for w in layer_params]
        y_p, y_s, st_p, st_s = _layer(y_p, y_s, cache_cmp_kv[layer], cache_slc_kv[layer], state_win_kv[layer],
                                      state_pool[layer], page_table, *lw)
        p_states.append(st_p)
        s_states.append(st_s)
    stack = lambda states, i: jnp.stack([s[i] for s in states], axis=0)
    return (y_p, y_s,
            stack(p_states, 0), stack(p_states, 1), stack(p_states, 2), stack(p_states, 3),
            stack(s_states, 0), stack(s_states, 1), stack(s_states, 2), stack(s_states, 3))
```

```python
import functools
import math

import jax
import jax.numpy as jnp
from jax import lax
from jax.experimental import pallas as pl
from jax.experimental.pallas import tpu as pltpu

F32 = jnp.float32
BF16 = jnp.bfloat16
I32 = jnp.int32

N_HEADS = 16
HEAD_DIM = 64
KV_HEADS = 4
Q_PER_KV = N_HEADS // KV_HEADS
ATTN_WIDTH = N_HEADS * HEAD_DIM
KV_WIDTH = KV_HEADS * HEAD_DIM
CMP_LEN = 32
CMP_STRIDE = 16
SLC_BLOCK = 64
N_SELECT = 16
WINDOW = 512
FORCE_SCORE = 1e4
NEG_INF = -1e30
POOL_WINDOWS = (2, 4, 8, 16)
MAX_POOL_W = max(POOL_WINDOWS)
POOL_HIST = MAX_POOL_W - 1
N_EXPERTS = 32
TOP_K = 4
SWIGLU_LIMIT = 7.0
SWIGLU_ALPHA = 1.702
EPS = 1e-6
PAGE_SIZE = 128

LANES = 128
VMEM_LIMIT = 56 * 1024 * 1024

GATE_PAD = LANES
SLOPES = [2.0 ** (-8.0 * (h + 1) / N_HEADS) for h in range(N_HEADS)]
LOG2E = math.log2(math.e)


def _cparams(sem):
    return pltpu.CompilerParams(dimension_semantics=sem, vmem_limit_bytes=VMEM_LIMIT)


def _iota(shape, dim):
    return lax.broadcasted_iota(I32, shape, dim)


def _split_bf16(x):
    hi = x.astype(BF16)
    lo = (x - hi.astype(F32)).astype(BF16)
    return hi, lo


def _dot(a, b):
    return jnp.dot(a, b, preferred_element_type=F32)


def _dot_t(a, b):
    return lax.dot_general(a, b, (((1,), (1,)), ((), ())), preferred_element_type=F32)


def _head_mean_sq(z):
    m, w = z.shape
    ones_bd = jnp.where(_iota((256, 256), 0) // HEAD_DIM == _iota((256, 256), 1) // HEAD_DIM, 1.0, 0.0).astype(BF16)
    zz = z * z
    hi, lo = _split_bf16(zz)
    parts = []
    for c in range(w // 256):
        sl = slice(c * 256, (c + 1) * 256)
        parts.append(_dot(hi[:, sl], ones_bd) + _dot(lo[:, sl], ones_bd))
    ss = parts[0] if len(parts) == 1 else jnp.concatenate(parts, axis=1)
    return ss * (1.0 / HEAD_DIM)


C_Q = 0
C_KVC = ATTN_WIDTH
C_KVS = C_KVC + 2 * KV_WIDTH
C_KVW = C_KVS + 2 * KV_WIDTH
C_U = C_KVW + 2 * KV_WIDTH


def _inproj_kernel(x_ref, g1_ref, w_ref, qg_ref, ksg_ref, kwg_ref, *refs, tm, pm_width, seq_tiles, hist_pos):
    if seq_tiles is None:
        sp_ref, refs = refs[0], refs[1:]
    q_out, kvc_out, kvs_out, kvw_out, gate_out, u_out, d_out = refs[:7]
    x = x_ref[...]
    ms = jnp.mean(x * x, axis=-1, keepdims=True)
    h = (x * lax.rsqrt(ms + EPS)) * g1_ref[...]
    hb = h.astype(BF16)
    c_gate = C_U + pm_width

    zq = _dot(hb, w_ref[:, C_Q:C_KVC])
    qn = (zq * lax.rsqrt(_head_mean_sq(zq) + EPS)) * qg_ref[...]
    q_out[...] = (qn * (HEAD_DIM ** -0.5 * LOG2E)).astype(BF16)

    kvc_out[...] = _dot(hb, w_ref[:, C_KVC:C_KVS])

    zs = _dot(hb, w_ref[:, C_KVS:C_KVW])
    ks = zs[:, :KV_WIDTH]
    ks = (ks * lax.rsqrt(_head_mean_sq(ks) + EPS)) * ksg_ref[...]
    kvs = jnp.concatenate([ks, zs[:, KV_WIDTH:]], axis=1)
    kvs_out[...] = kvs

    zw = _dot(hb, w_ref[:, C_KVW:C_U])
    kw = zw[:, :KV_WIDTH]
    kw = (kw * lax.rsqrt(_head_mean_sq(kw) + EPS)) * kwg_ref[...]
    kvw = jnp.concatenate([kw, zw[:, KV_WIDTH:]], axis=1)
    kvw_out[...] = kvw
    if seq_tiles is not None:
        ksb_out, vst_out, kwb_out, vwt_out = refs[7:11]
        ksb_out[...] = ks.astype(BF16)
        vst_out[...] = zs[:, KV_WIDTH:].T.astype(BF16)
        kwb_out[...] = kw.astype(BF16)
        vw_t = zw[:, KV_WIDTH:].T.astype(BF16)
        for j in range(tm // TKW):
            vwt_out[j] = vw_t[:, j * TKW:(j + 1) * TKW]

    gate_out[...] = jax.nn.sigmoid(_dot(hb, w_ref[:, c_gate:c_gate + GATE_PAD]))

    u = _dot(hb, w_ref[:, C_U:c_gate])
    u_out[...] = u

    pm_group = pm_width // len(POOL_WINDOWS)
    if seq_tiles is None:
        tpos = float(hist_pos + 1)
        for gi, w in enumerate(POOL_WINDOWS):
            cs = slice(gi * pm_group, (gi + 1) * pm_group)
            s = u[:, cs]
            for k in range(1, w):
                s = s + sp_ref[:, POOL_HIST - k, cs]
            d_out[:, cs] = (s / min(float(w), tpos) - u[:, cs]).astype(BF16)
    else:
        ext_ref = refs[11]
        j = pl.program_id(0) % seq_tiles

        @pl.when(j == 0)
        def _():
            ext_ref[0:MAX_POOL_W, :] = jnp.zeros((MAX_POOL_W, pm_width), F32)

        ext_ref[MAX_POOL_W:MAX_POOL_W + tm, :] = u
        tpos = (j * tm + _iota((tm, 1), 0) + 1).astype(F32)
        for gi, w in enumerate(POOL_WINDOWS):
            cs = slice(gi * pm_group, (gi + 1) * pm_group)
            s = ext_ref[MAX_POOL_W:MAX_POOL_W + tm, cs]
            for k in range(1, w):
                s = s + ext_ref[MAX_POOL_W - k:MAX_POOL_W - k + tm, cs]
            cnt = jnp.minimum(float(w), tpos)
            d_out[:, cs] = (s / cnt - u[:, cs]).astype(BF16)
        ext_ref[0:MAX_POOL_W, :] = ext_ref[tm:tm + MAX_POOL_W, :]


def _inproj(x2d, g1, w_packed, qg, ksg, kwg, *, tm, seq_len=None, pool_state=None, hist_pos=None):
    n, d_model = x2d.shape
    pm_width = d_model - ATTN_WIDTH
    seq_tiles = None if seq_len is None else seq_len // tm
    row = lambda w: pl.BlockSpec((tm, w), lambda i: (i, 0))
    full = lambda a: pl.BlockSpec(a.shape, lambda i: (0,) * a.ndim, pipeline_mode=pl.Buffered(1))
    out_shape = [
        jax.ShapeDtypeStruct((n, ATTN_WIDTH), BF16),
        jax.ShapeDtypeStruct((n, 2 * KV_WIDTH), F32),
        jax.ShapeDtypeStruct((n, 2 * KV_WIDTH), F32),
        jax.ShapeDtypeStruct((n, 2 * KV_WIDTH), F32),
        jax.ShapeDtypeStruct((n, GATE_PAD), F32),
        jax.ShapeDtypeStruct((n, pm_width), F32),
        jax.ShapeDtypeStruct((n, pm_width), BF16),
    ]
    out_specs = [row(ATTN_WIDTH), row(2 * KV_WIDTH), row(2 * KV_WIDTH), row(2 * KV_WIDTH),
                 row(GATE_PAD), row(pm_width), row(pm_width)]
    in_specs = [row(d_model), full(g1), full(w_packed), full(qg), full(ksg), full(kwg)]
    args = [x2d, g1, w_packed, qg, ksg, kwg]
    scratch = []
    if seq_tiles is None:
        in_specs.append(pl.BlockSpec((tm, POOL_HIST, pm_width), lambda i: (i, 0, 0)))
        args.append(pool_state)
    else:
        per_tile = TK // tm
        out_shape += [jax.ShapeDtypeStruct((n, KV_WIDTH), BF16), jax.ShapeDtypeStruct((n // TK, KV_WIDTH, TK), BF16),
                      jax.ShapeDtypeStruct((n, KV_WIDTH), BF16), jax.ShapeDtypeStruct((n // TKW, KV_WIDTH, TKW), BF16)]
        out_specs += [row(KV_WIDTH), pl.BlockSpec((None, KV_WIDTH, tm), lambda i: (i // per_tile, 0, i % per_tile)),
                      row(KV_WIDTH), pl.BlockSpec((tm // TKW, KV_WIDTH, TKW), lambda i: (i, 0, 0))]
        scratch.append(pltpu.VMEM((tm + MAX_POOL_W, pm_width), F32))
    return pl.pallas_call(
        functools.partial(_inproj_kernel, tm=tm, pm_width=pm_width, seq_tiles=seq_tiles, hist_pos=hist_pos),
        grid=(n // tm,),
        in_specs=in_specs,
        out_specs=out_specs,
        out_shape=out_shape,
        scratch_shapes=scratch,
        compiler_params=_cparams(("arbitrary",)),
        name="inproj",
    )(*args)


CMP_SUB_PER_PAGE = PAGE_SIZE // CMP_STRIDE


def _compress_kernel(pt_ref, *refs, pages_per_step):
    del pt_ref
    p = pages_per_step
    page_refs = refs[:p]
    halo_ref, w0_ref, w1_ref, b_ref, kg_ref, kc_out, vc_out, xs_ref = refs[p:]
    n_lt = 2 * KV_WIDTH // LANES
    nb = p * CMP_SUB_PER_PAGE
    parts = []
    for c in range(n_lt):
        cs = slice(c * LANES, (c + 1) * LANES)
        for k in range(p):
            xs_ref[c, k * PAGE_SIZE:(k + 1) * PAGE_SIZE, :] = page_refs[k][:, cs]
        xs_ref[c, p * PAGE_SIZE:p * PAGE_SIZE + CMP_STRIDE, :] = halo_ref[:, cs]
        acc = jnp.zeros((nb, LANES), F32) + b_ref[:, cs]
        for j in range(CMP_STRIDE):
            acc = acc + xs_ref[c, pl.ds(j, nb, stride=CMP_STRIDE), :] * w0_ref[j:j + 1, cs]
            acc = acc + xs_ref[c, pl.ds(CMP_STRIDE + j, nb, stride=CMP_STRIDE), :] * w1_ref[j:j + 1, cs]
        parts.append(acc)
    acc = jnp.concatenate(parts, axis=1)
    kc = acc[:, :KV_WIDTH]
    kc = (kc * lax.rsqrt(_head_mean_sq(kc) + EPS)) * kg_ref[...]
    kc_out[...] = kc.astype(BF16)
    vc_out[...] = acc[:, KV_WIDTH:].astype(BF16)


def _compress(page_table, pages, w0t, w1t, bias, kgain, *, pages_per_step):
    nb, npg = page_table.shape
    p = pages_per_step
    steps = npg // p

    def page_spec(k):
        return pl.BlockSpec((None, PAGE_SIZE, 2 * KV_WIDTH), lambda b, i, pt: (pt[b, i * p + k], 0, 0))

    halo_spec = pl.BlockSpec((None, CMP_STRIDE, 2 * KV_WIDTH),
                             lambda b, i, pt: (pt[b, jnp.minimum(i * p + p, npg - 1)], 0, 0))
    full = lambda a: pl.BlockSpec(a.shape, lambda b, i, pt: (0,) * a.ndim)
    out_spec = pl.BlockSpec((None, p * CMP_SUB_PER_PAGE, KV_WIDTH), lambda b, i, pt: (b, i, 0))
    grid_spec = pltpu.PrefetchScalarGridSpec(
        num_scalar_prefetch=1,
        grid=(nb, steps),
        in_specs=[page_spec(k) for k in range(p)] + [halo_spec, full(w0t), full(w1t), full(bias), full(kgain)],
        out_specs=[out_spec, out_spec],
        scratch_shapes=[pltpu.VMEM((2 * KV_WIDTH // LANES, p * PAGE_SIZE + CMP_STRIDE, LANES), F32)],
    )
    return pl.pallas_call(
        functools.partial(_compress_kernel, pages_per_step=p),
        grid_spec=grid_spec,
        out_shape=[jax.ShapeDtypeStruct((nb, npg * CMP_SUB_PER_PAGE, KV_WIDTH), BF16)] * 2,
        compiler_params=_cparams(("arbitrary", "arbitrary")),
        name="compress",
    )(page_table, *([pages] * p), pages, w0t, w1t, bias, kgain)


TQ = 128
TK = 512
TKW = 128
QROWS = Q_PER_KV * TQ


def _select_members_t(score_t, n_cand):
    rows = _iota((n_cand, 1), 0)
    rank = jnp.zeros(score_t.shape, F32)
    for i in range(n_cand):
        si = score_t[i:i + 1, :]
        ahead = jnp.where(si > score_t, 1.0, jnp.where(si == score_t, jnp.where(rows > i, 1.0, 0.0), 0.0))
        rank = rank + ahead
    return jnp.where(rank < float(N_SELECT), 1.0, 0.0)


def _prompt_attn_kernel(q_ref, gate_ref, kc_ref, vct_ref, ks_ref, vst_ref, kw_ref, vwt_ref, ct_ref, ext_ref,
                        o_ref, qpad_ref, memb_ref, tot_ref, *stat_refs, n_slc):
    qi = pl.program_id(1)
    q0 = qi * TQ
    lane = _iota((1, QROWS), 1)
    r_lane = lane // TQ
    qidx = q0 + lane % TQ
    qpos_l = q0 + _iota((1, TQ), 1)
    lane_g = _iota((1, KV_WIDTH), 1) // HEAD_DIM
    n_cmp_pad = kc_ref.shape[0]
    n_kt = memb_ref.shape[1]
    gates_t = gate_ref[...].T

    def slope_row(g):
        row = jnp.zeros((1, QROWS), F32)
        for r in range(Q_PER_KV):
            row = jnp.where(r_lane == r, SLOPES[g * Q_PER_KV + r] * LOG2E, row)
        return row

    def gate_row(g, k):
        c = k * N_HEADS + g * Q_PER_KV
        return jnp.concatenate([gates_t[c + r:c + r + 1, :] for r in range(Q_PER_KV)], axis=1)

    c_start = _iota((n_cmp_pad, QROWS), 0) * CMP_STRIDE
    c_mid = c_start.astype(F32) + 0.5 * (CMP_LEN - 1)
    cmask = (c_start + (CMP_LEN - 1)) <= qidx
    kidx = _iota((TK, TQ), 0)

    for g in range(KV_HEADS):
        qpad = jnp.concatenate(
            [jnp.where(lane_g == g, q_ref[:, r * KV_WIDTH:(r + 1) * KV_WIDTH], jnp.zeros((), BF16))
             for r in range(Q_PER_KV)], axis=0)
        qpad_ref[g] = qpad
        s = _dot_t(kc_ref[...], qpad) + slope_row(g) * c_mid
        s = jnp.where(cmask, s, NEG_INF)
        e = jnp.where(cmask, jnp.exp2(s - jnp.max(s, axis=0, keepdims=True)), 0.0)
        l = jnp.sum(e, axis=0, keepdims=True)
        pc = e * jnp.where(l > 0.0, 1.0 / l, 0.0)
        oc = _dot(vct_ref[g * HEAD_DIM:(g + 1) * HEAD_DIM, :], pc.astype(BF16))
        tot_ref[g] = oc * gate_row(g, 0)
        p_sum = pc[:, 0:TQ]
        for r in range(1, Q_PER_KV):
            p_sum = p_sum + pc[:, r * TQ:(r + 1) * TQ]
        hi, lo = _split_bf16(p_sum)
        imp_t = (_dot(ct_ref[...], hi) + _dot(ct_ref[...], lo))[0:n_slc]
        blk = _iota((n_slc, 1), 0)
        qblk = qpos_l // SLC_BLOCK
        forced = (blk == 0) | (blk == qblk) | (blk == qblk - 1)
        score_t = jnp.where(forced, FORCE_SCORE, jnp.where(blk * SLC_BLOCK <= qpos_l, imp_t, NEG_INF))
        member_t = _select_members_t(score_t, n_slc)
        member_t = jnp.concatenate([member_t, jnp.zeros((LANES - n_slc, TQ), F32)], axis=0).astype(BF16)
        memb_keys = _dot(ext_ref[...], member_t)
        for j in range(n_kt):
            ok = jnp.where(kidx + j * TK <= qpos_l, memb_keys[j * TK:(j + 1) * TK, :], 0.0)
            memb_ref[g, j] = jnp.where(ok > 0.5, 0.0, NEG_INF)

    m_refs, l_refs, acc_refs = (stat_refs[i * KV_HEADS:(i + 1) * KV_HEADS] for i in range(3))

    def sweep(k_ref, vt_ref, lo_t, hi_t, shared_fn, mask_fn, gate_k):
        for g in range(KV_HEADS):
            m_refs[g][...] = jnp.full(m_refs[g].shape, NEG_INF, F32)
            l_refs[g][...] = jnp.zeros(l_refs[g].shape, F32)
            acc_refs[g][...] = jnp.zeros(acc_refs[g].shape, F32)

        def body(kj, carry):
            k0 = pl.multiple_of(kj * TK, TK)
            kt = k_ref[pl.ds(k0, TK), :]
            vt = vt_ref[kj]
            kpos = (kidx + k0).astype(F32)
            shared = shared_fn(k0)
            for g in range(KV_HEADS):
                s_all = _dot_t(kt, qpad_ref[g])
                mask = mask_fn(g, kj, shared)
                vg = vt[g * HEAD_DIM:(g + 1) * HEAD_DIM, :]
                for r in range(Q_PER_KV):
                    cs = slice(r * TQ, (r + 1) * TQ)
                    s = s_all[:, cs] + (SLOPES[g * Q_PER_KV + r] * LOG2E) * kpos + mask
                    m_old = m_refs[g][:, cs]
                    m_new = jnp.maximum(m_old, jnp.max(s, axis=0, keepdims=True))
                    alpha = jnp.exp2(m_old - m_new)
                    p = jnp.exp2(s - m_new)
                    l_refs[g][:, cs] = alpha * l_refs[g][:, cs] + jnp.sum(p, axis=0, keepdims=True)
                    acc_refs[g][:, cs] = alpha * acc_refs[g][:, cs] + _dot(vg, p.astype(BF16))
                    m_refs[g][:, cs] = m_new
            return carry

        lax.fori_loop(lo_t, hi_t, body, 0)
        for g in range(KV_HEADS):
            l = l_refs[g][...]
            tot_ref[g] = tot_ref[g] + acc_refs[g][...] * (jnp.where(l > 0.0, 1.0 / l, 0.0) * gate_row(g, gate_k))

    n_hi = (q0 + TQ + TK - 1) // TK
    sweep(ks_ref, vst_ref, 0, n_hi, lambda k0: None, lambda g, kj, shared: memb_ref[g, kj], 1)

    n_wt = (WINDOW + TQ) // TKW
    nwk = n_wt * TKW
    j0 = jnp.maximum(q0 - WINDOW, 0) // TKW
    w0 = pl.multiple_of(j0 * TKW, TKW)
    kt = kw_ref[pl.ds(w0, nwk), :]
    vt = jnp.concatenate([vwt_ref[j0 + i] for i in range(n_wt)], axis=1)
    d = _iota((nwk, TQ), 0) - _iota((nwk, TQ), 1) + (w0 - q0)
    wmask = jnp.where(d <= 0, jnp.where(d > -WINDOW, 0.0, NEG_INF), NEG_INF)
    wpos = (_iota((nwk, TQ), 0) + w0).astype(F32)
    for g in range(KV_HEADS):
        s_all = _dot_t(kt, qpad_ref[g])
        vg = vt[g * HEAD_DIM:(g + 1) * HEAD_DIM, :]
        for r in range(Q_PER_KV):
            cs = slice(r * TQ, (r + 1) * TQ)
            s = s_all[:, cs] + (SLOPES[g * Q_PER_KV + r] * LOG2E) * wpos + wmask
            e = jnp.exp2(s - jnp.max(s, axis=0, keepdims=True))
            scale = gate_row(g, 2)[:, cs] / jnp.sum(e, axis=0, keepdims=True)
            tot_ref[g, :, cs] = tot_ref[g, :, cs] + _dot(vg, e.astype(BF16)) * scale

    total = jnp.concatenate([tot_ref[g] for g in range(KV_HEADS)], axis=0)
    for r in range(Q_PER_KV):
        o_ref[:, r * KV_WIDTH:(r + 1) * KV_WIDTH] = total[:, r * TQ:(r + 1) * TQ].T.astype(BF16)


def _prompt_attn(q, gate, kc, vct, kvs_b, vst, kvw_b, vwt, ct, ext, *, batch, seq):
    n_slc = -(-seq // SLC_BLOCK)
    n_cmp_pad = kc.shape[1]
    n_qt = seq // TQ
    n_kt = seq // TK
    rowblk = lambda w: pl.BlockSpec((TQ, w), lambda b, i: (b * n_qt + i, 0))
    kblk = pl.BlockSpec((seq, KV_WIDTH), lambda b, i: (b, 0))
    assert seq >= WINDOW + TQ
    vtblk = pl.BlockSpec((n_kt, KV_WIDTH, TK), lambda b, i: (b, 0, 0))
    vwblk = pl.BlockSpec((seq // TKW, KV_WIDTH, TKW), lambda b, i: (b, 0, 0))
    full = lambda a: pl.BlockSpec(a.shape, lambda b, i: (0,) * a.ndim)
    return pl.pallas_call(
        functools.partial(_prompt_attn_kernel, n_slc=n_slc),
        grid=(batch, n_qt),
        in_specs=[rowblk(ATTN_WIDTH), rowblk(GATE_PAD),
                  pl.BlockSpec((None, n_cmp_pad, KV_WIDTH), lambda b, i: (b, 0, 0)),
                  pl.BlockSpec((None, KV_WIDTH, n_cmp_pad), lambda b, i: (b, 0, 0)),
                  kblk, vtblk, kblk, vwblk, full(ct), full(ext)],
        out_specs=rowblk(ATTN_WIDTH),
        out_shape=jax.ShapeDtypeStruct((batch * seq, ATTN_WIDTH), BF16),
        scratch_shapes=[pltpu.VMEM((KV_HEADS, QROWS, KV_WIDTH), BF16),
                        pltpu.VMEM((KV_HEADS, n_kt, TK, TQ), F32),
                        pltpu.VMEM((KV_HEADS, HEAD_DIM, QROWS), F32)]
                       + [pltpu.VMEM((1, QROWS), F32)] * (2 * KV_HEADS) + [pltpu.VMEM((HEAD_DIM, QROWS), F32)] * KV_HEADS,
        compiler_params=_cparams(("arbitrary", "arbitrary")),
        name="prompt_attn",
    )(q, gate, kc, vct, kvs_b, vst, kvw_b, vwt, ct, ext)


DROWS = N_HEADS
NEVER = -3e38


def _decode_qpad(q_ref):
    lane_g = _iota((KV_HEADS, KV_WIDTH), 1) // HEAD_DIM
    row_g = _iota((KV_HEADS, KV_WIDTH), 0)
    parts = []
    for r in range(Q_PER_KV):
        qr = jnp.broadcast_to(q_ref[:, r * KV_WIDTH:(r + 1) * KV_WIDTH].astype(F32), (KV_HEADS, KV_WIDTH))
        parts.append(jnp.where(lane_g == row_g, qr, 0.0))
    return jnp.concatenate(parts, axis=0).astype(BF16)


def _compress_t_kernel(pt_ref, *refs, pages_per_step):
    del pt_ref
    p = pages_per_step
    page_refs = refs[:p]
    w0_ref, w1_ref, a0_out, a1_out = refs[p:]
    prow = _iota((PAGE_SIZE, LANES), 0)
    ocol = _iota((PAGE_SIZE, LANES), 1)
    a0 = jnp.zeros(a0_out.shape, F32)
    a1 = jnp.zeros(a1_out.shape, F32)
    for k in range(p):
        sel = jnp.where(ocol == k * CMP_SUB_PER_PAGE + prow // CMP_STRIDE, 1.0, 0.0).astype(BF16)
        x = page_refs[k][...]
        a0 = a0 + _dot((x * w0_ref[...]).astype(BF16), sel)
        a1 = a1 + _dot((x * w1_ref[...]).astype(BF16), sel)
    a0_out[...] = a0
    a1_out[...] = a1


def _compress_t(page_table, pages_t, w0t, w1t, *, pages_per_step):
    nb, npg = page_table.shape
    p = pages_per_step
    assert p * CMP_SUB_PER_PAGE == LANES

    def page_spec(k):
        return pl.BlockSpec((None, 2 * KV_WIDTH, PAGE_SIZE), lambda b, i, pt: (pt[b, i * p + k], 0, 0))

    full = lambda a: pl.BlockSpec(a.shape, lambda b, i, pt: (0,) * a.ndim)
    out_spec = pl.BlockSpec((None, 2 * KV_WIDTH, LANES), lambda b, i, pt: (b, 0, i))
    grid_spec = pltpu.PrefetchScalarGridSpec(
        num_scalar_prefetch=1,
        grid=(nb, npg // p),
        in_specs=[page_spec(k) for k in range(p)] + [full(w0t), full(w1t)],
        out_specs=[out_spec, out_spec],
    )
    return pl.pallas_call(
        functools.partial(_compress_t_kernel, pages_per_step=p),
        grid_spec=grid_spec,
        out_shape=[jax.ShapeDtypeStruct((nb, 2 * KV_WIDTH, npg * CMP_SUB_PER_PAGE), F32)] * 2,
        compiler_params=_cparams(("arbitrary", "arbitrary")),
        name="compress_t",
    )(page_table, *([pages_t] * p), w0t, w1t)


def _decode_cmp_kernel(q_ref, slope_ref, a0_ref, a1_ref, cb_ref, kg_ref, c_ref, oc_out, member_out, *, qpos, n_slc):
    qpad = _decode_qpad(q_ref)
    slope = slope_ref[...]
    n_cmp_pad = a0_ref.shape[1]
    acc = a0_ref[...] + pltpu.roll(a1_ref[...], n_cmp_pad - 1, 1) + cb_ref[...]
    kparts = []
    for g in range(KV_HEADS):
        kg = acc[g * HEAD_DIM:(g + 1) * HEAD_DIM, :]
        ms = jnp.mean(kg * kg, axis=0, keepdims=True)
        kparts.append((kg * lax.rsqrt(ms + EPS)) * kg_ref[...])
    kc_t = jnp.concatenate(kparts, axis=0).astype(BF16)
    vc_t = acc[KV_WIDTH:, :].astype(BF16)
    c_start = _iota((1, n_cmp_pad), 1) * CMP_STRIDE
    c_mid = c_start.astype(F32) + 0.5 * (CMP_LEN - 1)
    cmask = (c_start + (CMP_LEN - 1)) <= qpos
    s = _dot(qpad, kc_t) + slope * c_mid
    s = jnp.where(cmask, s, NEG_INF)
    e = jnp.where(cmask, jnp.exp2(s - jnp.max(s, axis=-1, keepdims=True)), 0.0)
    l = jnp.sum(e, axis=-1, keepdims=True)
    pc = e * jnp.where(l > 0.0, 1.0 / l, 0.0)
    oc_out[...] = _dot_t(pc.astype(BF16), vc_t)
    p_sum = pc[0:KV_HEADS]
    for r in range(1, Q_PER_KV):
        p_sum = p_sum + pc[r * KV_HEADS:(r + 1) * KV_HEADS]
    p_sum = jnp.concatenate([p_sum, jnp.zeros((DROWS - KV_HEADS, n_cmp_pad), F32)], axis=0)
    hi, lo = _split_bf16(p_sum)
    imp = _dot(hi, c_ref[...]) + _dot(lo, c_ref[...])
    n_pad = imp.shape[1]
    blk = _iota((1, n_pad), 1)
    qblk = qpos // SLC_BLOCK
    forced = (blk == 0) | (blk == qblk) | (blk == qblk - 1)
    score = jnp.where(forced, FORCE_SCORE, jnp.where(blk * SLC_BLOCK <= qpos, imp, NEG_INF))
    score = jnp.where(blk < n_slc, score, NEVER)
    rank = jnp.zeros(score.shape, F32)
    for i in range(n_slc):
        si = score[:, i:i + 1]
        rank = rank + jnp.where(si > score, 1.0, jnp.where(si == score, jnp.where(blk > i, 1.0, 0.0), 0.0))
    member_out[...] = jnp.where(rank < float(min(N_SELECT, n_slc)), 1.0, 0.0)


def _decode_cmp(q3, slopes, a0, a1, cb_col, kg_col, cmat, *, qpos, n_slc):
    nb = q3.shape[0]
    n_cmp_pad = a0.shape[2]
    n_pad = cmat.shape[1]
    per_b = lambda r, c: pl.BlockSpec((None, r, c), lambda b: (b, 0, 0))
    full = lambda a: pl.BlockSpec(a.shape, lambda b: (0,) * a.ndim)
    return pl.pallas_call(
        functools.partial(_decode_cmp_kernel, qpos=qpos, n_slc=n_slc),
        grid=(nb,),
        in_specs=[per_b(1, ATTN_WIDTH), full(slopes), per_b(2 * KV_WIDTH, n_cmp_pad), per_b(2 * KV_WIDTH, n_cmp_pad),
                  full(cb_col), full(kg_col), full(cmat)],
        out_specs=[per_b(DROWS, KV_WIDTH), per_b(DROWS, n_pad)],
        out_shape=[jax.ShapeDtypeStruct((nb, DROWS, KV_WIDTH), F32), jax.ShapeDtypeStruct((nb, DROWS, n_pad), F32)],
        compiler_params=_cparams(("arbitrary",)),
        name="decode_cmp",
    )(q3, slopes, a0, a1, cb_col, kg_col, cmat)


def _decode_sw_kernel(pt_ref, q_ref, slope_ref, oc_ref, member_ref, gate_ref, gexp_ref, win_ref, news_ref, neww_ref,
                      *refs, pages_per_step, qpos, past, n_buf):
    del pt_ref
    p = pages_per_step
    page_refs = refs[:p]
    o_ref, m_ref, l_ref, acc_ref, ow_ref = refs[p:]
    c = pl.program_id(1)
    qpad = _decode_qpad(q_ref)
    qf = qpad.astype(F32)
    slope = slope_ref[...]

    def new_row(row_ref):
        kn = row_ref[:, :KV_WIDTH].astype(BF16).astype(F32)
        vn = row_ref[:, KV_WIDTH:].astype(BF16).astype(F32)
        return jnp.sum(qf * kn, axis=-1, keepdims=True) + slope * float(qpos), vn

    @pl.when(c == 0)
    def _():
        m_ref[...] = jnp.full((DROWS, 1), NEG_INF, F32)
        l_ref[...] = jnp.zeros((DROWS, 1), F32)
        acc_ref[...] = jnp.zeros((DROWS, KV_WIDTH), F32)
        kw_t = win_ref[:KV_WIDTH, :].astype(BF16)
        vw_t = win_ref[KV_WIDTH:, :].astype(BF16)
        wpos = past - n_buf + _iota((1, n_buf), 1)
        ok = (wpos <= qpos) & (wpos > qpos - WINDOW) & (wpos >= 0)
        s = jnp.where(ok, _dot(qpad, kw_t) + slope * wpos.astype(F32), NEG_INF)
        s_new, v_new = new_row(neww_ref)
        m = jnp.maximum(jnp.max(s, axis=-1, keepdims=True), s_new)
        e = jnp.where(ok, jnp.exp2(s - m), 0.0)
        e_new = jnp.exp2(s_new - m)
        l = jnp.sum(e, axis=-1, keepdims=True) + e_new
        ow_ref[...] = (_dot_t(e.astype(BF16), vw_t) + e_new * v_new) / l

    nk = p * PAGE_SIZE
    kt_t = jnp.concatenate([page_refs[k][:KV_WIDTH, :].astype(BF16) for k in range(p)], axis=1)
    vt_t = jnp.concatenate([page_refs[k][KV_WIDTH:, :].astype(BF16) for k in range(p)], axis=1)
    member = member_ref[0:KV_HEADS, :].astype(BF16)
    member = jnp.concatenate([member] * Q_PER_KV, axis=0)
    n_pad = member.shape[1]
    kidx = c * nk + _iota((n_pad, nk), 1)
    expand = jnp.where(kidx // SLC_BLOCK == _iota((n_pad, nk), 0), 1.0, 0.0).astype(BF16)
    mk = _dot(member, expand)
    kpos = (c * nk + _iota((1, nk), 1)).astype(F32)
    s = jnp.where(mk > 0.5, _dot(qpad, kt_t) + slope * kpos, NEG_INF)
    m_old = m_ref[...]
    m_new = jnp.maximum(m_old, jnp.max(s, axis=-1, keepdims=True))
    alpha = jnp.exp2(m_old - m_new)
    pr = jnp.exp2(s - m_new)
    l_ref[...] = alpha * l_ref[...] + jnp.sum(pr, axis=-1, keepdims=True)
    acc_ref[...] = alpha * acc_ref[...] + _dot_t(pr.astype(BF16), vt_t)
    m_ref[...] = m_new

    @pl.when(c == pl.num_programs(1) - 1)
    def _():
        new_blk = qpos // SLC_BLOCK
        is_member = jnp.concatenate([member_ref[0:KV_HEADS, new_blk:new_blk + 1]] * Q_PER_KV, axis=0) > 0.5
        s_new, v_new = new_row(news_ref)
        s_new = jnp.where(is_member, s_new, NEG_INF)
        m_old = m_ref[...]
        m_new = jnp.maximum(m_old, s_new)
        alpha = jnp.exp2(m_old - m_new)
        e_new = jnp.where(is_member, jnp.exp2(s_new - m_new), 0.0)
        l = alpha * l_ref[...] + e_new
        os = (alpha * acc_ref[...] + e_new * v_new) / l

        lane_g = _iota((1, KV_WIDTH), 1) // HEAD_DIM

        def flat(o):
            segs = []
            for r in range(Q_PER_KV):
                seg = jnp.zeros((1, KV_WIDTH), F32)
                for g in range(KV_HEADS):
                    i = r * KV_HEADS + g
                    seg = seg + jnp.where(lane_g == g, o[i:i + 1, :], 0.0)
                segs.append(seg)
            return jnp.concatenate(segs, axis=1)

        ghi, glo = _split_bf16(jnp.broadcast_to(gate_ref[...], (DROWS, GATE_PAD)))
        gx = (_dot(ghi, gexp_ref[...]) + _dot(glo, gexp_ref[...]))[0:1]
        o = (gx[:, 0:ATTN_WIDTH] * flat(oc_ref[...])
             + gx[:, ATTN_WIDTH:2 * ATTN_WIDTH] * flat(os)
             + gx[:, 2 * ATTN_WIDTH:] * flat(ow_ref[...]))
        o_ref[...] = o.astype(BF16)


def _decode_sw(page_table, q3, slopes, oc, member, gate3, gexp, win_state_t, new_s, new_w, pages_t,
               *, pages_per_step, qpos, past):
    nb, npg = page_table.shape
    p = pages_per_step
    n_buf = win_state_t.shape[2]
    n_pad = member.shape[2]
    per_b = lambda r, c: pl.BlockSpec((None, r, c), lambda b, i, pt: (b, 0, 0))
    full = lambda a: pl.BlockSpec(a.shape, lambda b, i, pt: (0,) * a.ndim)

    def page_spec(k):
        return pl.BlockSpec((None, 2 * KV_WIDTH, PAGE_SIZE), lambda b, i, pt: (pt[b, i * p + k], 0, 0))

    grid_spec = pltpu.PrefetchScalarGridSpec(
        num_scalar_prefetch=1,
        grid=(nb, npg // p),
        in_specs=[per_b(1, ATTN_WIDTH), full(slopes), per_b(DROWS, KV_WIDTH), per_b(DROWS, n_pad), per_b(1, GATE_PAD),
                  full(gexp), per_b(2 * KV_WIDTH, n_buf), per_b(1, 2 * KV_WIDTH), per_b(1, 2 * KV_WIDTH)]
                 + [page_spec(k) for k in range(p)],
        out_specs=per_b(1, ATTN_WIDTH),
        scratch_shapes=[pltpu.VMEM((DROWS, 1), F32), pltpu.VMEM((DROWS, 1), F32),
                        pltpu.VMEM((DROWS, KV_WIDTH), F32), pltpu.VMEM((DROWS, KV_WIDTH), F32)],
    )
    return pl.pallas_call(
        functools.partial(_decode_sw_kernel, pages_per_step=p, qpos=qpos, past=past, n_buf=n_buf),
        grid_spec=grid_spec,
        out_shape=jax.ShapeDtypeStruct((nb, 1, ATTN_WIDTH), BF16),
        compiler_params=_cparams(("arbitrary", "arbitrary")),
        name="decode_slc_win",
    )(page_table, q3, slopes, oc, member, gate3, gexp, win_state_t, new_s, new_w, *([pages_t] * p))


def _outproj_kernel(x_ref, oa_ref, d_ref, wpm_ref, spm_ref, woa_ref, wop_ref, g2_ref, wrh_ref, wrl_ref, br_ref,
                    x1_out, h2_out, tope_out, gate_out):
    n_grp, pm_group = wpm_ref.shape[0], wpm_ref.shape[1]
    pm = jnp.concatenate([_dot(d_ref[:, gi * pm_group:(gi + 1) * pm_group], wpm_ref[gi]) for gi in range(n_grp)], axis=1)
    pm = pm * spm_ref[...]
    x1 = x_ref[...] + _dot(oa_ref[...], woa_ref[...]) + _dot(pm.astype(BF16), wop_ref[...])
    x1_out[...] = x1
    ms = jnp.mean(x1 * x1, axis=-1, keepdims=True)
    h2 = (x1 * lax.rsqrt(ms + EPS)) * g2_ref[...]
    half = h2.shape[1] // 2
    bits = lax.bitcast_convert_type(h2.astype(BF16).astype(F32), jnp.uint32)
    h2_out[...] = (bits[:, half:] & jnp.uint32(0xFFFF0000)) | (bits[:, :half] >> 16)
    hi, lo = _split_bf16(h2)
    logits = _dot(hi, wrh_ref[...]) + _dot(lo, wrh_ref[...]) + _dot(hi, wrl_ref[...]) + br_ref[...]
    tm = logits.shape[0]
    lane = _iota((tm, LANES), 1)
    logits = jnp.where(lane < N_EXPERTS, logits, NEVER)
    vals, idxs = [], []
    for _ in range(TOP_K):
        m = jnp.max(logits, axis=-1, keepdims=True)
        idx = jnp.min(jnp.where(logits == m, lane, LANES), axis=-1, keepdims=True)
        vals.append(m)
        idxs.append(idx)
        logits = jnp.where(lane == idx, NEVER, logits)
    es = [jnp.exp(v - vals[0]) for v in vals]
    den = es[0]
    for e in es[1:]:
        den = den + e
    tope = jnp.full((tm, LANES), -1, I32)
    gts = jnp.zeros((tm, LANES), F32)
    for k in range(TOP_K):
        tope = jnp.where(lane == k, idxs[k], tope)
        gts = jnp.where(lane == k, es[k] / den, gts)
    tope_out[...] = tope
    gate_out[...] = gts


def _outproj(x2d, o_attn, dpool, wpm, spm, woa, wop, g2, wrh, wrl, br, *, tm):
    n, d_model = x2d.shape
    row = lambda w: pl.BlockSpec((tm, w), lambda i: (i, 0))
    full = lambda a: pl.BlockSpec(a.shape, lambda i: (0,) * a.ndim, pipeline_mode=pl.Buffered(1))
    consts = [wpm, spm, woa, wop, g2, wrh, wrl, br]
    return pl.pallas_call(
        _outproj_kernel,
        grid=(n // tm,),
        in_specs=[row(d_model), row(o_attn.shape[1]), row(dpool.shape[1])] + [full(a) for a in consts],
        out_specs=[row(d_model), row(d_model // 2), row(LANES), row(LANES)],
        out_shape=[jax.ShapeDtypeStruct((n, d_model), F32), jax.ShapeDtypeStruct((n, d_model // 2), jnp.uint32),
                   jax.ShapeDtypeStruct((n, LANES), I32), jax.ShapeDtypeStruct((n, LANES), F32)],
        compiler_params=_cparams(("arbitrary",)),
        name="outproj",
    )(x2d, o_attn, dpool, *consts)


MOE_SUB = 128
MOE_GROUPS = (4, 2, 1)
MOE_ROWS = 10 * MOE_SUB
MOE_FC = 256
ROUTE_TILE = 512


def _route_kernel(e_ref, rank_out, cnt_out, carry_ref):
    @pl.when(pl.program_id(0) == 0)
    def _():
        carry_ref[...] = jnp.zeros(carry_ref.shape, F32)

    tr = e_ref.shape[0]
    lane = _iota((tr, LANES), 1)
    e = e_ref[...]
    ohs = [jnp.where(e[:, k:k + 1] == lane, 1.0, 0.0) for k in range(TOP_K)]
    tot = ohs[0]
    for oh in ohs[1:]:
        tot = tot + oh
    lower = jnp.where(_iota((tr, tr), 1) < _iota((tr, tr), 0), 1.0, 0.0).astype(BF16)
    before = _dot(lower, tot.astype(BF16)) + carry_ref[...]
    rank = jnp.zeros((tr, LANES), I32)
    for k in range(TOP_K):
        rk = jnp.sum(ohs[k] * before, axis=-1, keepdims=True).astype(I32)
        rank = jnp.where(lane == k, rk, rank)
    rank_out[...] = rank
    carry_ref[...] = carry_ref[...] + jnp.sum(tot, axis=0, keepdims=True)
    cnt_out[...] = carry_ref[...]


def _route(tope):
    n = tope.shape[0]
    return pl.pallas_call(
        _route_kernel,
        grid=(n // ROUTE_TILE,),
        in_specs=[pl.BlockSpec((ROUTE_TILE, LANES), lambda i: (i, 0))],
        out_specs=[pl.BlockSpec((ROUTE_TILE, LANES), lambda i: (i, 0)), pl.BlockSpec((1, LANES), lambda i: (0, 0))],
        out_shape=[jax.ShapeDtypeStruct((n, LANES), I32), jax.ShapeDtypeStruct((1, LANES), F32)],
        scratch_shapes=[pltpu.VMEM((1, LANES), F32)],
        compiler_params=_cparams(("arbitrary",)),
        name="moe_route",
    )(tope)


def _slots_kernel(e_ref, rank_ref, per_ref, start_ref, slot_out):
    tr = e_ref.shape[0]
    lane = _iota((tr, LANES), 1)
    e = e_ref[...]
    rank = rank_ref[...]
    slot = jnp.zeros((tr, LANES), I32)
    for k in range(TOP_K):
        oh = jnp.where(e[:, k:k + 1] == lane, 1.0, 0.0)
        per_e = jnp.maximum(jnp.sum(oh * per_ref[...], axis=-1, keepdims=True), 1.0)
        start_e = jnp.sum(oh * start_ref[...], axis=-1, keepdims=True)
        rk = rank[:, k:k + 1].astype(F32)
        item = jnp.floor((rk + 0.5) / per_e)
        s = (start_e + item) * float(MOE_ROWS) + (rk - item * per_e)
        slot = jnp.where(lane == k, s.astype(I32), slot)
    slot_out[...] = slot


def _slots(tope, rank, per_row, start_row):
    n = tope.shape[0]
    blk = pl.BlockSpec((ROUTE_TILE, LANES), lambda i: (i, 0))
    row = pl.BlockSpec((1, LANES), lambda i: (0, 0))
    return pl.pallas_call(
        _slots_kernel,
        grid=(n // ROUTE_TILE,),
        in_specs=[blk, blk, row, row],
        out_specs=blk,
        out_shape=jax.ShapeDtypeStruct((n, LANES), I32),
        compiler_params=_cparams(("arbitrary",)),
        name="moe_slots",
    )(tope, rank, per_row, start_row)


DMA_UNROLL = 4


def _dispatch_kernel(slot_ref, h_ref, *refs):
    xbuf, sem = refs[-2], refs[-1]
    tr = h_ref.shape[0]

    def row_copy(i, s):
        return pltpu.make_async_copy(h_ref.at[pl.ds(i, 1)], xbuf.at[pl.ds(s, 1)], sem)

    def issue(i, carry):
        for k in range(TOP_K):
            row_copy(i, slot_ref[i * TOP_K + k]).start(priority=k % 2)
        return carry

    lax.fori_loop(0, tr, issue, 0, unroll=DMA_UNROLL)
    n_rows = tr * TOP_K
    pltpu.make_async_copy(xbuf.at[pl.ds(0, n_rows)], xbuf.at[pl.ds(0, n_rows)], sem).wait()


def _dispatch(slots_flat, h2, xbuf, n_rows, *, tr):
    n, d = h2.shape
    in_specs = [pl.BlockSpec((tr * TOP_K,), lambda i: (i,), memory_space=pltpu.SMEM),
                pl.BlockSpec((tr, d), lambda i: (i, 0))]
    args = [slots_flat, h2]
    aliases = {}
    if xbuf is not None:
        in_specs.append(pl.BlockSpec(memory_space=pl.ANY))
        args.append(xbuf)
        aliases = {2: 0}
    return pl.pallas_call(
        _dispatch_kernel,
        grid=(n // tr,),
        in_specs=in_specs,
        out_specs=pl.BlockSpec(memory_space=pl.ANY),
        out_shape=jax.ShapeDtypeStruct((n_rows, d), h2.dtype),
        scratch_shapes=[pltpu.SemaphoreType.DMA(())],
        input_output_aliases=aliases,
        compiler_params=pltpu.CompilerParams(dimension_semantics=("arbitrary",), vmem_limit_bytes=VMEM_LIMIT,
                                             has_side_effects=True),
        name="moe_dispatch",
    )(*args)


def _experts_kernel(we_ref, wr_ref, wb_ref, x_ref, wg_ref, wu_ref, bg_ref, bu_ref, wd_ref, bd_ref, o_ref, xb_ref):
    del we_ref, wb_ref
    w = pl.program_id(0)
    c = pl.program_id(1)
    rows = wr_ref[w]

    half = x_ref.shape[1]

    @pl.when(rows > 0)
    def _():
        @pl.when(c == 0)
        def _():
            valid = _iota((MOE_ROWS, 1), 0) < rows
            x = x_ref[...]
            lo = lax.bitcast_convert_type(x << 16, F32)
            hi = lax.bitcast_convert_type(x & jnp.uint32(0xFFFF0000), F32)
            xb_ref[:, :half] = jnp.where(valid, lo, 0.0).astype(BF16)
            xb_ref[:, half:] = jnp.where(valid, hi, 0.0).astype(BF16)
            o_ref[...] = jnp.broadcast_to(bd_ref[...], o_ref.shape)

        def sub_tile(start, size, wg, wu, wd):
            rs = slice(start * MOE_SUB, (start + size) * MOE_SUB)
            xs = xb_ref[rs, :]
            g = _dot(xs, wg) + bg_ref[...]
            u = _dot(xs, wu) + bu_ref[...]
            gh = jnp.minimum(g, SWIGLU_LIMIT)
            up = jnp.clip(u, -SWIGLU_LIMIT, SWIGLU_LIMIT)
            act = (up + 1.0) * gh * jax.nn.sigmoid(SWIGLU_ALPHA * gh)
            o_ref[rs, :] = o_ref[rs, :] + _dot(act.astype(BF16), wd)

        n_sub = (rows + MOE_SUB - 1) // MOE_SUB
        for n in range(1, MOE_ROWS // MOE_SUB + 1):
            @pl.when(n_sub == n)
            def _():
                ws = (wg_ref[...].astype(BF16), wu_ref[...].astype(BF16), wd_ref[...].astype(BF16))
                start = 0
                for size in MOE_GROUPS:
                    while n - start >= size:
                        sub_tile(start, size, *ws)
                        start += size


def _experts(work_e, work_rows, work_blk, xbuf, w_gu, b_gu3, w_down, b_down3):
    n_work = work_e.shape[0]
    n_exp, d_model, two_ff = w_gu.shape
    d_ff = two_ff // 2
    nc = d_ff // MOE_FC

    def cidx(w, c, wr):
        return jnp.where(wr[w] > 0, c, nc - 1)

    grid_spec = pltpu.PrefetchScalarGridSpec(
        num_scalar_prefetch=3,
        grid=(n_work, nc),
        in_specs=[
            pl.BlockSpec((MOE_ROWS, d_model // 2), lambda w, c, we, wr, wb: (wb[w], 0)),
            pl.BlockSpec((None, d_model, MOE_FC), lambda w, c, we, wr, wb: (we[w], 0, cidx(w, c, wr))),
            pl.BlockSpec((None, d_model, MOE_FC), lambda w, c, we, wr, wb: (we[w], 0, nc + cidx(w, c, wr))),
            pl.BlockSpec((None, 1, MOE_FC), lambda w, c, we, wr, wb: (we[w], 0, cidx(w, c, wr))),
            pl.BlockSpec((None, 1, MOE_FC), lambda w, c, we, wr, wb: (we[w], 0, nc + cidx(w, c, wr))),
            pl.BlockSpec((None, MOE_FC, d_model), lambda w, c, we, wr, wb: (we[w], cidx(w, c, wr), 0)),
            pl.BlockSpec((None, 1, d_model), lambda w, c, we, wr, wb: (we[w], 0, 0)),
        ],
        out_specs=pl.BlockSpec((MOE_ROWS, d_model), lambda w, c, we, wr, wb: (wb[w], 0)),
        scratch_shapes=[pltpu.VMEM((MOE_ROWS, d_model), BF16)],
    )
    return pl.pallas_call(
        _experts_kernel,
        grid_spec=grid_spec,
        out_shape=jax.ShapeDtypeStruct((xbuf.shape[0], d_model), F32),
        compiler_params=_cparams(("arbitrary", "arbitrary")),
        name="moe_experts",
    )(work_e, work_rows, work_blk, xbuf, w_gu, w_gu, b_gu3, b_gu3, w_down, b_down3)


def _combine_kernel(slot_ref, next_slot_ref, x1_ref, gate_ref, ybuf, o_ref, rows_ref, sems):
    tc = x1_ref.shape[0]
    step = pl.program_id(0)
    cur = step % 2

    def fetch(slots, buf):
        def issue(i, carry):
            for k in range(TOP_K):
                pltpu.make_async_copy(ybuf.at[pl.ds(slots[i * TOP_K + k], 1)], rows_ref.at[buf, k, pl.ds(i, 1)],
                                      sems.at[buf]).start(priority=k % 2)
            return carry

        lax.fori_loop(0, tc, issue, 0, unroll=DMA_UNROLL)

    @pl.when(step == 0)
    def _():
        fetch(slot_ref, 0)

    @pl.when(step + 1 < pl.num_programs(0))
    def _():
        fetch(next_slot_ref, 1 - cur)

    pltpu.make_async_copy(rows_ref.at[cur], rows_ref.at[cur], sems.at[cur]).wait()
    gates = gate_ref[...]
    y = x1_ref[...]
    for k in range(TOP_K):
        y = y + gates[:, k:k + 1] * rows_ref[cur, k]
    o_ref[...] = y


def _combine(slots_flat, x1, gates, ybuf, *, tc):
    n, d = x1.shape
    steps = n // tc
    return pl.pallas_call(
        _combine_kernel,
        grid=(steps,),
        in_specs=[pl.BlockSpec((tc * TOP_K,), lambda i: (i,), memory_space=pltpu.SMEM),
                  pl.BlockSpec((tc * TOP_K,), lambda i: (jnp.minimum(i + 1, steps - 1),), memory_space=pltpu.SMEM),
                  pl.BlockSpec((tc, d), lambda i: (i, 0)),
                  pl.BlockSpec((tc, LANES), lambda i: (i, 0)),
                  pl.BlockSpec(memory_space=pl.ANY)],
        out_specs=pl.BlockSpec((tc, d), lambda i: (i, 0)),
        out_shape=jax.ShapeDtypeStruct((n, d), F32),
        scratch_shapes=[pltpu.VMEM((2, TOP_K, tc, d), F32), pltpu.SemaphoreType.DMA((2,))],
        compiler_params=_cparams(("arbitrary",)),
        name="moe_combine",
    )(slots_flat, slots_flat, x1, gates, ybuf)


def _moe(h2_p, h2_s, tope_p, tope_s, gate_p, gate_s, x1_p, x1_s, w_gu, b_gu, w_down, b_down):
    n_p, n_s = h2_p.shape[0], h2_s.shape[0]
    n_exp = w_gu.shape[0]
    pad = (-(n_p + n_s)) % ROUTE_TILE
    tope_all = jnp.concatenate([tope_p, tope_s, jnp.full((pad, LANES), -1, I32)], axis=0)
    rank, counts = _route(tope_all)
    counts = counts[0, :n_exp].astype(I32)
    n_assign = (n_p + n_s) * TOP_K
    n_work = n_assign // MOE_ROWS + n_exp
    items = (counts + MOE_ROWS - 1) // MOE_ROWS
    per = -(-counts // jnp.maximum(items, 1))
    per = jnp.maximum(-(-per // MOE_SUB) * MOE_SUB, MOE_SUB)
    item_end = jnp.cumsum(items)
    item_start = item_end - items
    n_used = item_end[-1]
    w_ids = jnp.arange(n_work, dtype=I32)
    used = w_ids < n_used
    w_eff = jnp.where(used, w_ids, n_used - 1)
    work_e = jnp.minimum(jnp.searchsorted(item_end, w_eff, side="right"), n_exp - 1).astype(I32)
    work_rows = jnp.clip(counts[work_e] - (w_eff - item_start[work_e]) * per[work_e], 0, per[work_e])
    work_rows = jnp.where(used, work_rows, 0).astype(I32)
    lane_pad = lambda v: jnp.pad(v.astype(F32), (0, LANES - n_exp))[None, :]
    slots = _slots(tope_all, rank, lane_pad(per), lane_pad(item_start))[:n_p + n_s, :TOP_K].reshape(-1)
    slots_p, slots_s = slots[:n_p * TOP_K], slots[n_p * TOP_K:]
    n_rows = n_work * MOE_ROWS
    xbuf = _dispatch(slots_p, h2_p, None, n_rows, tr=512)
    xbuf = _dispatch(slots_s, h2_s, xbuf, n_rows, tr=n_s)
    ybuf = _experts(work_e, work_rows, w_eff.astype(I32), xbuf, w_gu, b_gu[:, None, :], w_down, b_down[:, None, :])
    y_p = _combine(slots_p, x1_p, gate_p, ybuf, tc=128)
    y_s = _combine(slots_s, x1_s, gate_s, ybuf, tc=n_s)
    return y_p, y_s


def _head_perm():
    return [g * Q_PER_KV + r for r in range(Q_PER_KV) for g in range(KV_HEADS)]


def _pack_w_in(w_in):
    d_model = w_in.shape[0]
    pm_width = d_model - ATTN_WIDTH
    s0 = ATTN_WIDTH
    s1 = s0 + 2 * KV_WIDTH
    s2 = s1 + 2 * KV_WIDTH
    s3 = s2 + 2 * KV_WIDTH
    s4 = s3 + 3 * N_HEADS
    wq = w_in[:, :s0].reshape(d_model, N_HEADS, HEAD_DIM)[:, jnp.array(_head_perm())].reshape(d_model, ATTN_WIDTH)
    wg = w_in[:, s3:s4].reshape(d_model, N_HEADS, 3).transpose(0, 2, 1).reshape(d_model, 3 * N_HEADS)
    wg = jnp.pad(wg, ((0, 0), (0, GATE_PAD - 3 * N_HEADS)))
    return jnp.concatenate([wq, w_in[:, s0:s3], w_in[:, s4:s4 + pm_width], wg], axis=1).astype(BF16)


def _cmp_weights(w_cmp_k, b_cmp_k, w_cmp_v, b_cmp_v):
    def half(o):
        wk = jnp.tile(w_cmp_k[o * CMP_STRIDE:(o + 1) * CMP_STRIDE], (1, KV_HEADS))
        wv = jnp.tile(w_cmp_v[o * CMP_STRIDE:(o + 1) * CMP_STRIDE], (1, KV_HEADS))
        return jnp.concatenate([wk, wv], axis=1)
    bias = jnp.concatenate([jnp.tile(b_cmp_k, KV_HEADS), jnp.tile(b_cmp_v, KV_HEADS)])[None, :]
    return half(0), half(1), bias


def _cmp_to_slc_t(n_cmp_pad, n_cmp, n_slc, n_slc_pad):
    i0 = jnp.arange(n_cmp_pad)[None, :] * CMP_STRIDE
    j0 = jnp.arange(n_slc_pad)[:, None] * SLC_BLOCK
    shared = jnp.minimum(i0 + CMP_LEN, j0 + SLC_BLOCK) - jnp.maximum(i0, j0)
    frac = jnp.clip(shared, 0, None).astype(F32) / CMP_LEN
    ok = (jnp.arange(n_cmp_pad)[None, :] < n_cmp) & (jnp.arange(n_slc_pad)[:, None] < n_slc)
    return jnp.where(ok, frac, 0.0).astype(BF16)


def _block_expand(n_blk_pad, n_keys):
    return (jnp.arange(n_blk_pad)[:, None] == (jnp.arange(n_keys)[None, :] // SLC_BLOCK)).astype(BF16)


def _prompt_mixer(x_prompt, p):
    batch, seq, d_model = x_prompt.shape
    n = batch * seq
    q, kvc, kvs, kvw, gate, u, dpool, ks_b, vs_t, kw_b, vw_t = _inproj(
        x_prompt.reshape(n, d_model), p["g1"], p["w_in"], p["qg"], p["ksg"], p["kwg"], tm=512, seq_len=seq)
    npg = seq // PAGE_SIZE
    pt = (jnp.arange(batch, dtype=I32)[:, None] * npg + jnp.arange(npg, dtype=I32)[None, :])
    kc, vc = _compress(pt, kvc.reshape(batch * npg, PAGE_SIZE, 2 * KV_WIDTH), p["cw0"], p["cw1"], p["cb"], p["kcg"],
                       pages_per_step=8)
    n_cmp = seq // CMP_STRIDE - 1
    n_slc = -(-seq // SLC_BLOCK)
    ct = _cmp_to_slc_t(kc.shape[1], n_cmp, n_slc, LANES)
    ext = _block_expand(LANES, seq).T

    o_attn = _prompt_attn(q, gate, kc, jnp.swapaxes(vc, 1, 2), ks_b, vs_t, kw_b, vw_t, ct, ext, batch=batch, seq=seq)
    return o_attn, dpool, kvc, kvs, kvw, u


def _sample_mixer(x_sample, cache_cmp, cache_slc, state_win, state_pool, page_table, p):
    nb, t, d_model = x_sample.shape
    assert t == 1, "decode path handles one new row per sequence"
    npg = page_table.shape[1]
    past = npg * PAGE_SIZE
    qpos = past
    q, kvc, kvs, kvw, gate, u, dpool = _inproj(
        x_sample.reshape(nb, d_model), p["g1"], p["w_in"], p["qg"], p["ksg"], p["kwg"], tm=nb,
        pool_state=state_pool, hist_pos=qpos)
    def rows_on_lanes(a):
        return jnp.transpose(a, (0, 2, 3, 4, 1)).reshape(a.shape[0], 2 * KV_WIDTH, a.shape[1])

    cw0_t = jnp.tile(p["cw0"].T, (1, PAGE_SIZE // CMP_STRIDE))
    cw1_t = jnp.tile(p["cw1"].T, (1, PAGE_SIZE // CMP_STRIDE))
    a0, a1 = _compress_t(page_table, rows_on_lanes(cache_cmp), cw0_t, cw1_t, pages_per_step=LANES // CMP_SUB_PER_PAGE)
    n_cmp = (past + t) // CMP_STRIDE - 1
    n_slc = -(-(past + t) // SLC_BLOCK)
    n_pad = -(-n_slc // LANES) * LANES
    cmat = _cmp_to_slc_t(a0.shape[2], n_cmp, n_slc, n_pad).T
    slopes = jnp.array([SLOPES[g * Q_PER_KV + r] * LOG2E for r in range(Q_PER_KV) for g in range(KV_HEADS)], F32)[:, None]
    q3 = q.reshape(nb, 1, ATTN_WIDTH)
    oc, member = _decode_cmp(q3, slopes, a0, a1, p["cb"].T, p["kcg"][:, :HEAD_DIM].T, cmat, qpos=qpos, n_slc=n_slc)
    rows = jnp.arange(3 * N_HEADS)
    k_i, g_i, r_i = rows // N_HEADS, (rows % N_HEADS) // Q_PER_KV, rows % Q_PER_KV
    col_head = k_i * N_HEADS + r_i * KV_HEADS + g_i
    gexp = (jnp.arange(3 * ATTN_WIDTH)[None, :] // HEAD_DIM == col_head[:, None])
    gexp = jnp.pad(gexp, ((0, GATE_PAD - 3 * N_HEADS), (0, 0))).astype(BF16)
    o = _decode_sw(page_table, q3, slopes, oc, member, gate.reshape(nb, 1, GATE_PAD), gexp,
                   rows_on_lanes(state_win), kvs.reshape(nb, 1, 2 * KV_WIDTH), kvw.reshape(nb, 1, 2 * KV_WIDTH),
                   rows_on_lanes(cache_slc), pages_per_step=16, qpos=qpos, past=past)
    return o.reshape(nb, ATTN_WIDTH), dpool, kvc, kvs, kvw, u


def _prep_params(norm1_g, w_in, q_gain, k_cmp_gain, k_slc_gain, k_win_gain, w_cmp_k, b_cmp_k, w_cmp_v, b_cmp_v):
    cw0, cw1, cb = _cmp_weights(w_cmp_k, b_cmp_k, w_cmp_v, b_cmp_v)
    return {
        "g1": norm1_g[None, :],
        "w_in": _pack_w_in(w_in),
        "qg": jnp.tile(q_gain, N_HEADS)[None, :],
        "ksg": jnp.tile(k_slc_gain, KV_HEADS)[None, :],
        "kwg": jnp.tile(k_win_gain, KV_HEADS)[None, :],
        "kcg": jnp.tile(k_cmp_gain, KV_HEADS)[None, :],
        "cw0": cw0, "cw1": cw1, "cb": cb,
    }


def _prep_out_params(w_pm, s_pm, w_out, norm2_g, w_router, b_router):
    d_model = w_out.shape[0]
    woa = w_out[:ATTN_WIDTH].reshape(N_HEADS, HEAD_DIM, d_model)[jnp.array(_head_perm())].reshape(ATTN_WIDTH, d_model)
    wr = jnp.pad(w_router, ((0, 0), (0, LANES - w_router.shape[1])))
    wrh, wrl = _split_bf16(wr)
    return (w_pm.astype(BF16), s_pm.reshape(1, -1), woa.astype(BF16), w_out[ATTN_WIDTH:].astype(BF16),
            norm2_g[None, :], wrh, wrl, jnp.pad(b_router, (0, LANES - b_router.shape[0]))[None, :])


def _layer(x_prompt, x_sample, cache_cmp, cache_slc, state_win, state_pool, page_table,
           norm1_g, w_in, q_gain, k_cmp_gain, k_slc_gain, k_win_gain, w_cmp_k, b_cmp_k, w_cmp_v, b_cmp_v,
           w_pm, s_pm, w_out, norm2_g, w_router, b_router, w_gu, b_gu, w_down, b_down):
    batch, seq, d_model = x_prompt.shape
    nb, t = x_sample.shape[:2]
    p = _prep_params(norm1_g, w_in, q_gain, k_cmp_gain, k_slc_gain, k_win_gain, w_cmp_k, b_cmp_k, w_cmp_v, b_cmp_v)
    oa_p, d_p, kvc_p, kvs_p, kvw_p, u_p = _prompt_mixer(x_prompt, p)
    oa_s, d_s, kvc_s, kvs_s, kvw_s, u_s = _sample_mixer(x_sample, cache_cmp, cache_slc, state_win, state_pool, page_table, p)
    op = _prep_out_params(w_pm, s_pm, w_out, norm2_g, w_router, b_router)
    x1_p, h2_p, te_p, gt_p = _outproj(x_prompt.reshape(batch * seq, d_model), oa_p, d_p, *op, tm=512)
    x1_s, h2_s, te_s, gt_s = _outproj(x_sample.reshape(nb * t, d_model), oa_s, d_s, *op, tm=nb * t)
    y_p, y_s = _moe(h2_p, h2_s, te_p, te_s, gt_p, gt_s, x1_p, x1_s, w_gu, b_gu, w_down, b_down)
    kv_shape = (2, KV_HEADS, HEAD_DIM)
    n_win = min(WINDOW, seq)
    st_p = (kvc_p.reshape(batch, seq, *kv_shape), kvs_p.reshape(batch, seq, *kv_shape),
            kvw_p.reshape(batch, seq, *kv_shape)[:, seq - n_win:], u_p.reshape(batch, seq, -1)[:, seq - POOL_HIST:])
    kvw_s5 = kvw_s.reshape(nb, t, *kv_shape)
    u_s3 = u_s.reshape(nb, t, -1)
    st_s = (kvc_s.reshape(nb, t, *kv_shape), kvs_s.reshape(nb, t, *kv_shape),
            jnp.concatenate([state_win, kvw_s5], axis=1)[:, t:], jnp.concatenate([state_pool, u_s3], axis=1)[:, t:])
    return y_p.reshape(batch, seq, d_model), y_s.reshape(nb, t, d_model), st_p, st_s


def kernel(x_prompt, x_sample, cache_cmp_kv, cache_slc_kv, state_win_kv, state_pool, page_table, norm1_g, w_in, q_gain, k_cmp_gain, k_slc_gain, k_win_gain, w_cmp_k, b_cmp_k, w_cmp_v, b_cmp_v, w_pm, s_pm, w_out, norm2_g, w_router, b_router, w_gu, b_gu, w_down, b_down):
    layer_params = (norm1_g, w_in, q_gain, k_cmp_gain, k_slc_gain, k_win_gain, w_cmp_k, b_cmp_k, w_cmp_v, b_cmp_v,
                    w_pm, s_pm, w_out, norm2_g, w_router, b_router, w_gu, b_gu, w_down, b_down)
    y_p, y_s = x_prompt, x_sample
    p_states, s_states = [], []
    for layer in range(norm1_g.shape[0]):
        lw = [w[layer] for w in layer_params]
        y_p, y_s, st_p, st_s = _layer(y_p, y_s, cache_cmp_kv[layer], cache_slc_kv[layer], state_win_kv[layer],
                                      state_pool[layer], page_table, *lw)
        p_states.append(st_p)
        s_states.append(st_s)
    stack = lambda states, i: jnp.stack([s[i] for s in states], axis=0)
    return (y_p, y_s,
            stack(p_states, 0), stack(p_states, 1), stack(p_states, 2), stack(p_states, 3),
            stack(s_states, 0), stack(s_states, 1), stack(s_states, 2), stack(s_states, 3))
```

```python
import functools
import math

import jax
import jax.numpy as jnp
from jax import lax
from jax.experimental import pallas as pl
from jax.experimental.pallas import tpu as pltpu

F32 = jnp.float32
BF16 = jnp.bfloat16
I32 = jnp.int32

N_HEADS = 16
HEAD_DIM = 64
KV_HEADS = 4
Q_PER_KV = N_HEADS // KV_HEADS
ATTN_WIDTH = N_HEADS * HEAD_DIM
KV_WIDTH = KV_HEADS * HEAD_DIM
CMP_LEN = 32
CMP_STRIDE = 16
SLC_BLOCK = 64
N_SELECT = 16
WINDOW = 512
FORCE_SCORE = 1e4
NEG_INF = -1e30
POOL_WINDOWS = (2, 4, 8, 16)
MAX_POOL_W = max(POOL_WINDOWS)
POOL_HIST = MAX_POOL_W - 1
N_EXPERTS = 32
TOP_K = 4
SWIGLU_LIMIT = 7.0
SWIGLU_ALPHA = 1.702
EPS = 1e-6
PAGE_SIZE = 128

LANES = 128
VMEM_LIMIT = 56 * 1024 * 1024

GATE_PAD = LANES
SLOPES = [2.0 ** (-8.0 * (h + 1) / N_HEADS) for h in range(N_HEADS)]
LOG2E = math.log2(math.e)


def _cparams(sem):
    return pltpu.CompilerParams(dimension_semantics=sem, vmem_limit_bytes=VMEM_LIMIT)


def _iota(shape, dim):
    return lax.broadcasted_iota(I32, shape, dim)


def _split_bf16(x):
    hi = x.astype(BF16)
    lo = (x - hi.astype(F32)).astype(BF16)
    return hi, lo


def _dot(a, b):
    return jnp.dot(a, b, preferred_element_type=F32)


def _dot_t(a, b):
    return lax.dot_general(a, b, (((1,), (1,)), ((), ())), preferred_element_type=F32)


def _head_mean_sq(z):
    m, w = z.shape
    ones_bd = jnp.where(_iota((256, 256), 0) // HEAD_DIM == _iota((256, 256), 1) // HEAD_DIM, 1.0, 0.0).astype(BF16)
    zz = z * z
    hi, lo = _split_bf16(zz)
    parts = []
    for c in range(w // 256):
        sl = slice(c * 256, (c + 1) * 256)
        parts.append(_dot(hi[:, sl], ones_bd) + _dot(lo[:, sl], ones_bd))
    ss = parts[0] if len(parts) == 1 else jnp.concatenate(parts, axis=1)
    return ss * (1.0 / HEAD_DIM)


C_Q = 0
C_KVC = ATTN_WIDTH
C_KVS = C_KVC + 2 * KV_WIDTH
C_KVW = C_KVS + 2 * KV_WIDTH
C_U = C_KVW + 2 * KV_WIDTH


def _inproj_kernel(x_ref, g1_ref, w_ref, qg_ref, ksg_ref, kwg_ref, *refs, tm, pm_width, seq_tiles, hist_pos):
    if seq_tiles is None:
        sp_ref, refs = refs[0], refs[1:]
    q_out, kvc_out, kvs_out, kvw_out, gate_out, u_out, d_out = refs[:7]
    x = x_ref[...]
    ms = jnp.mean(x * x, axis=-1, keepdims=True)
    h = (x * lax.rsqrt(ms + EPS)) * g1_ref[...]
    hb = h.astype(BF16)
    c_gate = C_U + pm_width

    zq = _dot(hb, w_ref[:, C_Q:C_KVC])
    qn = (zq * lax.rsqrt(_head_mean_sq(zq) + EPS)) * qg_ref[...]
    q_out[...] = (qn * (HEAD_DIM ** -0.5 * LOG2E)).astype(BF16)

    kvc_out[...] = _dot(hb, w_ref[:, C_KVC:C_KVS])

    zs = _dot(hb, w_ref[:, C_KVS:C_KVW])
    ks = zs[:, :KV_WIDTH]
    ks = (ks * lax.rsqrt(_head_mean_sq(ks) + EPS)) * ksg_ref[...]
    kvs = jnp.concatenate([ks, zs[:, KV_WIDTH:]], axis=1)
    kvs_out[...] = kvs

    zw = _dot(hb, w_ref[:, C_KVW:C_U])
    kw = zw[:, :KV_WIDTH]
    kw = (kw * lax.rsqrt(_head_mean_sq(kw) + EPS)) * kwg_ref[...]
    kvw = jnp.concatenate([kw, zw[:, KV_WIDTH:]], axis=1)
    kvw_out[...] = kvw
    if seq_tiles is not None:
        ksb_out, vst_out, kwb_out, vwt_out = refs[7:11]
        ksb_out[...] = ks.astype(BF16)
        vst_out[...] = zs[:, KV_WIDTH:].T.astype(BF16)
        kwb_out[...] = kw.astype(BF16)
        vw_t = zw[:, KV_WIDTH:].T.astype(BF16)
        for j in range(tm // TKW):
            vwt_out[j] = vw_t[:, j * TKW:(j + 1) * TKW]

    gate_out[...] = jax.nn.sigmoid(_dot(hb, w_ref[:, c_gate:c_gate + GATE_PAD]))

    u = _dot(hb, w_ref[:, C_U:c_gate])
    u_out[...] = u

    pm_group = pm_width // len(POOL_WINDOWS)
    if seq_tiles is None:
        tpos = float(hist_pos + 1)
        for gi, w in enumerate(POOL_WINDOWS):
            cs = slice(gi * pm_group, (gi + 1) * pm_group)
            s = u[:, cs]
            for k in range(1, w):
                s = s + sp_ref[:, POOL_HIST - k, cs]
            d_out[:, cs] = (s / min(float(w), tpos) - u[:, cs]).astype(BF16)
    else:
        ext_ref = refs[11]
        j = pl.program_id(0) % seq_tiles

        @pl.when(j == 0)
        def _():
            ext_ref[0:MAX_POOL_W, :] = jnp.zeros((MAX_POOL_W, pm_width), F32)

        ext_ref[MAX_POOL_W:MAX_POOL_W + tm, :] = u
        tpos = (j * tm + _iota((tm, 1), 0) + 1).astype(F32)
        for gi, w in enumerate(POOL_WINDOWS):
            cs = slice(gi * pm_group, (gi + 1) * pm_group)
            s = ext_ref[MAX_POOL_W:MAX_POOL_W + tm, cs]
            for k in range(1, w):
                s = s + ext_ref[MAX_POOL_W - k:MAX_POOL_W - k + tm, cs]
            cnt = jnp.minimum(float(w), tpos)
            d_out[:, cs] = (s / cnt - u[:, cs]).astype(BF16)
        ext_ref[0:MAX_POOL_W, :] = ext_ref[tm:tm + MAX_POOL_W, :]


def _inproj(x2d, g1, w_packed, qg, ksg, kwg, *, tm, seq_len=None, pool_state=None, hist_pos=None):
    n, d_model = x2d.shape
    pm_width = d_model - ATTN_WIDTH
    seq_tiles = None if seq_len is None else seq_len // tm
    row = lambda w: pl.BlockSpec((tm, w), lambda i: (i, 0))
    full = lambda a: pl.BlockSpec(a.shape, lambda i: (0,) * a.ndim, pipeline_mode=pl.Buffered(1))
    out_shape = [
        jax.ShapeDtypeStruct((n, ATTN_WIDTH), BF16),
        jax.ShapeDtypeStruct((n, 2 * KV_WIDTH), F32),
        jax.ShapeDtypeStruct((n, 2 * KV_WIDTH), F32),
        jax.ShapeDtypeStruct((n, 2 * KV_WIDTH), F32),
        jax.ShapeDtypeStruct((n, GATE_PAD), F32),
        jax.ShapeDtypeStruct((n, pm_width), F32),
        jax.ShapeDtypeStruct((n, pm_width), BF16),
    ]
    out_specs = [row(ATTN_WIDTH), row(2 * KV_WIDTH), row(2 * KV_WIDTH), row(2 * KV_WIDTH),
                 row(GATE_PAD), row(pm_width), row(pm_width)]
    in_specs = [row(d_model), full(g1), full(w_packed), full(qg), full(ksg), full(kwg)]
    args = [x2d, g1, w_packed, qg, ksg, kwg]
    scratch = []
    if seq_tiles is None:
        in_specs.append(pl.BlockSpec((tm, POOL_HIST, pm_width), lambda i: (i, 0, 0)))
        args.append(pool_state)
    else:
        per_tile = TK // tm
        out_shape += [jax.ShapeDtypeStruct((n, KV_WIDTH), BF16), jax.ShapeDtypeStruct((n // TK, KV_WIDTH, TK), BF16),
                      jax.ShapeDtypeStruct((n, KV_WIDTH), BF16), jax.ShapeDtypeStruct((n // TKW, KV_WIDTH, TKW), BF16)]
        out_specs += [row(KV_WIDTH), pl.BlockSpec((None, KV_WIDTH, tm), lambda i: (i // per_tile, 0, i % per_tile)),
                      row(KV_WIDTH), pl.BlockSpec((tm // TKW, KV_WIDTH, TKW), lambda i: (i, 0, 0))]
        scratch.append(pltpu.VMEM((tm + MAX_POOL_W, pm_width), F32))
    return pl.pallas_call(
        functools.partial(_inproj_kernel, tm=tm, pm_width=pm_width, seq_tiles=seq_tiles, hist_pos=hist_pos),
        grid=(n // tm,),
        in_specs=in_specs,
        out_specs=out_specs,
        out_shape=out_shape,
        scratch_shapes=scratch,
        compiler_params=_cparams(("arbitrary",)),
        name="inproj",
    )(*args)


CMP_SUB_PER_PAGE = PAGE_SIZE // CMP_STRIDE


def _compress_kernel(pt_ref, *refs, pages_per_step):
    del pt_ref
    p = pages_per_step
    page_refs = refs[:p]
    halo_ref, w0_ref, w1_ref, b_ref, kg_ref, kc_out, vc_out, xs_ref = refs[p:]
    n_lt = 2 * KV_WIDTH // LANES
    nb = p * CMP_SUB_PER_PAGE
    parts = []
    for c in range(n_lt):
        cs = slice(c * LANES, (c + 1) * LANES)
        for k in range(p):
            xs_ref[c, k * PAGE_SIZE:(k + 1) * PAGE_SIZE, :] = page_refs[k][:, cs]
        xs_ref[c, p * PAGE_SIZE:p * PAGE_SIZE + CMP_STRIDE, :] = halo_ref[:, cs]
        acc = jnp.zeros((nb, LANES), F32) + b_ref[:, cs]
        for j in range(CMP_STRIDE):
            acc = acc + xs_ref[c, pl.ds(j, nb, stride=CMP_STRIDE), :] * w0_ref[j:j + 1, cs]
            acc = acc + xs_ref[c, pl.ds(CMP_STRIDE + j, nb, stride=CMP_STRIDE), :] * w1_ref[j:j + 1, cs]
        parts.append(acc)
    acc = jnp.concatenate(parts, axis=1)
    kc = acc[:, :KV_WIDTH]
    kc = (kc * lax.rsqrt(_head_mean_sq(kc) + EPS)) * kg_ref[...]
    kc_out[...] = kc.astype(BF16)
    vc_out[...] = acc[:, KV_WIDTH:].astype(BF16)


def _compress(page_table, pages, w0t, w1t, bias, kgain, *, pages_per_step):
    nb, npg = page_table.shape
    p = pages_per_step
    steps = npg // p

    def page_spec(k):
        return pl.BlockSpec((None, PAGE_SIZE, 2 * KV_WIDTH), lambda b, i, pt: (pt[b, i * p + k], 0, 0))

    halo_spec = pl.BlockSpec((None, CMP_STRIDE, 2 * KV_WIDTH),
                             lambda b, i, pt: (pt[b, jnp.minimum(i * p + p, npg - 1)], 0, 0))
    full = lambda a: pl.BlockSpec(a.shape, lambda b, i, pt: (0,) * a.ndim)
    out_spec = pl.BlockSpec((None, p * CMP_SUB_PER_PAGE, KV_WIDTH), lambda b, i, pt: (b, i, 0))
    grid_spec = pltpu.PrefetchScalarGridSpec(
        num_scalar_prefetch=1,
        grid=(nb, steps),
        in_specs=[page_spec(k) for k in range(p)] + [halo_spec, full(w0t), full(w1t), full(bias), full(kgain)],
        out_specs=[out_spec, out_spec],
        scratch_shapes=[pltpu.VMEM((2 * KV_WIDTH // LANES, p * PAGE_SIZE + CMP_STRIDE, LANES), F32)],
    )
    return pl.pallas_call(
        functools.partial(_compress_kernel, pages_per_step=p),
        grid_spec=grid_spec,
        out_shape=[jax.ShapeDtypeStruct((nb, npg * CMP_SUB_PER_PAGE, KV_WIDTH), BF16)] * 2,
        compiler_params=_cparams(("arbitrary", "arbitrary")),
        name="compress",
    )(page_table, *([pages] * p), pages, w0t, w1t, bias, kgain)


TQ = 128
TK = 512
TKW = 128
QROWS = Q_PER_KV * TQ


def _select_members_t(score_t, n_cand):
    rows = _iota((n_cand, 1), 0)
    rank = jnp.zeros(score_t.shape, F32)
    for i in range(n_cand):
        si = score_t[i:i + 1, :]
        ahead = jnp.where(si > score_t, 1.0, jnp.where(si == score_t, jnp.where(rows > i, 1.0, 0.0), 0.0))
        rank = rank + ahead
    return jnp.where(rank < float(N_SELECT), 1.0, 0.0)


def _prompt_attn_kernel(q_ref, gate_ref, kc_ref, vct_ref, ks_ref, vst_ref, kw_ref, vwt_ref, ct_ref, ext_ref,
                        o_ref, qpad_ref, memb_ref, tot_ref, *stat_refs, n_slc):
    qi = pl.program_id(1)
    q0 = qi * TQ
    lane = _iota((1, QROWS), 1)
    r_lane = lane // TQ
    qidx = q0 + lane % TQ
    qpos_l = q0 + _iota((1, TQ), 1)
    lane_g = _iota((1, KV_WIDTH), 1) // HEAD_DIM
    n_cmp_pad = kc_ref.shape[0]
    n_kt = memb_ref.shape[1]
    gates_t = gate_ref[...].T

    def slope_row(g):
        row = jnp.zeros((1, QROWS), F32)
        for r in range(Q_PER_KV):
            row = jnp.where(r_lane == r, SLOPES[g * Q_PER_KV + r] * LOG2E, row)
        return row

    def gate_row(g, k):
        c = k * N_HEADS + g * Q_PER_KV
        return jnp.concatenate([gates_t[c + r:c + r + 1, :] for r in range(Q_PER_KV)], axis=1)

    c_start = _iota((n_cmp_pad, QROWS), 0) * CMP_STRIDE
    c_mid = c_start.astype(F32) + 0.5 * (CMP_LEN - 1)
    cmask = (c_start + (CMP_LEN - 1)) <= qidx
    kidx = _iota((TK, TQ), 0)

    for g in range(KV_HEADS):
        qpad = jnp.concatenate(
            [jnp.where(lane_g == g, q_ref[:, r * KV_WIDTH:(r + 1) * KV_WIDTH], jnp.zeros((), BF16))
             for r in range(Q_PER_KV)], axis=0)
        qpad_ref[g] = qpad
        s = _dot_t(kc_ref[...], qpad) + slope_row(g) * c_mid
        s = jnp.where(cmask, s, NEG_INF)
        e = jnp.where(cmask, jnp.exp2(s - jnp.max(s, axis=0, keepdims=True)), 0.0)
        l = jnp.sum(e, axis=0, keepdims=True)
        pc = e * jnp.where(l > 0.0, 1.0 / l, 0.0)
        oc = _dot(vct_ref[g * HEAD_DIM:(g + 1) * HEAD_DIM, :], pc.astype(BF16))
        tot_ref[g] = oc * gate_row(g, 0)
        p_sum = pc[:, 0:TQ]
        for r in range(1, Q_PER_KV):
            p_sum = p_sum + pc[:, r * TQ:(r + 1) * TQ]
        hi, lo = _split_bf16(p_sum)
        imp_t = (_dot(ct_ref[...], hi) + _dot(ct_ref[...], lo))[0:n_slc]
        blk = _iota((n_slc, 1), 0)
        qblk = qpos_l // SLC_BLOCK
        forced = (blk == 0) | (blk == qblk) | (blk == qblk - 1)
        score_t = jnp.where(forced, FORCE_SCORE, jnp.where(blk * SLC_BLOCK <= qpos_l, imp_t, NEG_INF))
        member_t = _select_members_t(score_t, n_slc)
        penalty = jnp.where(member_t > 0.5, 0.0, NEG_INF)
        penalty = jnp.concatenate([penalty, jnp.zeros((LANES - n_slc, TQ), F32)], axis=0).astype(BF16)
        memb_add = _dot(ext_ref[...], penalty)
        for j in range(n_kt):
            memb_ref[g, j] = memb_add[j * TK:(j + 1) * TK, :]
        jd = q0 // TK
        memb_ref[g, jd] = jnp.where(kidx + jd * TK <= qpos_l, memb_ref[g, jd], NEG_INF)

    m_refs, l_refs, acc_refs = (stat_refs[i * KV_HEADS:(i + 1) * KV_HEADS] for i in range(3))

    def sweep(k_ref, vt_ref, lo_t, hi_t, shared_fn, mask_fn, gate_k):
        for g in range(KV_HEADS):
            m_refs[g][...] = jnp.full(m_refs[g].shape, NEG_INF, F32)
            l_refs[g][...] = jnp.zeros(l_refs[g].shape, F32)
            acc_refs[g][...] = jnp.zeros(acc_refs[g].shape, F32)

        def body(kj, carry):
            k0 = pl.multiple_of(kj * TK, TK)
            kt = k_ref[pl.ds(k0, TK), :]
            vt = vt_ref[kj]
            kpos = (kidx + k0).astype(F32)
            shared = shared_fn(k0)
            for g in range(KV_HEADS):
                s_all = _dot_t(kt, qpad_ref[g])
                mask = mask_fn(g, kj, shared)
                vg = vt[g * HEAD_DIM:(g + 1) * HEAD_DIM, :]
                for r in range(Q_PER_KV):
                    cs = slice(r * TQ, (r + 1) * TQ)
                    s = s_all[:, cs] + (SLOPES[g * Q_PER_KV + r] * LOG2E) * kpos + mask
                    m_old = m_refs[g][:, cs]
                    m_new = jnp.maximum(m_old, jnp.max(s, axis=0, keepdims=True))
                    alpha = jnp.exp2(m_old - m_new)
                    p = jnp.exp2(s - m_new)
                    l_refs[g][:, cs] = alpha * l_refs[g][:, cs] + jnp.sum(p, axis=0, keepdims=True)
                    acc_refs[g][:, cs] = alpha * acc_refs[g][:, cs] + _dot(vg, p.astype(BF16))
                    m_refs[g][:, cs] = m_new
            return carry

        lax.fori_loop(lo_t, hi_t, body, 0)
        for g in range(KV_HEADS):
            l = l_refs[g][...]
            tot_ref[g] = tot_ref[g] + acc_refs[g][...] * (jnp.where(l > 0.0, 1.0 / l, 0.0) * gate_row(g, gate_k))

    n_hi = (q0 + TQ + TK - 1) // TK
    sweep(ks_ref, vst_ref, 0, n_hi, lambda k0: None, lambda g, kj, shared: memb_ref[g, kj], 1)

    n_wt = (WINDOW + TQ) // TKW
    nwk = n_wt * TKW
    j0 = jnp.maximum(q0 - WINDOW, 0) // TKW
    w0 = pl.multiple_of(j0 * TKW, TKW)
    kt = kw_ref[pl.ds(w0, nwk), :]
    vt = jnp.concatenate([vwt_ref[j0 + i] for i in range(n_wt)], axis=1)
    d = _iota((nwk, TQ), 0) - _iota((nwk, TQ), 1) + (w0 - q0)
    wmask = jnp.where(d <= 0, jnp.where(d > -WINDOW, 0.0, NEG_INF), NEG_INF)
    wpos = (_iota((nwk, TQ), 0) + w0).astype(F32)
    for g in range(KV_HEADS):
        s_all = _dot_t(kt, qpad_ref[g])
        vg = vt[g * HEAD_DIM:(g + 1) * HEAD_DIM, :]
        for r in range(Q_PER_KV):
            cs = slice(r * TQ, (r + 1) * TQ)
            s = s_all[:, cs] + (SLOPES[g * Q_PER_KV + r] * LOG2E) * wpos + wmask
            e = jnp.exp2(s - jnp.max(s, axis=0, keepdims=True))
            scale = gate_row(g, 2)[:, cs] / jnp.sum(e, axis=0, keepdims=True)
            tot_ref[g, :, cs] = tot_ref[g, :, cs] + _dot(vg, e.astype(BF16)) * scale

    total = jnp.concatenate([tot_ref[g] for g in range(KV_HEADS)], axis=0)
    for r in range(Q_PER_KV):
        o_ref[:, r * KV_WIDTH:(r + 1) * KV_WIDTH] = total[:, r * TQ:(r + 1) * TQ].T.astype(BF16)


def _prompt_attn(q, gate, kc, vct, kvs_b, vst, kvw_b, vwt, ct, ext, *, batch, seq):
    n_slc = -(-seq // SLC_BLOCK)
    n_cmp_pad = kc.shape[1]
    n_qt = seq // TQ
    n_kt = seq // TK
    rowblk = lambda w: pl.BlockSpec((TQ, w), lambda b, i: (b * n_qt + i, 0))
    kblk = pl.BlockSpec((seq, KV_WIDTH), lambda b, i: (b, 0))
    assert seq >= WINDOW + TQ
    vtblk = pl.BlockSpec((n_kt, KV_WIDTH, TK), lambda b, i: (b, 0, 0))
    vwblk = pl.BlockSpec((seq // TKW, KV_WIDTH, TKW), lambda b, i: (b, 0, 0))
    full = lambda a: pl.BlockSpec(a.shape, lambda b, i: (0,) * a.ndim)
    return pl.pallas_call(
        functools.partial(_prompt_attn_kernel, n_slc=n_slc),
        grid=(batch, n_qt),
        in_specs=[rowblk(ATTN_WIDTH), rowblk(GATE_PAD),
                  pl.BlockSpec((None, n_cmp_pad, KV_WIDTH), lambda b, i: (b, 0, 0)),
                  pl.BlockSpec((None, KV_WIDTH, n_cmp_pad), lambda b, i: (b, 0, 0)),
                  kblk, vtblk, kblk, vwblk, full(ct), full(ext)],
        out_specs=rowblk(ATTN_WIDTH),
        out_shape=jax.ShapeDtypeStruct((batch * seq, ATTN_WIDTH), BF16),
        scratch_shapes=[pltpu.VMEM((KV_HEADS, QROWS, KV_WIDTH), BF16),
                        pltpu.VMEM((KV_HEADS, n_kt, TK, TQ), F32),
                        pltpu.VMEM((KV_HEADS, HEAD_DIM, QROWS), F32)]
                       + [pltpu.VMEM((1, QROWS), F32)] * (2 * KV_HEADS) + [pltpu.VMEM((HEAD_DIM, QROWS), F32)] * KV_HEADS,
        compiler_params=_cparams(("arbitrary", "arbitrary")),
        name="prompt_attn",
    )(q, gate, kc, vct, kvs_b, vst, kvw_b, vwt, ct, ext)


DROWS = N_HEADS
NEVER = -3e38


def _decode_qpad(q_ref):
    lane_g = _iota((KV_HEADS, KV_WIDTH), 1) // HEAD_DIM
    row_g = _iota((KV_HEADS, KV_WIDTH), 0)
    parts = []
    for r in range(Q_PER_KV):
        qr = jnp.broadcast_to(q_ref[:, r * KV_WIDTH:(r + 1) * KV_WIDTH].astype(F32), (KV_HEADS, KV_WIDTH))
        parts.append(jnp.where(lane_g == row_g, qr, 0.0))
    return jnp.concatenate(parts, axis=0).astype(BF16)


def _compress_t_kernel(pt_ref, *refs, pages_per_step):
    del pt_ref
    p = pages_per_step
    page_refs = refs[:p]
    w0_ref, w1_ref, a0_out, a1_out = refs[p:]
    prow = _iota((PAGE_SIZE, LANES), 0)
    ocol = _iota((PAGE_SIZE, LANES), 1)
    a0 = jnp.zeros(a0_out.shape, F32)
    a1 = jnp.zeros(a1_out.shape, F32)
    for k in range(p):
        sel = jnp.where(ocol == k * CMP_SUB_PER_PAGE + prow // CMP_STRIDE, 1.0, 0.0).astype(BF16)
        x = page_refs[k][...]
        a0 = a0 + _dot((x * w0_ref[...]).astype(BF16), sel)
        a1 = a1 + _dot((x * w1_ref[...]).astype(BF16), sel)
    a0_out[...] = a0
    a1_out[...] = a1


def _compress_t(page_table, pages_t, w0t, w1t, *, pages_per_step):
    nb, npg = page_table.shape
    p = pages_per_step
    assert p * CMP_SUB_PER_PAGE == LANES

    def page_spec(k):
        return pl.BlockSpec((None, 2 * KV_WIDTH, PAGE_SIZE), lambda b, i, pt: (pt[b, i * p + k], 0, 0))

    full = lambda a: pl.BlockSpec(a.shape, lambda b, i, pt: (0,) * a.ndim)
    out_spec = pl.BlockSpec((None, 2 * KV_WIDTH, LANES), lambda b, i, pt: (b, 0, i))
    grid_spec = pltpu.PrefetchScalarGridSpec(
        num_scalar_prefetch=1,
        grid=(nb, npg // p),
        in_specs=[page_spec(k) for k in range(p)] + [full(w0t), full(w1t)],
        out_specs=[out_spec, out_spec],
    )
    return pl.pallas_call(
        functools.partial(_compress_t_kernel, pages_per_step=p),
        grid_spec=grid_spec,
        out_shape=[jax.ShapeDtypeStruct((nb, 2 * KV_WIDTH, npg * CMP_SUB_PER_PAGE), F32)] * 2,
        compiler_params=_cparams(("arbitrary", "arbitrary")),
        name="compress_t",
    )(page_table, *([pages_t] * p), w0t, w1t)


def _decode_cmp_kernel(q_ref, slope_ref, a0_ref, a1_ref, cb_ref, kg_ref, c_ref, oc_out, member_out, *, qpos, n_slc):
    qpad = _decode_qpad(q_ref)
    slope = slope_ref[...]
    n_cmp_pad = a0_ref.shape[1]
    acc = a0_ref[...] + pltpu.roll(a1_ref[...], n_cmp_pad - 1, 1) + cb_ref[...]
    kparts = []
    for g in range(KV_HEADS):
        kg = acc[g * HEAD_DIM:(g + 1) * HEAD_DIM, :]
        ms = jnp.mean(kg * kg, axis=0, keepdims=True)
        kparts.append((kg * lax.rsqrt(ms + EPS)) * kg_ref[...])
    kc_t = jnp.concatenate(kparts, axis=0).astype(BF16)
    vc_t = acc[KV_WIDTH:, :].astype(BF16)
    c_start = _iota((1, n_cmp_pad), 1) * CMP_STRIDE
    c_mid = c_start.astype(F32) + 0.5 * (CMP_LEN - 1)
    cmask = (c_start + (CMP_LEN - 1)) <= qpos
    s = _dot(qpad, kc_t) + slope * c_mid
    s = jnp.where(cmask, s, NEG_INF)
    e = jnp.where(cmask, jnp.exp2(s - jnp.max(s, axis=-1, keepdims=True)), 0.0)
    l = jnp.sum(e, axis=-1, keepdims=True)
    pc = e * jnp.where(l > 0.0, 1.0 / l, 0.0)
    oc_out[...] = _dot_t(pc.astype(BF16), vc_t)
    p_sum = pc[0:KV_HEADS]
    for r in range(1, Q_PER_KV):
        p_sum = p_sum + pc[r * KV_HEADS:(r + 1) * KV_HEADS]
    p_sum = jnp.concatenate([p_sum, jnp.zeros((DROWS - KV_HEADS, n_cmp_pad), F32)], axis=0)
    hi, lo = _split_bf16(p_sum)
    imp = _dot(hi, c_ref[...]) + _dot(lo, c_ref[...])
    n_pad = imp.shape[1]
    blk = _iota((1, n_pad), 1)
    qblk = qpos // SLC_BLOCK
    forced = (blk == 0) | (blk == qblk) | (blk == qblk - 1)
    score = jnp.where(forced, FORCE_SCORE, jnp.where(blk * SLC_BLOCK <= qpos, imp, NEG_INF))
    score = jnp.where(blk < n_slc, score, NEVER)
    rank = jnp.zeros(score.shape, F32)
    for i in range(n_slc):
        si = score[:, i:i + 1]
        rank = rank + jnp.where(si > score, 1.0, jnp.where(si == score, jnp.where(blk > i, 1.0, 0.0), 0.0))
    member_out[...] = jnp.where(rank < float(min(N_SELECT, n_slc)), 1.0, 0.0)


def _decode_cmp(q3, slopes, a0, a1, cb_col, kg_col, cmat, *, qpos, n_slc):
    nb = q3.shape[0]
    n_cmp_pad = a0.shape[2]
    n_pad = cmat.shape[1]
    per_b = lambda r, c: pl.BlockSpec((None, r, c), lambda b: (b, 0, 0))
    full = lambda a: pl.BlockSpec(a.shape, lambda b: (0,) * a.ndim)
    return pl.pallas_call(
        functools.partial(_decode_cmp_kernel, qpos=qpos, n_slc=n_slc),
        grid=(nb,),
        in_specs=[per_b(1, ATTN_WIDTH), full(slopes), per_b(2 * KV_WIDTH, n_cmp_pad), per_b(2 * KV_WIDTH, n_cmp_pad),
                  full(cb_col), full(kg_col), full(cmat)],
        out_specs=[per_b(DROWS, KV_WIDTH), per_b(DROWS, n_pad)],
        out_shape=[jax.ShapeDtypeStruct((nb, DROWS, KV_WIDTH), F32), jax.ShapeDtypeStruct((nb, DROWS, n_pad), F32)],
        compiler_params=_cparams(("arbitrary",)),
        name="decode_cmp",
    )(q3, slopes, a0, a1, cb_col, kg_col, cmat)


def _decode_sw_kernel(pt_ref, q_ref, slope_ref, oc_ref, member_ref, gate_ref, gexp_ref, win_ref, news_ref, neww_ref,
                      *refs, pages_per_step, qpos, past, n_buf):
    del pt_ref
    p = pages_per_step
    page_refs = refs[:p]
    o_ref, m_ref, l_ref, acc_ref, ow_ref = refs[p:]
    c = pl.program_id(1)
    qpad = _decode_qpad(q_ref)
    qf = qpad.astype(F32)
    slope = slope_ref[...]

    def new_row(row_ref):
        kn = row_ref[:, :KV_WIDTH].astype(BF16).astype(F32)
        vn = row_ref[:, KV_WIDTH:].astype(BF16).astype(F32)
        return jnp.sum(qf * kn, axis=-1, keepdims=True) + slope * float(qpos), vn

    @pl.when(c == 0)
    def _():
        m_ref[...] = jnp.full((DROWS, 1), NEG_INF, F32)
        l_ref[...] = jnp.zeros((DROWS, 1), F32)
        acc_ref[...] = jnp.zeros((DROWS, KV_WIDTH), F32)
        kw_t = win_ref[:KV_WIDTH, :].astype(BF16)
        vw_t = win_ref[KV_WIDTH:, :].astype(BF16)
        wpos = past - n_buf + _iota((1, n_buf), 1)
        ok = (wpos <= qpos) & (wpos > qpos - WINDOW) & (wpos >= 0)
        s = jnp.where(ok, _dot(qpad, kw_t) + slope * wpos.astype(F32), NEG_INF)
        s_new, v_new = new_row(neww_ref)
        m = jnp.maximum(jnp.max(s, axis=-1, keepdims=True), s_new)
        e = jnp.where(ok, jnp.exp2(s - m), 0.0)
        e_new = jnp.exp2(s_new - m)
        l = jnp.sum(e, axis=-1, keepdims=True) + e_new
        ow_ref[...] = (_dot_t(e.astype(BF16), vw_t) + e_new * v_new) / l

    nk = p * PAGE_SIZE
    kt_t = jnp.concatenate([page_refs[k][:KV_WIDTH, :].astype(BF16) for k in range(p)], axis=1)
    vt_t = jnp.concatenate([page_refs[k][KV_WIDTH:, :].astype(BF16) for k in range(p)], axis=1)
    member = member_ref[0:KV_HEADS, :].astype(BF16)
    member = jnp.concatenate([member] * Q_PER_KV, axis=0)
    n_pad = member.shape[1]
    kidx = c * nk + _iota((n_pad, nk), 1)
    expand = jnp.where(kidx // SLC_BLOCK == _iota((n_pad, nk), 0), 1.0, 0.0).astype(BF16)
    mk = _dot(member, expand)
    kpos = (c * nk + _iota((1, nk), 1)).astype(F32)
    s = jnp.where(mk > 0.5, _dot(qpad, kt_t) + slope * kpos, NEG_INF)
    m_old = m_ref[...]
    m_new = jnp.maximum(m_old, jnp.max(s, axis=-1, keepdims=True))
    alpha = jnp.exp2(m_old - m_new)
    pr = jnp.exp2(s - m_new)
    l_ref[...] = alpha * l_ref[...] + jnp.sum(pr, axis=-1, keepdims=True)
    acc_ref[...] = alpha * acc_ref[...] + _dot_t(pr.astype(BF16), vt_t)
    m_ref[...] = m_new

    @pl.when(c == pl.num_programs(1) - 1)
    def _():
        new_blk = qpos // SLC_BLOCK
        is_member = jnp.concatenate([member_ref[0:KV_HEADS, new_blk:new_blk + 1]] * Q_PER_KV, axis=0) > 0.5
        s_new, v_new = new_row(news_ref)
        s_new = jnp.where(is_member, s_new, NEG_INF)
        m_old = m_ref[...]
        m_new = jnp.maximum(m_old, s_new)
        alpha = jnp.exp2(m_old - m_new)
        e_new = jnp.where(is_member, jnp.exp2(s_new - m_new), 0.0)
        l = alpha * l_ref[...] + e_new
        os = (alpha * acc_ref[...] + e_new * v_new) / l

        lane_g = _iota((1, KV_WIDTH), 1) // HEAD_DIM

        def flat(o):
            segs = []
            for r in range(Q_PER_KV):
                seg = jnp.zeros((1, KV_WIDTH), F32)
                for g in range(KV_HEADS):
                    i = r * KV_HEADS + g
                    seg = seg + jnp.where(lane_g == g, o[i:i + 1, :], 0.0)
                segs.append(seg)
            return jnp.concatenate(segs, axis=1)

        ghi, glo = _split_bf16(jnp.broadcast_to(gate_ref[...], (DROWS, GATE_PAD)))
        gx = (_dot(ghi, gexp_ref[...]) + _dot(glo, gexp_ref[...]))[0:1]
        o = (gx[:, 0:ATTN_WIDTH] * flat(oc_ref[...])
             + gx[:, ATTN_WIDTH:2 * ATTN_WIDTH] * flat(os)
             + gx[:, 2 * ATTN_WIDTH:] * flat(ow_ref[...]))
        o_ref[...] = o.astype(BF16)


def _decode_sw(page_table, q3, slopes, oc, member, gate3, gexp, win_state_t, new_s, new_w, pages_t,
               *, pages_per_step, qpos, past):
    nb, npg = page_table.shape
    p = pages_per_step
    n_buf = win_state_t.shape[2]
    n_pad = member.shape[2]
    per_b = lambda r, c: pl.BlockSpec((None, r, c), lambda b, i, pt: (b, 0, 0))
    full = lambda a: pl.BlockSpec(a.shape, lambda b, i, pt: (0,) * a.ndim)

    def page_spec(k):
        return pl.BlockSpec((None, 2 * KV_WIDTH, PAGE_SIZE), lambda b, i, pt: (pt[b, i * p + k], 0, 0))

    grid_spec = pltpu.PrefetchScalarGridSpec(
        num_scalar_prefetch=1,
        grid=(nb, npg // p),
        in_specs=[per_b(1, ATTN_WIDTH), full(slopes), per_b(DROWS, KV_WIDTH), per_b(DROWS, n_pad), per_b(1, GATE_PAD),
                  full(gexp), per_b(2 * KV_WIDTH, n_buf), per_b(1, 2 * KV_WIDTH), per_b(1, 2 * KV_WIDTH)]
                 + [page_spec(k) for k in range(p)],
        out_specs=per_b(1, ATTN_WIDTH),
        scratch_shapes=[pltpu.VMEM((DROWS, 1), F32), pltpu.VMEM((DROWS, 1), F32),
                        pltpu.VMEM((DROWS, KV_WIDTH), F32), pltpu.VMEM((DROWS, KV_WIDTH), F32)],
    )
    return pl.pallas_call(
        functools.partial(_decode_sw_kernel, pages_per_step=p, qpos=qpos, past=past, n_buf=n_buf),
        grid_spec=grid_spec,
        out_shape=jax.ShapeDtypeStruct((nb, 1, ATTN_WIDTH), BF16),
        compiler_params=_cparams(("arbitrary", "arbitrary")),
        name="decode_slc_win",
    )(page_table, q3, slopes, oc, member, gate3, gexp, win_state_t, new_s, new_w, *([pages_t] * p))


def _outproj_kernel(x_ref, oa_ref, d_ref, wpm_ref, spm_ref, woa_ref, wop_ref, g2_ref, wrh_ref, wrl_ref, br_ref,
                    x1_out, h2_out, tope_out, gate_out):
    n_grp, pm_group = wpm_ref.shape[0], wpm_ref.shape[1]
    pm = jnp.concatenate([_dot(d_ref[:, gi * pm_group:(gi + 1) * pm_group], wpm_ref[gi]) for gi in range(n_grp)], axis=1)
    pm = pm * spm_ref[...]
    x1 = x_ref[...] + _dot(oa_ref[...], woa_ref[...]) + _dot(pm.astype(BF16), wop_ref[...])
    x1_out[...] = x1
    ms = jnp.mean(x1 * x1, axis=-1, keepdims=True)
    h2 = (x1 * lax.rsqrt(ms + EPS)) * g2_ref[...]
    half = h2.shape[1] // 2
    bits = lax.bitcast_convert_type(h2.astype(BF16).astype(F32), jnp.uint32)
    h2_out[...] = (bits[:, half:] & jnp.uint32(0xFFFF0000)) | (bits[:, :half] >> 16)
    hi, lo = _split_bf16(h2)
    logits = _dot(hi, wrh_ref[...]) + _dot(lo, wrh_ref[...]) + _dot(hi, wrl_ref[...]) + br_ref[...]
    tm = logits.shape[0]
    lane = _iota((tm, LANES), 1)
    logits = jnp.where(lane < N_EXPERTS, logits, NEVER)
    vals, idxs = [], []
    for _ in range(TOP_K):
        m = jnp.max(logits, axis=-1, keepdims=True)
        idx = jnp.min(jnp.where(logits == m, lane, LANES), axis=-1, keepdims=True)
        vals.append(m)
        idxs.append(idx)
        logits = jnp.where(lane == idx, NEVER, logits)
    es = [jnp.exp(v - vals[0]) for v in vals]
    den = es[0]
    for e in es[1:]:
        den = den + e
    tope = jnp.full((tm, LANES), -1, I32)
    gts = jnp.zeros((tm, LANES), F32)
    for k in range(TOP_K):
        tope = jnp.where(lane == k, idxs[k], tope)
        gts = jnp.where(lane == k, es[k] / den, gts)
    tope_out[...] = tope
    gate_out[...] = gts


def _outproj(x2d, o_attn, dpool, wpm, spm, woa, wop, g2, wrh, wrl, br, *, tm):
    n, d_model = x2d.shape
    row = lambda w: pl.BlockSpec((tm, w), lambda i: (i, 0))
    full = lambda a: pl.BlockSpec(a.shape, lambda i: (0,) * a.ndim, pipeline_mode=pl.Buffered(1))
    consts = [wpm, spm, woa, wop, g2, wrh, wrl, br]
    return pl.pallas_call(
        _outproj_kernel,
        grid=(n // tm,),
        in_specs=[row(d_model), row(o_attn.shape[1]), row(dpool.shape[1])] + [full(a) for a in consts],
        out_specs=[row(d_model), row(d_model // 2), row(LANES), row(LANES)],
        out_shape=[jax.ShapeDtypeStruct((n, d_model), F32), jax.ShapeDtypeStruct((n, d_model // 2), jnp.uint32),
                   jax.ShapeDtypeStruct((n, LANES), I32), jax.ShapeDtypeStruct((n, LANES), F32)],
        compiler_params=_cparams(("arbitrary",)),
        name="outproj",
    )(x2d, o_attn, dpool, *consts)


MOE_SUB = 128
MOE_GROUPS = (4, 2, 1)
MOE_ROWS = 10 * MOE_SUB
MOE_FC = 256
ROUTE_TILE = 512


def _route_kernel(e_ref, rank_out, cnt_out, carry_ref):
    @pl.when(pl.program_id(0) == 0)
    def _():
        carry_ref[...] = jnp.zeros(carry_ref.shape, F32)

    tr = e_ref.shape[0]
    lane = _iota((tr, LANES), 1)
    e = e_ref[...]
    ohs = [jnp.where(e[:, k:k + 1] == lane, 1.0, 0.0) for k in range(TOP_K)]
    tot = ohs[0]
    for oh in ohs[1:]:
        tot = tot + oh
    lower = jnp.where(_iota((tr, tr), 1) < _iota((tr, tr), 0), 1.0, 0.0).astype(BF16)
    before = _dot(lower, tot.astype(BF16)) + carry_ref[...]
    rank = jnp.zeros((tr, LANES), I32)
    for k in range(TOP_K):
        rk = jnp.sum(ohs[k] * before, axis=-1, keepdims=True).astype(I32)
        rank = jnp.where(lane == k, rk, rank)
    rank_out[...] = rank
    carry_ref[...] = carry_ref[...] + jnp.sum(tot, axis=0, keepdims=True)
    cnt_out[...] = carry_ref[...]


def _route(tope):
    n = tope.shape[0]
    return pl.pallas_call(
        _route_kernel,
        grid=(n // ROUTE_TILE,),
        in_specs=[pl.BlockSpec((ROUTE_TILE, LANES), lambda i: (i, 0))],
        out_specs=[pl.BlockSpec((ROUTE_TILE, LANES), lambda i: (i, 0)), pl.BlockSpec((1, LANES), lambda i: (0, 0))],
        out_shape=[jax.ShapeDtypeStruct((n, LANES), I32), jax.ShapeDtypeStruct((1, LANES), F32)],
        scratch_shapes=[pltpu.VMEM((1, LANES), F32)],
        compiler_params=_cparams(("arbitrary",)),
        name="moe_route",
    )(tope)


def _slots_kernel(e_ref, rank_ref, per_ref, start_ref, slot_out):
    tr = e_ref.shape[0]
    lane = _iota((tr, LANES), 1)
    e = e_ref[...]
    rank = rank_ref[...]
    slot = jnp.zeros((tr, LANES), I32)
    for k in range(TOP_K):
        oh = jnp.where(e[:, k:k + 1] == lane, 1.0, 0.0)
        per_e = jnp.maximum(jnp.sum(oh * per_ref[...], axis=-1, keepdims=True), 1.0)
        start_e = jnp.sum(oh * start_ref[...], axis=-1, keepdims=True)
        rk = rank[:, k:k + 1].astype(F32)
        item = jnp.floor((rk + 0.5) / per_e)
        s = (start_e + item) * float(MOE_ROWS) + (rk - item * per_e)
        slot = jnp.where(lane == k, s.astype(I32), slot)
    slot_out[...] = slot


def _slots(tope, rank, per_row, start_row):
    n = tope.shape[0]
    blk = pl.BlockSpec((ROUTE_TILE, LANES), lambda i: (i, 0))
    row = pl.BlockSpec((1, LANES), lambda i: (0, 0))
    return pl.pallas_call(
        _slots_kernel,
        grid=(n // ROUTE_TILE,),
        in_specs=[blk, blk, row, row],
        out_specs=blk,
        out_shape=jax.ShapeDtypeStruct((n, LANES), I32),
        compiler_params=_cparams(("arbitrary",)),
        name="moe_slots",
    )(tope, rank, per_row, start_row)


DMA_UNROLL = 4


def _dispatch_kernel(slot_ref, h_ref, *refs):
    xbuf, sem = refs[-2], refs[-1]
    tr = h_ref.shape[0]

    def row_copy(i, s):
        return pltpu.make_async_copy(h_ref.at[pl.ds(i, 1)], xbuf.at[pl.ds(s, 1)], sem)

    def issue(i, carry):
        for k in range(TOP_K):
            row_copy(i, slot_ref[i * TOP_K + k]).start(priority=k % 2)
        return carry

    lax.fori_loop(0, tr, issue, 0, unroll=DMA_UNROLL)
    n_rows = tr * TOP_K
    pltpu.make_async_copy(xbuf.at[pl.ds(0, n_rows)], xbuf.at[pl.ds(0, n_rows)], sem).wait()


def _dispatch(slots_flat, h2, xbuf, n_rows, *, tr):
    n, d = h2.shape
    in_specs = [pl.BlockSpec((tr * TOP_K,), lambda i: (i,), memory_space=pltpu.SMEM),
                pl.BlockSpec((tr, d), lambda i: (i, 0))]
    args = [slots_flat, h2]
    aliases = {}
    if xbuf is not None:
        in_specs.append(pl.BlockSpec(memory_space=pl.ANY))
        args.append(xbuf)
        aliases = {2: 0}
    return pl.pallas_call(
        _dispatch_kernel,
        grid=(n // tr,),
        in_specs=in_specs,
        out_specs=pl.BlockSpec(memory_space=pl.ANY),
        out_shape=jax.ShapeDtypeStruct((n_rows, d), h2.dtype),
        scratch_shapes=[pltpu.SemaphoreType.DMA(())],
        input_output_aliases=aliases,
        compiler_params=pltpu.CompilerParams(dimension_semantics=("arbitrary",), vmem_limit_bytes=VMEM_LIMIT,
                                             has_side_effects=True),
        name="moe_dispatch",
    )(*args)


def _experts_kernel(we_ref, wr_ref, wb_ref, x_ref, wg_ref, wu_ref, bg_ref, bu_ref, wd_ref, bd_ref, o_ref, xb_ref):
    del we_ref, wb_ref
    w = pl.program_id(0)
    c = pl.program_id(1)
    rows = wr_ref[w]

    half = x_ref.shape[1]

    @pl.when(rows > 0)
    def _():
        @pl.when(c == 0)
        def _():
            valid = _iota((MOE_ROWS, 1), 0) < rows
            x = x_ref[...]
            lo = lax.bitcast_convert_type(x << 16, F32)
            hi = lax.bitcast_convert_type(x & jnp.uint32(0xFFFF0000), F32)
            xb_ref[:, :half] = jnp.where(valid, lo, 0.0).astype(BF16)
            xb_ref[:, half:] = jnp.where(valid, hi, 0.0).astype(BF16)
            o_ref[...] = jnp.broadcast_to(bd_ref[...], o_ref.shape)

        def sub_tile(start, size, wg, wu, wd):
            rs = slice(start * MOE_SUB, (start + size) * MOE_SUB)
            xs = xb_ref[rs, :]
            g = _dot(xs, wg) + bg_ref[...]
            u = _dot(xs, wu) + bu_ref[...]
            gh = jnp.minimum(g, SWIGLU_LIMIT)
            up = jnp.clip(u, -SWIGLU_LIMIT, SWIGLU_LIMIT)
            act = (up + 1.0) * gh * jax.nn.sigmoid(SWIGLU_ALPHA * gh)
            o_ref[rs, :] = o_ref[rs, :] + _dot(act.astype(BF16), wd)

        n_sub = (rows + MOE_SUB - 1) // MOE_SUB
        for n in range(1, MOE_ROWS // MOE_SUB + 1):
            @pl.when(n_sub == n)
            def _():
                ws = (wg_ref[...].astype(BF16), wu_ref[...].astype(BF16), wd_ref[...].astype(BF16))
                start = 0
                for size in MOE_GROUPS:
                    while n - start >= size:
                        sub_tile(start, size, *ws)
                        start += size


def _experts(work_e, work_rows, work_blk, xbuf, w_gu, b_gu3, w_down, b_down3):
    n_work = work_e.shape[0]
    n_exp, d_model, two_ff = w_gu.shape
    d_ff = two_ff // 2
    nc = d_ff // MOE_FC

    def cidx(w, c, wr):
        return jnp.where(wr[w] > 0, c, nc - 1)

    grid_spec = pltpu.PrefetchScalarGridSpec(
        num_scalar_prefetch=3,
        grid=(n_work, nc),
        in_specs=[
            pl.BlockSpec((MOE_ROWS, d_model // 2), lambda w, c, we, wr, wb: (wb[w], 0)),
            pl.BlockSpec((None, d_model, MOE_FC), lambda w, c, we, wr, wb: (we[w], 0, cidx(w, c, wr))),
            pl.BlockSpec((None, d_model, MOE_FC), lambda w, c, we, wr, wb: (we[w], 0, nc + cidx(w, c, wr))),
            pl.BlockSpec((None, 1, MOE_FC), lambda w, c, we, wr, wb: (we[w], 0, cidx(w, c, wr))),
            pl.BlockSpec((None, 1, MOE_FC), lambda w, c, we, wr, wb: (we[w], 0, nc + cidx(w, c, wr))),
            pl.BlockSpec((None, MOE_FC, d_model), lambda w, c, we, wr, wb: (we[w], cidx(w, c, wr), 0)),
            pl.BlockSpec((None, 1, d_model), lambda w, c, we, wr, wb: (we[w], 0, 0)),
        ],
        out_specs=pl.BlockSpec((MOE_ROWS, d_model), lambda w, c, we, wr, wb: (wb[w], 0)),
        scratch_shapes=[pltpu.VMEM((MOE_ROWS, d_model), BF16)],
    )
    return pl.pallas_call(
        _experts_kernel,
        grid_spec=grid_spec,
        out_shape=jax.ShapeDtypeStruct((xbuf.shape[0], d_model), F32),
        compiler_params=_cparams(("arbitrary", "arbitrary")),
        name="moe_experts",
    )(work_e, work_rows, work_blk, xbuf, w_gu, w_gu, b_gu3, b_gu3, w_down, b_down3)


def _combine_kernel(slot_ref, next_slot_ref, x1_ref, gate_ref, ybuf, o_ref, rows_ref, sems):
    tc = x1_ref.shape[0]
    step = pl.program_id(0)
    cur = step % 2

    def fetch(slots, buf):
        def issue(i, carry):
            for k in range(TOP_K):
                pltpu.make_async_copy(ybuf.at[pl.ds(slots[i * TOP_K + k], 1)], rows_ref.at[buf, k, pl.ds(i, 1)],
                                      sems.at[buf]).start(priority=k % 2)
            return carry

        lax.fori_loop(0, tc, issue, 0, unroll=DMA_UNROLL)

    @pl.when(step == 0)
    def _():
        fetch(slot_ref, 0)

    @pl.when(step + 1 < pl.num_programs(0))
    def _():
        fetch(next_slot_ref, 1 - cur)

    pltpu.make_async_copy(rows_ref.at[cur], rows_ref.at[cur], sems.at[cur]).wait()
    gates = gate_ref[...]
    y = x1_ref[...]
    for k in range(TOP_K):
        y = y + gates[:, k:k + 1] * rows_ref[cur, k]
    o_ref[...] = y


def _combine(slots_flat, x1, gates, ybuf, *, tc):
    n, d = x1.shape
    steps = n // tc
    return pl.pallas_call(
        _combine_kernel,
        grid=(steps,),
        in_specs=[pl.BlockSpec((tc * TOP_K,), lambda i: (i,), memory_space=pltpu.SMEM),
                  pl.BlockSpec((tc * TOP_K,), lambda i: (jnp.minimum(i + 1, steps - 1),), memory_space=pltpu.SMEM),
                  pl.BlockSpec((tc, d), lambda i: (i, 0)),
                  pl.BlockSpec((tc, LANES), lambda i: (i, 0)),
                  pl.BlockSpec(memory_space=pl.ANY)],
        out_specs=pl.BlockSpec((tc, d), lambda i: (i, 0)),
        out_shape=jax.ShapeDtypeStruct((n, d), F32),
        scratch_shapes=[pltpu.VMEM((2, TOP_K, tc, d), F32), pltpu.SemaphoreType.DMA((2,))],
        compiler_params=_cparams(("arbitrary",)),
        name="moe_combine",
    )(slots_flat, slots_flat, x1, gates, ybuf)


def _moe(h2_p, h2_s, tope_p, tope_s, gate_p, gate_s, x1_p, x1_s, w_gu, b_gu, w_down, b_down):
    n_p, n_s = h2_p.shape[0], h2_s.shape[0]
    n_exp = w_gu.shape[0]
    pad = (-(n_p + n_s)) % ROUTE_TILE
    tope_all = jnp.concatenate([tope_p, tope_s, jnp.full((pad, LANES), -1, I32)], axis=0)
    rank, counts = _route(tope_all)
    counts = counts[0, :n_exp].astype(I32)
    n_assign = (n_p + n_s) * TOP_K
    n_work = n_assign // MOE_ROWS + n_exp
    items = (counts + MOE_ROWS - 1) // MOE_ROWS
    per = -(-counts // jnp.maximum(items, 1))
    per = jnp.maximum(-(-per // MOE_SUB) * MOE_SUB, MOE_SUB)
    item_end = jnp.cumsum(items)
    item_start = item_end - items
    n_used = item_end[-1]
    w_ids = jnp.arange(n_work, dtype=I32)
    used = w_ids < n_used
    w_eff = jnp.where(used, w_ids, n_used - 1)
    work_e = jnp.minimum(jnp.searchsorted(item_end, w_eff, side="right"), n_exp - 1).astype(I32)
    work_rows = jnp.clip(counts[work_e] - (w_eff - item_start[work_e]) * per[work_e], 0, per[work_e])
    work_rows = jnp.where(used, work_rows, 0).astype(I32)
    lane_pad = lambda v: jnp.pad(v.astype(F32), (0, LANES - n_exp))[None, :]
    slots = _slots(tope_all, rank, lane_pad(per), lane_pad(item_start))[:n_p + n_s, :TOP_K].reshape(-1)
    slots_p, slots_s = slots[:n_p * TOP_K], slots[n_p * TOP_K:]
    n_rows = n_work * MOE_ROWS
    xbuf = _dispatch(slots_p, h2_p, None, n_rows, tr=512)
    xbuf = _dispatch(slots_s, h2_s, xbuf, n_rows, tr=n_s)
    ybuf = _experts(work_e, work_rows, w_eff.astype(I32), xbuf, w_gu, b_gu[:, None, :], w_down, b_down[:, None, :])
    y_p = _combine(slots_p, x1_p, gate_p, ybuf, tc=256)
    y_s = _combine(slots_s, x1_s, gate_s, ybuf, tc=n_s)
    return y_p, y_s


def _head_perm():
    return [g * Q_PER_KV + r for r in range(Q_PER_KV) for g in range(KV_HEADS)]


def _pack_w_in(w_in):
    d_model = w_in.shape[0]
    pm_width = d_model - ATTN_WIDTH
    s0 = ATTN_WIDTH
    s1 = s0 + 2 * KV_WIDTH
    s2 = s1 + 2 * KV_WIDTH
    s3 = s2 + 2 * KV_WIDTH
    s4 = s3 + 3 * N_HEADS
    wq = w_in[:, :s0].reshape(d_model, N_HEADS, HEAD_DIM)[:, jnp.array(_head_perm())].reshape(d_model, ATTN_WIDTH)
    wg = w_in[:, s3:s4].reshape(d_model, N_HEADS, 3).transpose(0, 2, 1).reshape(d_model, 3 * N_HEADS)
    wg = jnp.pad(wg, ((0, 0), (0, GATE_PAD - 3 * N_HEADS)))
    return jnp.concatenate([wq, w_in[:, s0:s3], w_in[:, s4:s4 + pm_width], wg], axis=1).astype(BF16)


def _cmp_weights(w_cmp_k, b_cmp_k, w_cmp_v, b_cmp_v):
    def half(o):
        wk = jnp.tile(w_cmp_k[o * CMP_STRIDE:(o + 1) * CMP_STRIDE], (1, KV_HEADS))
        wv = jnp.tile(w_cmp_v[o * CMP_STRIDE:(o + 1) * CMP_STRIDE], (1, KV_HEADS))
        return jnp.concatenate([wk, wv], axis=1)
    bias = jnp.concatenate([jnp.tile(b_cmp_k, KV_HEADS), jnp.tile(b_cmp_v, KV_HEADS)])[None, :]
    return half(0), half(1), bias


def _cmp_to_slc_t(n_cmp_pad, n_cmp, n_slc, n_slc_pad):
    i0 = jnp.arange(n_cmp_pad)[None, :] * CMP_STRIDE
    j0 = jnp.arange(n_slc_pad)[:, None] * SLC_BLOCK
    shared = jnp.minimum(i0 + CMP_LEN, j0 + SLC_BLOCK) - jnp.maximum(i0, j0)
    frac = jnp.clip(shared, 0, None).astype(F32) / CMP_LEN
    ok = (jnp.arange(n_cmp_pad)[None, :] < n_cmp) & (jnp.arange(n_slc_pad)[:, None] < n_slc)
    return jnp.where(ok, frac, 0.0).astype(BF16)


def _block_expand(n_blk_pad, n_keys):
    return (jnp.arange(n_blk_pad)[:, None] == (jnp.arange(n_keys)[None, :] // SLC_BLOCK)).astype(BF16)


def _prompt_mixer(x_prompt, p):
    batch, seq, d_model = x_prompt.shape
    n = batch * seq
    q, kvc, kvs, kvw, gate, u, dpool, ks_b, vs_t, kw_b, vw_t = _inproj(
        x_prompt.reshape(n, d_model), p["g1"], p["w_in"], p["qg"], p["ksg"], p["kwg"], tm=512, seq_len=seq)
    npg = seq // PAGE_SIZE
    pt = (jnp.arange(batch, dtype=I32)[:, None] * npg + jnp.arange(npg, dtype=I32)[None, :])
    kc, vc = _compress(pt, kvc.reshape(batch * npg, PAGE_SIZE, 2 * KV_WIDTH), p["cw0"], p["cw1"], p["cb"], p["kcg"],
                       pages_per_step=8)
    n_cmp = seq // CMP_STRIDE - 1
    n_slc = -(-seq // SLC_BLOCK)
    ct = _cmp_to_slc_t(kc.shape[1], n_cmp, n_slc, LANES)
    ext = _block_expand(LANES, seq).T

    o_attn = _prompt_attn(q, gate, kc, jnp.swapaxes(vc, 1, 2), ks_b, vs_t, kw_b, vw_t, ct, ext, batch=batch, seq=seq)
    return o_attn, dpool, kvc, kvs, kvw, u


def _sample_mixer(x_sample, cache_cmp, cache_slc, state_win, state_pool, page_table, p):
    nb, t, d_model = x_sample.shape
    assert t == 1, "decode path handles one new row per sequence"
    npg = page_table.shape[1]
    past = npg * PAGE_SIZE
    qpos = past
    q, kvc, kvs, kvw, gate, u, dpool = _inproj(
        x_sample.reshape(nb, d_model), p["g1"], p["w_in"], p["qg"], p["ksg"], p["kwg"], tm=nb,
        pool_state=state_pool, hist_pos=qpos)
    def rows_on_lanes(a):
        return jnp.transpose(a, (0, 2, 3, 4, 1)).reshape(a.shape[0], 2 * KV_WIDTH, a.shape[1])

    cw0_t = jnp.tile(p["cw0"].T, (1, PAGE_SIZE // CMP_STRIDE))
    cw1_t = jnp.tile(p["cw1"].T, (1, PAGE_SIZE // CMP_STRIDE))
    a0, a1 = _compress_t(page_table, rows_on_lanes(cache_cmp), cw0_t, cw1_t, pages_per_step=LANES // CMP_SUB_PER_PAGE)
    n_cmp = (past + t) // CMP_STRIDE - 1
    n_slc = -(-(past + t) // SLC_BLOCK)
    n_pad = -(-n_slc // LANES) * LANES
    cmat = _cmp_to_slc_t(a0.shape[2], n_cmp, n_slc, n_pad).T
    slopes = jnp.array([SLOPES[g * Q_PER_KV + r] * LOG2E for r in range(Q_PER_KV) for g in range(KV_HEADS)], F32)[:, None]
    q3 = q.reshape(nb, 1, ATTN_WIDTH)
    oc, member = _decode_cmp(q3, slopes, a0, a1, p["cb"].T, p["kcg"][:, :HEAD_DIM].T, cmat, qpos=qpos, n_slc=n_slc)
    rows = jnp.arange(3 * N_HEADS)
    k_i, g_i, r_i = rows // N_HEADS, (rows % N_HEADS) // Q_PER_KV, rows % Q_PER_KV
    col_head = k_i * N_HEADS + r_i * KV_HEADS + g_i
    gexp = (jnp.arange(3 * ATTN_WIDTH)[None, :] // HEAD_DIM == col_head[:, None])
    gexp = jnp.pad(gexp, ((0, GATE_PAD - 3 * N_HEADS), (0, 0))).astype(BF16)
    o = _decode_sw(page_table, q3, slopes, oc, member, gate.reshape(nb, 1, GATE_PAD), gexp,
                   rows_on_lanes(state_win), kvs.reshape(nb, 1, 2 * KV_WIDTH), kvw.reshape(nb, 1, 2 * KV_WIDTH),
                   rows_on_lanes(cache_slc), pages_per_step=16, qpos=qpos, past=past)
    return o.reshape(nb, ATTN_WIDTH), dpool, kvc, kvs, kvw, u


def _prep_params(norm1_g, w_in, q_gain, k_cmp_gain, k_slc_gain, k_win_gain, w_cmp_k, b_cmp_k, w_cmp_v, b_cmp_v):
    cw0, cw1, cb = _cmp_weights(w_cmp_k, b_cmp_k, w_cmp_v, b_cmp_v)
    return {
        "g1": norm1_g[None, :],
        "w_in": _pack_w_in(w_in),
        "qg": jnp.tile(q_gain, N_HEADS)[None, :],
        "ksg": jnp.tile(k_slc_gain, KV_HEADS)[None, :],
        "kwg": jnp.tile(k_win_gain, KV_HEADS)[None, :],
        "kcg": jnp.tile(k_cmp_gain, KV_HEADS)[None, :],
        "cw0": cw0, "cw1": cw1, "cb": cb,
    }


def _prep_out_params(w_pm, s_pm, w_out, norm2_g, w_router, b_router):
    d_model = w_out.shape[0]
    woa = w_out[:ATTN_WIDTH].reshape(N_HEADS, HEAD_DIM, d_model)[jnp.array(_head_perm())].reshape(ATTN_WIDTH, d_model)
    wr = jnp.pad(w_router, ((0, 0), (0, LANES - w_router.shape[1])))
    wrh, wrl = _split_bf16(wr)
    return (w_pm.astype(BF16), s_pm.reshape(1, -1), woa.astype(BF16), w_out[ATTN_WIDTH:].astype(BF16),
            norm2_g[None, :], wrh, wrl, jnp.pad(b_router, (0, LANES - b_router.shape[0]))[None, :])


def _layer(x_prompt, x_sample, cache_cmp, cache_slc, state_win, state_pool, page_table,
           norm1_g, w_in, q_gain, k_cmp_gain, k_slc_gain, k_win_gain, w_cmp_k, b_cmp_k, w_cmp_v, b_cmp_v,
           w_pm, s_pm, w_out, norm2_g, w_router, b_router, w_gu, b_gu, w_down, b_down):
    batch, seq, d_model = x_prompt.shape
    nb, t = x_sample.shape[:2]
    p = _prep_params(norm1_g, w_in, q_gain, k_cmp_gain, k_slc_gain, k_win_gain, w_cmp_k, b_cmp_k, w_cmp_v, b_cmp_v)
    oa_p, d_p, kvc_p, kvs_p, kvw_p, u_p = _prompt_mixer(x_prompt, p)
    oa_s, d_s, kvc_s, kvs_s, kvw_s, u_s = _sample_mixer(x_sample, cache_cmp, cache_slc, state_win, state_pool, page_table, p)
    op = _prep_out_params(w_pm, s_pm, w_out, norm2_g, w_router, b_router)
    x1_p, h2_p, te_p, gt_p = _outproj(x_prompt.reshape(batch * seq, d_model), oa_p, d_p, *op, tm=512)
    x1_s, h2_s, te_s, gt_s = _outproj(x_sample.reshape(nb * t, d_model), oa_s, d_s, *op, tm=nb * t)
    y_p, y_s = _moe(h2_p, h2_s, te_p, te_s, gt_p, gt_s, x1_p, x1_s, w_gu, b_gu, w_down, b_down)
    kv_shape = (2, KV_HEADS, HEAD_DIM)
    n_win = min(WINDOW, seq)
    st_p = (kvc_p.reshape(batch, seq, *kv_shape), kvs_p.reshape(batch, seq, *kv_shape),
            kvw_p.reshape(batch, seq, *kv_shape)[:, seq - n_win:], u_p.reshape(batch, seq, -1)[:, seq - POOL_HIST:])
    kvw_s5 = kvw_s.reshape(nb, t, *kv_shape)
    u_s3 = u_s.reshape(nb, t, -1)
    st_s = (kvc_s.reshape(nb, t, *kv_shape), kvs_s.reshape(nb, t, *kv_shape),
            jnp.concatenate([state_win, kvw_s5], axis=1)[:, t:], jnp.concatenate([state_pool, u_s3], axis=1)[:, t:])
    return y_p.reshape(batch, seq, d_model), y_s.reshape(nb, t, d_model), st_p, st_s


def kernel(x_prompt, x_sample, cache_cmp_kv, cache_slc_kv, state_win_kv, state_pool, page_table, norm1_g, w_in, q_gain, k_cmp_gain, k_slc_gain, k_win_gain, w_cmp_k, b_cmp_k, w_cmp_v, b_cmp_v, w_pm, s_pm, w_out, norm2_g, w_router, b_router, w_gu, b_gu, w_down, b_down):
    layer_params = (norm1_g, w_in, q_gain, k_cmp_gain, k_slc_gain, k_win_gain, w_cmp_k, b_cmp_k, w_cmp_v, b_cmp_v,
                    w_pm, s_pm, w_out, norm2_g, w_router, b_router, w_gu, b_gu, w_down, b_down)
    y_p, y_s = x_prompt, x_sample
    p_states, s_states = [], []
    for layer in range(norm1_g.shape[0]):
        lw = [w[layer] for w in layer_params]
        y_p, y_s, st_p, st_s = _layer(y_p, y_s, cache_cmp_kv[layer], cache_slc_kv[layer], state_win_kv[layer],
                                      state_pool[layer], page_table, *lw)
        p_states.append(st_p)
        s_states.append(st_s)
    stack = lambda states, i: jnp.stack([s[i] for s in states], axis=0)
    return (y_p, y_s,
            stack(p_states, 0), stack(p_states, 1), stack(p_states, 2), stack(p_states, 3),
            stack(s_states, 0), stack(s_states, 1), stack(s_states, 2), stack(s_states, 3))
```
